```python
import math
import jax
import jax.numpy as jnp
from jax import lax
import numpy as np

D_MODEL = 2048
BATCH = 4
SEQ = 4096
DEPTH = 1
DEC_BATCH = 32
DEC_SEQ = 64
PAST_LEN = 2048

CHUNK = 64
Q_BLOCK = 128
D_MIX = D_MODEL
D_HGRN = D_MIX // 2
D_ATTN = D_MIX - D_HGRN
N_HEADS_R = 8
DK_R = D_HGRN // N_HEADS_R
DV_R = D_HGRN // N_HEADS_R
N_HEADS_A = 8
DV_A = D_ATTN // N_HEADS_A
D_QK = DV_A // 2
D_QKA = 4 * N_HEADS_A * D_QK
D_IN = 4 * D_HGRN + D_QKA + D_ATTN
SPLITS = [D_HGRN, 2 * D_HGRN, 3 * D_HGRN, 4 * D_HGRN, 4 * D_HGRN + D_QKA]
N_GROUPS = 4
EXPERTS_PER_GROUP = 8
N_EXPERTS = N_GROUPS * EXPERTS_PER_GROUP
TOP_K = 2
D_EXPERT = D_MODEL // 4
MOE_BLOCK = 128
EPS = 1e-6

kernel_name = 'hymba_hgrn2_diffattn_hmoe_stream_step'


def _rmsnorm(x, g):
    xf = x.astype(jnp.float32)
    y = xf * lax.rsqrt(jnp.mean(xf * xf, axis=-1, keepdims=True) + EPS)
    return (y * g.astype(jnp.float32)).astype(x.dtype)


def _hgrn2_chunk(S, xs):
    q, logf, k, v = xs
    C = q.shape[2]
    b = jnp.cumsum(logf, axis=2)
    causal = jnp.tril(jnp.ones((C, C), dtype=bool))
    rel = b[:, :, :, None, :] - b[:, :, None, :, :]
    decay = jnp.exp(jnp.where(causal[:, :, None], rel, -jnp.inf))
    scores = jnp.einsum('bhtk,bhtsk,bhsk->bhts', q, decay, k)
    o = (jnp.einsum('bhtk,bhkv->bhtv', q * jnp.exp(b), S)
         + jnp.einsum('bhts,bhsv->bhtv', scores, v))
    b_last = b[:, :, -1:, :]
    S_new = (jnp.exp(b_last[:, :, 0, :])[..., None] * S
             + jnp.einsum('bhsk,bhsv->bhkv', k * jnp.exp(b_last - b), v))
    return S_new, o


def _diff_attn_block(q1, q2, k1, k2, v, q_pos, k_pos, lam):
    f32 = jnp.float32
    slopes = 2.0 ** (-8.0 * jnp.arange(1, N_HEADS_A + 1, dtype=f32) / N_HEADS_A)
    allowed = (q_pos[:, None] // CHUNK) >= (k_pos[None, :] // CHUNK)
    dist = jnp.abs(q_pos[:, None] - k_pos[None, :]).astype(f32)
    bias = jnp.where(allowed[None], -slopes[:, None, None] * dist[None], -jnp.inf)
    scale = D_QK ** -0.5
    s1 = jnp.einsum('bhtd,bhsd->bhts', q1.astype(f32), k1.astype(f32)) * scale + bias
    s2 = jnp.einsum('bhtd,bhsd->bhts', q2.astype(f32), k2.astype(f32)) * scale + bias
    a = jax.nn.softmax(s1, axis=-1) - lam * jax.nn.softmax(s2, axis=-1)
    return jnp.einsum('bhts,bhsv->bhtv', a, v.astype(f32))


def _diff_attention(q1, q2, k1, k2, v, q_pos, k_pos, lam):
    B, H, T, _ = q1.shape
    if T % Q_BLOCK == 0:
        nb = T // Q_BLOCK
        blk = lambda a: a.reshape(B, H, nb, Q_BLOCK, a.shape[-1]).transpose(2, 0, 1, 3, 4)
        out = lax.map(lambda xs: _diff_attn_block(xs[0], xs[1], k1, k2, v, xs[2], k_pos, lam),
                      (blk(q1), blk(q2), q_pos.reshape(nb, Q_BLOCK)))
        return out.transpose(1, 2, 0, 3, 4).reshape(B, H, T, DV_A)
    return _diff_attn_block(q1, q2, k1, k2, v, q_pos, k_pos, lam)


def _mixer(h, k_past, v_past, s_past, lb, lam, lam_init, w_in, w_out,
           hgrn_onorm_g, attn_qnorm_g, attn_knorm_g, attn_subln_g):
    B, T, _ = h.shape
    P = k_past.shape[1]
    dt = h.dtype
    f32 = jnp.float32
    proj = jnp.einsum('btd,de->bte', h, w_in)
    q_r, f_r, i_r, g_r, qk_a, v_a = jnp.split(proj, SPLITS, axis=-1)

    heads_r = lambda a: a.reshape(B, T, N_HEADS_R, -1).transpose(0, 2, 1, 3).astype(f32)
    f = heads_r(f_r)
    lb_h = lb.reshape(N_HEADS_R, 1, DK_R)
    logf = jnp.logaddexp(jnp.log(lb_h), jnp.log1p(-lb_h) + jax.nn.log_sigmoid(f))
    k_in = (1.0 - lb_h) * jax.nn.sigmoid(-f)
    q = jax.nn.silu(heads_r(q_r))
    v_in = heads_r(i_r)
    n_c = max(T // CHUNK, 1)
    C = T // n_c
    to_chunks = lambda a: a.reshape(B, N_HEADS_R, n_c, C, a.shape[-1]).transpose(2, 0, 1, 3, 4)
    s_new, o_r = lax.scan(_hgrn2_chunk, s_past.astype(f32),
                          (to_chunks(q), to_chunks(logf), to_chunks(k_in), to_chunks(v_in)))
    o_r = o_r.transpose(1, 2, 0, 3, 4).reshape(B, N_HEADS_R, T, DV_R)
    o_r = _rmsnorm(o_r, hgrn_onorm_g) * jax.nn.silu(heads_r(g_r))
    o_r = o_r.transpose(0, 2, 1, 3).reshape(B, T, D_HGRN).astype(dt)

    qk = qk_a.reshape(B, T, 4, N_HEADS_A, D_QK)
    q1 = _rmsnorm(qk[:, :, 0], attn_qnorm_g)
    q2 = _rmsnorm(qk[:, :, 1], attn_qnorm_g)
    k1 = _rmsnorm(qk[:, :, 2], attn_knorm_g)
    k2 = _rmsnorm(qk[:, :, 3], attn_knorm_g)
    k_new = jnp.concatenate([k1, k2], axis=-1)
    v_new = v_a.reshape(B, T, N_HEADS_A, DV_A)
    k_all = jnp.concatenate([k_past.astype(dt), k_new], axis=1).transpose(0, 2, 1, 3)
    v_all = jnp.concatenate([v_past.astype(dt), v_new], axis=1).transpose(0, 2, 1, 3)
    q_pos = P + jnp.arange(T, dtype=jnp.int32)
    k_pos = jnp.arange(P + T, dtype=jnp.int32)
    o_a = _diff_attention(q1.transpose(0, 2, 1, 3), q2.transpose(0, 2, 1, 3),
                          k_all[..., :D_QK], k_all[..., D_QK:], v_all, q_pos, k_pos, lam)
    o_a = _rmsnorm(o_a, attn_subln_g) * (1.0 - lam_init)
    o_a = o_a.transpose(0, 2, 1, 3).reshape(B, T, D_ATTN).astype(dt)

    y = jnp.einsum('bte,ed->btd', jnp.concatenate([o_r, o_a], axis=-1), w_out)
    return y, k_new, v_new, s_new.astype(dt)


def _hier_moe(h, rg_w, rg_b, re_w, re_b, e_gate, e_up, e_down):
    B, T, D = h.shape
    N = B * T
    f32 = jnp.float32
    t = h.reshape(N, D)
    g_logit = jnp.einsum('nd,dg->ng', t, rg_w).astype(f32) + rg_b.astype(f32)
    g_idx = jnp.argmax(g_logit, axis=-1)
    g_prob = jnp.take_along_axis(jax.nn.softmax(g_logit, axis=-1), g_idx[:, None], axis=-1)
    e_logit = (jnp.einsum('nd,de->ne', t, re_w).astype(f32) + re_b.astype(f32)).reshape(N, N_GROUPS, EXPERTS_PER_GROUP)
    e_in_group = jnp.take_along_axis(e_logit, g_idx[:, None, None], axis=1)[:, 0]
    top_logit, top_idx = lax.top_k(e_in_group, TOP_K)
    slot_w = jax.nn.softmax(top_logit, axis=-1) * g_prob
    slot_e = g_idx[:, None] * EXPERTS_PER_GROUP + top_idx
    S = N * TOP_K
    flat_e = slot_e.reshape(S).astype(jnp.int32)
    flat_w = slot_w.reshape(S)
    flat_tok = jnp.repeat(jnp.arange(N, dtype=jnp.int32), TOP_K)
    order = jnp.argsort(flat_e)
    se, stok, sw = flat_e[order], flat_tok[order], flat_w[order]
    counts = jnp.bincount(flat_e, length=N_EXPERTS)
    padded = (counts + MOE_BLOCK - 1) // MOE_BLOCK * MOE_BLOCK
    pad_end = jnp.cumsum(padded)
    pad_start = pad_end - padded
    seg_start = jnp.cumsum(counts) - counts
    dest = pad_start[se] + (jnp.arange(S, dtype=jnp.int32) - seg_start[se])
    n_blocks = -(-S // MOE_BLOCK) + N_EXPERTS
    L = n_blocks * MOE_BLOCK
    row_tok = jnp.zeros((L,), jnp.int32).at[dest].set(stok)
    row_w = jnp.zeros((L,), f32).at[dest].set(sw)
    block_e = jnp.minimum(jnp.searchsorted(pad_end, jnp.arange(n_blocks, dtype=jnp.int32) * MOE_BLOCK, side='right'),
                          N_EXPERTS - 1)

    def expert_block(args):
        tok, e = args
        xb = t[tok]
        hid = jax.nn.silu(xb @ e_gate[e]) * (xb @ e_up[e])
        return hid @ e_down[e]

    rows = lax.map(expert_block, (row_tok.reshape(n_blocks, MOE_BLOCK), block_e))
    out = jax.ops.segment_sum(rows.reshape(L, D).astype(f32) * row_w[:, None], row_tok, num_segments=N)
    return out.reshape(B, T, D).astype(h.dtype)


def _layer(x, c, k_past, v_past, s_past, lb, lam, lam_init, w_ada, b_ada, norm1_g, norm2_g,
           w_in, w_out, hgrn_onorm_g, attn_qnorm_g, attn_knorm_g, attn_subln_g,
           rg_w, rg_b, re_w, re_b, e_gate, e_up, e_down):
    mod = jnp.einsum('bd,de->be', jax.nn.silu(c), w_ada) + b_ada
    sh_a, sc_a, g_a, sh_f, sc_f, g_f = jnp.split(mod[:, None, :], 6, axis=-1)
    h = _rmsnorm(x, norm1_g) * (1.0 + sc_a) + sh_a
    mix, k_new, v_new, s_new = _mixer(h, k_past, v_past, s_past, lb, lam, lam_init, w_in, w_out,
                                      hgrn_onorm_g, attn_qnorm_g, attn_knorm_g, attn_subln_g)
    x = x + g_a * mix
    h = _rmsnorm(x, norm2_g) * (1.0 + sc_f) + sh_f
    x = x + g_f * _hier_moe(h, rg_w, rg_b, re_w, re_b, e_gate, e_up, e_down)
    return x, k_new, v_new, s_new


def setup_inputs(seed: int = 0) -> dict:
    key = jax.random.key(seed)
    ks = iter(jax.random.split(key, 40))
    f32 = jnp.float32
    D = D_MODEL

    def nrm(shape, scale):
        return jax.random.normal(next(ks), shape, f32) * scale

    def gain(shape):
        return 1.0 + nrm(shape, 0.05)

    return {
        'x_prompt': nrm((BATCH, SEQ, D), 1.0),
        'x_sample': nrm((DEC_BATCH, DEC_SEQ, D), 1.0),
        'cache_k_attn': nrm((DEPTH, DEC_BATCH, PAST_LEN, N_HEADS_A, 2 * D_QK), 1.0),
        'cache_v_attn': nrm((DEPTH, DEC_BATCH, PAST_LEN, N_HEADS_A, DV_A), 1.0),
        'state_hgrn': nrm((DEPTH, DEC_BATCH, N_HEADS_R, DK_R, DV_R), 0.5),
        'c_prompt': nrm((BATCH, D), 1.0),
        'c_sample': nrm((DEC_BATCH, D), 1.0),
        'w_ada': nrm((DEPTH, D, 6 * D), 0.5 * D ** -0.5),
        'b_ada': nrm((DEPTH, 6 * D), 0.02),
        'norm1_g': gain((DEPTH, D)),
        'norm2_g': gain((DEPTH, D)),
        'w_in': nrm((DEPTH, D, D_IN), D ** -0.5),
        'w_out': nrm((DEPTH, D_MIX, D), D_MIX ** -0.5),
        'hgrn_lb_param': nrm((DEPTH + 1, D_HGRN), 0.5),
        'hgrn_onorm_g': gain((DEPTH, DV_R)),
        'attn_qnorm_g': gain((DEPTH, D_QK)),
        'attn_knorm_g': gain((DEPTH, D_QK)),
        'lambda_q1': nrm((DEPTH, D_QK), 0.1),
        'lambda_k1': nrm((DEPTH, D_QK), 0.1),
        'lambda_q2': nrm((DEPTH, D_QK), 0.1),
        'lambda_k2': nrm((DEPTH, D_QK), 0.1),
        'attn_subln_g': gain((DEPTH, DV_A)),
        'router_group_w': nrm((DEPTH, D, N_GROUPS), D ** -0.5),
        'router_group_b': nrm((DEPTH, N_GROUPS), 0.01),
        'router_expert_w': nrm((DEPTH, D, N_EXPERTS), D ** -0.5),
        'router_expert_b': nrm((DEPTH, N_EXPERTS), 0.01),
        'expert_w_gate': nrm((DEPTH, N_EXPERTS, D, D_EXPERT), D ** -0.5),
        'expert_w_up': nrm((DEPTH, N_EXPERTS, D, D_EXPERT), D ** -0.5),
        'expert_w_down': nrm((DEPTH, N_EXPERTS, D_EXPERT, D), D_EXPERT ** -0.5),
    }


def reference(x_prompt, x_sample, cache_k_attn, cache_v_attn, state_hgrn, c_prompt, c_sample,
              w_ada, b_ada, norm1_g, norm2_g, w_in, w_out, hgrn_lb_param, hgrn_onorm_g,
              attn_qnorm_g, attn_knorm_g, lambda_q1, lambda_k1, lambda_q2, lambda_k2, attn_subln_g,
              router_group_w, router_group_b, router_expert_w, router_expert_b,
              expert_w_gate, expert_w_up, expert_w_down):
    f32 = jnp.float32
    dt = x_prompt.dtype
    lb_all = jnp.cumsum(jax.nn.softmax(hgrn_lb_param.astype(f32), axis=0), axis=0)
    k0 = jnp.zeros((x_prompt.shape[0], 0, N_HEADS_A, 2 * D_QK), dt)
    v0 = jnp.zeros((x_prompt.shape[0], 0, N_HEADS_A, DV_A), dt)
    s0 = jnp.zeros((x_prompt.shape[0], N_HEADS_R, DK_R, DV_R), f32)
    xp, xs = x_prompt, x_sample
    kp, vp, sp, ksm, vsm, ssm = [], [], [], [], [], []
    for l in range(DEPTH):
        lam_init = 0.8 - 0.6 * math.exp(-0.3 * l)
        lam = (jnp.exp(jnp.sum(lambda_q1[l].astype(f32) * lambda_k1[l].astype(f32)))
               - jnp.exp(jnp.sum(lambda_q2[l].astype(f32) * lambda_k2[l].astype(f32))) + lam_init)
        weights = (w_ada[l], b_ada[l], norm1_g[l], norm2_g[l], w_in[l], w_out[l], hgrn_onorm_g[l],
                   attn_qnorm_g[l], attn_knorm_g[l], attn_subln_g[l],
                   router_group_w[l], router_group_b[l], router_expert_w[l], router_expert_b[l],
                   expert_w_gate[l], expert_w_up[l], expert_w_down[l])
        xp, k_n, v_n, s_n = _layer(xp, c_prompt, k0, v0, s0, lb_all[l], lam, lam_init, *weights)
        kp.append(k_n)
        vp.append(v_n)
        sp.append(s_n)
        xs, k_n, v_n, s_n = _layer(xs, c_sample, cache_k_attn[l], cache_v_attn[l], state_hgrn[l],
                                   lb_all[l], lam, lam_init, *weights)
        ksm.append(k_n)
        vsm.append(v_n)
        ssm.append(s_n)
    return (xp, xs, jnp.stack(kp), jnp.stack(vp), jnp.stack(sp), jnp.stack(ksm), jnp.stack(vsm), jnp.stack(ssm))
```

```python
import functools
import math

import numpy as np
import jax
import jax.numpy as jnp
from jax import lax
from jax.experimental import pallas as pl
from jax.experimental.pallas import tpu as pltpu

F32 = jnp.float32
BF16 = jnp.bfloat16

N_HEADS = 8
HEAD_DIM = 128
D_QK = 64
CHUNK = 64
N_GROUPS = 4
EXPERTS_PER_GROUP = 8
N_EXPERTS = N_GROUPS * EXPERTS_PER_GROUP
TOP_K = 2
EPS = 1e-6
NEG_BIG = -1e30

LANES = 128
ROW_TILE = 512
ATTN_TILE = 256
SAMPLE_KEY_TILE = 512
MOE_ROWS = 256
VMEM_LIMIT = 56 * 1024 * 1024

HGRN_LEVELS = (32, 16, 8, 4, 2, 1)


def _cparams(sem, vmem=VMEM_LIMIT):
    return pltpu.CompilerParams(dimension_semantics=sem, vmem_limit_bytes=vmem)


def _dot(a, b):
    return jnp.dot(a, b, preferred_element_type=F32)


def _dot_nt(a, b):
    return lax.dot_general(a, b, (((1,), (1,)), ((), ())), preferred_element_type=F32)


def _dot_tn(a, b):
    return lax.dot_general(a, b, (((0,), (0,)), ((), ())), preferred_element_type=F32)


def _silu(x):
    return x / (1.0 + jnp.exp(-x))


def _row_tiling(batch, seq, target):
    if seq >= target:
        assert seq % target == 0
        return 1, target
    bb = max(1, min(batch, target // seq))
    while batch % bb:
        bb -= 1
    return bb, seq


def _ada_kernel(c_ref, w_ref, b_ref, o_ref):
    s = _silu(c_ref[...]).astype(BF16)
    o_ref[...] = _dot(s, w_ref[...].astype(BF16)) + b_ref[...]


def _ada(c_all, w_ada, b_ada):
    rows, d = c_all.shape
    n_out = w_ada.shape[1]
    tn = 1024
    return pl.pallas_call(
        _ada_kernel,
        out_shape=jax.ShapeDtypeStruct((rows, n_out), F32),
        grid=(n_out // tn,),
        in_specs=[pl.BlockSpec((rows, d), lambda j: (0, 0)),
                  pl.BlockSpec((d, tn), lambda j: (0, j)),
                  pl.BlockSpec((1, tn), lambda j: (0, j))],
        out_specs=pl.BlockSpec((rows, tn), lambda j: (0, j)),
        compiler_params=_cparams(("arbitrary",)),
        name="ada_mod",
    )(c_all, w_ada, b_ada.reshape(1, n_out))


def _group_norm64(acc, bd, gain):
    sq = (acc * acc).astype(BF16)
    parts = []
    for t in range(acc.shape[1] // LANES):
        ss = _dot(sq[:, t * LANES:(t + 1) * LANES], bd)
        a = acc[:, t * LANES:(t + 1) * LANES]
        parts.append(a * lax.rsqrt(ss * (1.0 / D_QK) + EPS) * gain)
    return jnp.concatenate(parts, axis=1)


def _in_kernel(x_ref, mod_ref, g1_ref, w_ref, qg_ref, kg_ref, bd_ref,
               q_ref, f_ref, i_ref, g_ref, qa_ref, kf_ref, kb_ref, vf_ref, vb_ref,
               h_scr):
    j = pl.program_id(1)
    tm, d = h_scr.shape

    @pl.when(j == 0)
    def _():
        x = x_ref[...]
        ms = jnp.mean(x * x, axis=-1, keepdims=True)
        xn = x * lax.rsqrt(ms + EPS) * g1_ref[...]
        m = mod_ref[...]
        h = xn * (1.0 + m[:, 1:2, :]) + m[:, 0:1, :]
        h_scr[...] = h.reshape(tm, d).astype(BF16)

    acc = _dot(h_scr[...], w_ref[...])

    @pl.when(j == 0)
    def _():
        q_ref[...] = _silu(acc).astype(BF16)

    @pl.when(j == 1)
    def _():
        f_ref[...] = acc

    @pl.when(j == 2)
    def _():
        i_ref[...] = acc.astype(BF16)

    @pl.when(j == 3)
    def _():
        g_ref[...] = _silu(acc).astype(BF16)

    @pl.when(j == 4)
    def _():
        qa_ref[...] = (_group_norm64(acc, bd_ref[...], qg_ref[...]) * (D_QK ** -0.5)).astype(BF16)

    @pl.when(j == 5)
    def _():
        kn = _group_norm64(acc, bd_ref[...], kg_ref[...])
        kf_ref[...] = kn
        kb_ref[...] = kn.astype(BF16)

    @pl.when(j == 6)
    def _():
        vf_ref[...] = acc
        vb_ref[...] = acc.astype(BF16)


def _in_proj(x, mod, g1, w_in_b, qg, kg, bd):
    batch, seq, d = x.shape
    n = batch * seq
    bb, t = _row_tiling(batch, seq, ROW_TILE)
    tm = bb * t
    tiles_per_mod = (seq // t) if bb == 1 else 1
    sec = 1024
    n_sec = w_in_b.shape[1] // sec
    xv = x.reshape(n // t, t, d)
    row = lambda i, j: (i, 0)
    out_dt = [BF16, F32, BF16, BF16, BF16, F32, BF16, F32, BF16]
    return pl.pallas_call(
        _in_kernel,
        out_shape=[jax.ShapeDtypeStruct((n, sec), dt) for dt in out_dt],
        grid=(n // tm, n_sec),
        in_specs=[pl.BlockSpec((bb, t, d), lambda i, j: (i, 0, 0)),
                  pl.BlockSpec((bb, 6, d), lambda i, j: (i // tiles_per_mod, 0, 0)),
                  pl.BlockSpec((1, d), lambda i, j: (0, 0)),
                  pl.BlockSpec((d, sec), lambda i, j: (0, j)),
                  pl.BlockSpec((1, LANES), lambda i, j: (0, 0)),
                  pl.BlockSpec((1, LANES), lambda i, j: (0, 0)),
                  pl.BlockSpec((LANES, LANES), lambda i, j: (0, 0))],
        out_specs=[pl.BlockSpec((tm, sec), row) for _ in out_dt],
        scratch_shapes=[pltpu.VMEM((tm, d), BF16)],
        compiler_params=_cparams(("arbitrary", "arbitrary")),
        name="in_proj",
    )(xv, mod, g1, w_in_b, qg, kg, bd)


def _hgrn_consts():
    c = CHUNK
    t = np.arange(c)[:, None]
    s = np.arange(c)[None, :]
    sums = [(s <= t), (s > t)]
    masks = []
    for m in HGRN_LEVELS:
        start = (t // (2 * m)) * (2 * m)
        ref = start + m - 1
        lower = (t % (2 * m)) >= m
        sums.append(np.where(lower, (s > ref) & (s <= t), (s > t) & (s <= ref)))
        s_start = (s // (2 * m)) * (2 * m)
        masks.append((s_start == start) & lower & ((s % (2 * m)) < m))
    masks.append(s == t)
    sum_mat = np.concatenate(sums, axis=0).astype(np.float32)
    sum_cat = np.concatenate([sum_mat, sum_mat], axis=1)
    mask_mat = np.concatenate(masks, axis=0).astype(np.float32)
    return jnp.asarray(sum_cat, BF16), jnp.asarray(mask_mat, F32)


def _hgrn_kernel(q_ref, f_ref, i_ref, g_ref, s0_ref, lb_ref, og_ref, sum_ref, mask_ref,
                 o_ref, sout_ref, st_scr, *, n_chunks):
    ci = pl.program_id(1)
    c = CHUNK
    hd = HEAD_DIM

    @pl.when(ci == 0)
    def _():
        for h in range(N_HEADS):
            st_scr[h] = s0_ref[0, h].T

    f = f_ref[...]
    lb = lb_ref[...]
    e = jnp.exp(-jnp.abs(f))
    r = 1.0 / (1.0 + e)
    pos = f >= 0.0
    sig = jnp.where(pos, r, e * r)
    nsig = jnp.where(pos, e * r, r)
    logf = jnp.log(lb + (1.0 - lb) * sig)
    kin = (1.0 - lb) * nsig

    hi = logf.astype(BF16)
    lo = (logf - hi.astype(F32)).astype(BF16)
    expo = _dot(sum_ref[...], jnp.concatenate([hi, lo], axis=0))
    dec = jnp.exp(expo)

    masks = mask_ref[...]
    og = og_ref[...]
    n_lv = len(HGRN_LEVELS)
    for h in range(N_HEADS):
        cs = slice(h * hd, (h + 1) * hd)
        q = q_ref[:, cs].astype(F32)
        k = kin[:, cs]
        v = i_ref[:, cs]
        d0 = dec[0:c, cs]
        d1 = dec[c:2 * c, cs]
        st = st_scr[h]
        o = _dot_nt((q * d0).astype(BF16), st.astype(BF16))
        a = _dot_nt(q.astype(BF16), k.astype(BF16)) * masks[n_lv * c:(n_lv + 1) * c]
        for lv in range(n_lv):
            dl = dec[(2 + lv) * c:(3 + lv) * c, cs]
            a = a + _dot_nt((q * dl).astype(BF16), (k * dl).astype(BF16)) * masks[lv * c:(lv + 1) * c]
        o = o + _dot(a.astype(BF16), v)
        st_scr[h] = st * d0[c - 1:c, :] + _dot_tn(v, (k * d1).astype(BF16))
        ms = jnp.mean(o * o, axis=-1, keepdims=True)
        o = o * lax.rsqrt(ms + EPS) * og * g_ref[:, cs].astype(F32)
        o_ref[:, cs] = o.astype(BF16)

    @pl.when(ci == n_chunks - 1)
    def _():
        for h in range(N_HEADS):
            sout_ref[0, h] = st_scr[h].T


def _hgrn(q, f, i, g, s0, lb, og, batch, seq):
    n, w = q.shape
    nc = seq // CHUNK
    sum_cat, mask_mat = _hgrn_consts()
    row = lambda b, c: (b * nc + c, 0)
    const = lambda b, c: (0, 0)
    return pl.pallas_call(
        functools.partial(_hgrn_kernel, n_chunks=nc),
        out_shape=[jax.ShapeDtypeStruct((n, w), BF16),
                   jax.ShapeDtypeStruct(s0.shape, F32)],
        grid=(batch, nc),
        in_specs=[pl.BlockSpec((CHUNK, w), row),
                  pl.BlockSpec((CHUNK, w), row),
                  pl.BlockSpec((CHUNK, w), row),
                  pl.BlockSpec((CHUNK, w), row),
                  pl.BlockSpec((1,) + s0.shape[1:], lambda b, c: (b, 0, 0, 0)),
                  pl.BlockSpec((1, w), const),
                  pl.BlockSpec((1, HEAD_DIM), const),
                  pl.BlockSpec(sum_cat.shape, const),
                  pl.BlockSpec(mask_mat.shape, const)],
        out_specs=[pl.BlockSpec((CHUNK, w), row),
                   pl.BlockSpec((1,) + s0.shape[1:], lambda b, c: (b, 0, 0, 0))],
        scratch_shapes=[pltpu.VMEM((N_HEADS, HEAD_DIM, HEAD_DIM), F32)],
        compiler_params=_cparams(("arbitrary", "arbitrary")),
        name="hgrn2",
    )(q, f, i, g, s0, lb, og, sum_cat, mask_mat)


def _attn_kernel(*refs, tq, tk, n_past_fn, base_fn, cached, lam_init):
    if cached:
        (q_ref, kn_ref, vn_ref, pk_ref, pv_ref, slope_ref, lq1_ref, lk1_ref, lq2_ref, lk2_ref,
         sg_ref, o_ref) = refs
    else:
        (q_ref, kn_ref, vn_ref, slope_ref, lq1_ref, lk1_ref, lq2_ref, lk2_ref,
         sg_ref, o_ref) = refs
    qi = pl.program_id(2)
    slope = slope_ref[0, :, 0:1]

    qt = q_ref[...]
    lane = lax.broadcasted_iota(jnp.int32, qt.shape, 1)
    zero = jnp.zeros_like(qt)
    qall = jnp.concatenate([jnp.where(lane < D_QK, qt, zero),
                            jnp.where(lane >= D_QK, qt, zero)], axis=0)

    def online(carry, s, v):
        m, l, acc = carry
        m_new = jnp.maximum(m, jnp.max(s, axis=1, keepdims=True))
        alpha = jnp.exp(m - m_new)
        p = jnp.exp(s - m_new)
        l = alpha * l + jnp.sum(p, axis=1, keepdims=True)
        acc = alpha * acc + _dot(p.astype(BF16), v)
        return m_new, l, acc

    base = base_fn(qi)
    col = lax.broadcasted_iota(jnp.int32, (1, tk), 1)

    def past_step(kj, carry):
        if cached:
            k = pk_ref[0, pl.ds(kj * tk, tk), :].astype(BF16)
            v = pv_ref[0, pl.ds(kj * tk, tk), :].astype(BF16)
        else:
            k = kn_ref[pl.ds(kj * tk, tk), :]
            v = vn_ref[pl.ds(kj * tk, tk), :]
        bias = slope * (col + (kj * tk - base)).astype(F32)
        return online(carry, _dot_nt(qall, k) + bias, v)

    init = (jnp.full((2 * tq, 1), NEG_BIG, F32), jnp.zeros((2 * tq, 1), F32),
            jnp.zeros((2 * tq, HEAD_DIM), F32))
    carry = lax.fori_loop(0, n_past_fn(qi), past_step, init)

    if cached:
        kd = kn_ref[...]
        vd = vn_ref[...]
    else:
        kd = kn_ref[pl.ds(qi * tq, tq), :]
        vd = vn_ref[pl.ds(qi * tq, tq), :]
    r = lax.broadcasted_iota(jnp.int32, (2 * tq, tq), 0)
    r = jnp.where(r >= tq, r - tq, r)
    c = lax.broadcasted_iota(jnp.int32, (2 * tq, tq), 1)
    allowed = (r // CHUNK) >= (c // CHUNK)
    bias = slope * (r - jnp.abs(r - c)).astype(F32)
    s = jnp.where(allowed, _dot_nt(qall, kd) + bias, NEG_BIG)
    m, l, acc = online(carry, s, vd)

    lam = (jnp.exp(jnp.sum(lq1_ref[...] * lk1_ref[...], axis=1, keepdims=True))
           - jnp.exp(jnp.sum(lq2_ref[...] * lk2_ref[...], axis=1, keepdims=True)) + lam_init)
    o = acc[:tq] / l[:tq] - lam * (acc[tq:] / l[tq:])
    ms = jnp.mean(o * o, axis=-1, keepdims=True)
    o = o * lax.rsqrt(ms + EPS) * sg_ref[...] * (1.0 - lam_init)
    o_ref[...] = o.astype(BF16)


def _attention(qa, kb, vb, batch, seq, past_k, past_v, lam_params, subln_g, lam_init):
    n, w = qa.shape
    cached = past_k is not None
    slopes = jnp.asarray(
        np.broadcast_to((2.0 ** (-8.0 * np.arange(1, N_HEADS + 1) / N_HEADS))[:, None, None],
                        (N_HEADS, 1, LANES)), F32)
    small = lambda b, h, qi: (0, 0)
    if cached:
        tq = seq
        nq = 1
        p_len = past_k.shape[1]
        tk = min(SAMPLE_KEY_TILE, p_len)
        assert p_len % tk == 0 and p_len % CHUNK == 0
        n_past_fn = lambda qi: p_len // tk
        base_fn = lambda qi: p_len
        kv_spec = pl.BlockSpec((seq, HEAD_DIM), lambda b, h, qi: (b, h))
        past_spec = pl.BlockSpec((1, p_len, HEAD_DIM), lambda b, h, qi: (b, 0, h))
        extra_in = [past_k, past_v]
        extra_specs = [past_spec, past_spec]
    else:
        tq = min(ATTN_TILE, seq)
        tk = tq
        nq = seq // tq
        n_past_fn = lambda qi: qi
        base_fn = lambda qi: qi * tq
        kv_spec = pl.BlockSpec((seq, HEAD_DIM), lambda b, h, qi: (b, h))
        extra_in = []
        extra_specs = []
    kern = functools.partial(_attn_kernel, tq=tq, tk=tk, n_past_fn=n_past_fn, base_fn=base_fn,
                             cached=cached, lam_init=lam_init)
    q_spec = pl.BlockSpec((tq, HEAD_DIM), lambda b, h, qi: (b * nq + qi, h))
    return pl.pallas_call(
        kern,
        out_shape=jax.ShapeDtypeStruct((n, w), BF16),
        grid=(batch, N_HEADS, nq),
        in_specs=[q_spec, kv_spec, kv_spec] + extra_specs + [
            pl.BlockSpec((1, 1, LANES), lambda b, h, qi: (h, 0, 0)),
            pl.BlockSpec((1, D_QK), small), pl.BlockSpec((1, D_QK), small),
            pl.BlockSpec((1, D_QK), small), pl.BlockSpec((1, D_QK), small),
            pl.BlockSpec((1, HEAD_DIM), small)],
        out_specs=q_spec,
        compiler_params=_cparams(("arbitrary", "arbitrary", "arbitrary")),
        name="diff_attn_cached" if cached else "diff_attn",
    )(qa, kb, vb, *extra_in, slopes, *lam_params, subln_g)


def _split_bf16(x):
    hi = x.astype(BF16)
    lo = (x - hi.astype(F32)).astype(BF16)
    return hi, lo


def _out_kernel(or_ref, oa_ref, x_ref, mod_ref, g2_ref, w_ref, rw_ref, rb_ref,
                x1_ref, h2_ref, route_ref):
    bb, t, d = x_ref.shape
    tm = bb * t
    half = or_ref.shape[1]
    mix = _dot(or_ref[...], w_ref[0:half, :]) + _dot(oa_ref[...], w_ref[half:, :])
    m = mod_ref[...]
    x1 = x_ref[...] + m[:, 2:3, :] * mix.reshape(bb, t, d)
    ms = jnp.mean(x1 * x1, axis=-1, keepdims=True)
    h2 = x1 * lax.rsqrt(ms + EPS) * g2_ref[...] * (1.0 + m[:, 4:5, :]) + m[:, 3:4, :]
    x1_ref[...] = x1.reshape(tm, d)
    h2 = h2.reshape(tm, d)
    h2_ref[...] = h2

    hh, hl = _split_bf16(h2)
    wh, wl = _split_bf16(rw_ref[...])
    logits = _dot(hh, wh) + _dot(hl, wh) + _dot(hh, wl) + rb_ref[...]
    lane = lax.broadcasted_iota(jnp.int32, logits.shape, 1)
    neg = jnp.float32(-jnp.inf)
    big = jnp.int32(1 << 20)

    is_g = (lane >= N_EXPERTS) & (lane < N_EXPERTS + N_GROUPS)
    gl = jnp.where(is_g, logits, neg)
    gmax = jnp.max(gl, axis=1, keepdims=True)
    gidx = jnp.min(jnp.where(gl == gmax, lane - N_EXPERTS, big), axis=1, keepdims=True)
    g_prob = 1.0 / jnp.sum(jnp.exp(gl - gmax), axis=1, keepdims=True)

    in_grp = (lane < N_EXPERTS) & ((lane // EXPERTS_PER_GROUP) == gidx)
    el = jnp.where(in_grp, logits, neg)
    m1 = jnp.max(el, axis=1, keepdims=True)
    i1 = jnp.min(jnp.where(el == m1, lane, big), axis=1, keepdims=True)
    el2 = jnp.where(lane == i1, neg, el)
    m2 = jnp.max(el2, axis=1, keepdims=True)
    i2 = jnp.min(jnp.where(el2 == m2, lane, big), axis=1, keepdims=True)
    e21 = jnp.exp(m2 - m1)
    w1 = g_prob / (1.0 + e21)
    w2 = g_prob * e21 / (1.0 + e21)
    route = jnp.where(lane == 0, i1.astype(F32),
                      jnp.where(lane == 1, i2.astype(F32),
                                jnp.where(lane == 2, w1, jnp.where(lane == 3, w2, 0.0))))
    route_ref[...] = route


def _out_proj(o_r, o_a, x, mod, g2, w_out_b, rw, rb):
    batch, seq, d = x.shape
    n = batch * seq
    bb, t = _row_tiling(batch, seq, ROW_TILE)
    tm = bb * t
    tiles_per_mod = (seq // t) if bb == 1 else 1
    half = o_r.shape[1]
    xv = x.reshape(n // t, t, d)
    row = lambda i: (i, 0)
    const = lambda i: (0, 0)
    return pl.pallas_call(
        _out_kernel,
        out_shape=[jax.ShapeDtypeStruct((n, d), F32), jax.ShapeDtypeStruct((n, d), F32),
                   jax.ShapeDtypeStruct((n, LANES), F32)],
        grid=(n // tm,),
        in_specs=[pl.BlockSpec((tm, half), row), pl.BlockSpec((tm, half), row),
                  pl.BlockSpec((bb, t, d), lambda i: (i, 0, 0)),
                  pl.BlockSpec((bb, 6, d), lambda i: (i // tiles_per_mod, 0, 0)),
                  pl.BlockSpec((1, d), const),
                  pl.BlockSpec(w_out_b.shape, const),
                  pl.BlockSpec(rw.shape, const),
                  pl.BlockSpec((1, LANES), const)],
        out_specs=[pl.BlockSpec((tm, d), row), pl.BlockSpec((tm, d), row),
                   pl.BlockSpec((tm, LANES), row)],
        compiler_params=_cparams(("arbitrary",)),
        name="out_proj_router",
    )(o_r, o_a, xv, mod, g2, w_out_b, rw, rb)


def _moe_kernel(ids_ref, be_ref, cnt_ref, nu_ref,
                h2_hbm, wg_ref, wu_ref, wd_ref, rw_ref, out_hbm,
                xbuf, obuf, wgb, wub, wdb, gsem, ssem):
    i = pl.program_id(0)
    n_used = nu_ref[0]
    bm = xbuf.shape[1]
    slot = lax.rem(i, 2)

    def gather_copy(blk, r, s):
        tok = lax.shift_right_logical(jnp.maximum(ids_ref[blk * bm + r], 0), 1)
        return pltpu.make_async_copy(h2_hbm.at[pl.ds(tok, 1)], xbuf.at[s, pl.ds(r, 1)], gsem.at[s])

    def scatter_copy(blk, r, s):
        dst = ids_ref[blk * bm + r]
        return pltpu.make_async_copy(obuf.at[s, pl.ds(r, 1)], out_hbm.at[pl.ds(dst, 1)], ssem.at[s])

    def start_gather(blk, s):
        def body(r, c):
            gather_copy(blk, r, s).start()
            return c
        lax.fori_loop(0, bm, body, 0)

    def wait_gather(blk, s):
        def body(r, c):
            gather_copy(blk, r, s).wait()
            return c
        lax.fori_loop(0, bm, body, 0)

    def start_scatter(blk, s):
        def body(r, c):
            scatter_copy(blk, r, s).start()
            return c
        lax.fori_loop(0, cnt_ref[blk], body, 0)

    def wait_scatter(blk, s):
        def body(r, c):
            scatter_copy(blk, r, s).wait()
            return c
        lax.fori_loop(0, cnt_ref[blk], body, 0)

    @pl.when(i < n_used)
    def _():
        @pl.when(i == 0)
        def _():
            start_gather(0, 0)

        @pl.when(i + 1 < n_used)
        def _():
            start_gather(i + 1, 1 - slot)

        new_expert = jnp.logical_or(i == 0, be_ref[i] != be_ref[jnp.maximum(i - 1, 0)])

        @pl.when(new_expert)
        def _():
            wgb[...] = wg_ref[0].astype(BF16)
            wub[...] = wu_ref[0].astype(BF16)
            wdb[...] = wd_ref[0].astype(BF16)

        wait_gather(i, slot)
        x = xbuf[slot].astype(BF16)
        hid = _silu(_dot(x, wgb[...])) * _dot(x, wub[...])
        y = _dot(hid.astype(BF16), wdb[...]) * rw_ref[...]

        @pl.when(i >= 2)
        def _():
            wait_scatter(i - 2, slot)

        obuf[slot] = y
        start_scatter(i, slot)

        @pl.when(i == n_used - 1)
        def _():
            wait_scatter(i, slot)

            @pl.when(i >= 1)
            def _():
                wait_scatter(i - 1, 1 - slot)


def _moe(h2, route, wg, wu, wd):
    n, d = h2.shape
    s_rows = n * TOP_K
    bm = MOE_ROWS
    n_blk = -(-s_rows // bm) + N_EXPERTS
    n_pad = n_blk * bm

    flat_e = route[:, 0:TOP_K].astype(jnp.int32).reshape(s_rows)
    flat_w = route[:, TOP_K:2 * TOP_K].reshape(s_rows)
    onehot = (flat_e[:, None] == jnp.arange(N_EXPERTS, dtype=jnp.int32)[None, :]).astype(jnp.int32)
    csum = jnp.cumsum(onehot, axis=0)
    rank = jnp.sum((csum - onehot) * onehot, axis=1)
    counts = csum[-1]
    padded = (counts + bm - 1) // bm * bm
    pad_end = jnp.cumsum(padded)
    pad_start = pad_end - padded
    dest = pad_start[flat_e] + rank
    row_id = jnp.full((n_pad,), -1, jnp.int32).at[dest].set(jnp.arange(s_rows, dtype=jnp.int32))
    row_w = jnp.zeros((n_pad,), F32).at[dest].set(flat_w)
    blk_start = jnp.arange(n_blk, dtype=jnp.int32) * bm
    blk_e = jnp.minimum(jnp.searchsorted(pad_end, blk_start, side='right'), N_EXPERTS - 1).astype(jnp.int32)
    blk_cnt = jnp.clip(pad_start[blk_e] + counts[blk_e] - blk_start, 0, bm).astype(jnp.int32)
    n_used = (pad_end[-1] // bm).astype(jnp.int32).reshape(1)
    blk_cnt = jnp.where(jnp.arange(n_blk) < n_used[0], blk_cnt, 0)

    de = wg.shape[2]
    wmap = lambda i, ids, be, cnt, nu: (be[i], 0, 0)
    grid_spec = pltpu.PrefetchScalarGridSpec(
        num_scalar_prefetch=4,
        grid=(n_blk,),
        in_specs=[pl.BlockSpec(memory_space=pl.ANY),
                  pl.BlockSpec((1, d, de), wmap),
                  pl.BlockSpec((1, d, de), wmap),
                  pl.BlockSpec((1, de, d), wmap),
                  pl.BlockSpec((bm, 1), lambda i, ids, be, cnt, nu: (i, 0))],
        out_specs=pl.BlockSpec(memory_space=pl.ANY),
        scratch_shapes=[pltpu.VMEM((2, bm, d), F32), pltpu.VMEM((2, bm, d), F32),
                        pltpu.VMEM((d, de), BF16), pltpu.VMEM((d, de), BF16),
                        pltpu.VMEM((de, d), BF16),
                        pltpu.SemaphoreType.DMA((2,)), pltpu.SemaphoreType.DMA((2,))])
    return pl.pallas_call(
        _moe_kernel,
        out_shape=jax.ShapeDtypeStruct((s_rows, d), F32),
        grid_spec=grid_spec,
        compiler_params=_cparams(("arbitrary",)),
        name="moe_experts",
    )(row_id, blk_e, blk_cnt, n_used, h2, wg, wu, wd, row_w.reshape(n_pad, 1))


def _final_kernel(x1_ref, r_ref, mod_ref, y_ref):
    d = x1_ref.shape[2]
    r = r_ref[...]
    y_ref[...] = x1_ref[...] + mod_ref[...][:, 5:6, :] * (r[:, :, 0:d] + r[:, :, d:])


def _final(x1, rows2, mod, batch, seq, row_off):
    n, d = x1.shape
    bb, t = _row_tiling(batch, seq, ROW_TILE)
    tm = bb * t
    tiles_per_mod = (seq // t) if bb == 1 else 1
    assert row_off % tm == 0 and rows2.shape[0] % (TOP_K * t) == 0
    tile_off = row_off // tm
    return pl.pallas_call(
        _final_kernel,
        out_shape=jax.ShapeDtypeStruct((n // t, t, d), F32),
        grid=(n // tm,),
        in_specs=[pl.BlockSpec((bb, t, d), lambda i: (i, 0, 0)),
                  pl.BlockSpec((bb, t, TOP_K * d), lambda i: (i + tile_off, 0, 0)),
                  pl.BlockSpec((bb, 6, d), lambda i: (i // tiles_per_mod, 0, 0))],
        out_specs=pl.BlockSpec((bb, t, d), lambda i: (i, 0, 0)),
        compiler_params=_cparams(("arbitrary",)),
        name="final_residual",
    )(x1.reshape(n // t, t, d), rows2.reshape(-1, t, TOP_K * d), mod).reshape(batch, seq, d)


def _mixer_group(x, mod, past_k, past_v, s0, lyr):
    batch, seq, d = x.shape
    q, f, i, g, qa, kf, kb, vf, vb = _in_proj(x, mod, lyr["g1"], lyr["w_in"], lyr["qg"], lyr["kg"],
                                               lyr["bd"])
    o_r, s_new = _hgrn(q, f, i, g, s0, lyr["lb"], lyr["og"], batch, seq)
    o_a = _attention(qa, kb, vb, batch, seq, past_k, past_v, lyr["lam_params"], lyr["sg"],
                     lyr["lam_init"])
    x1, h2, route = _out_proj(o_r, o_a, x, mod, lyr["g2"], lyr["w_out"], lyr["rw"], lyr["rb"])
    k_new = kf.reshape(batch, seq, N_HEADS, HEAD_DIM)
    v_new = vf.reshape(batch, seq, N_HEADS, HEAD_DIM)
    return x1, h2, route, k_new, v_new, s_new


def kernel(x_prompt, x_sample, cache_k_attn, cache_v_attn, state_hgrn, c_prompt, c_sample, w_ada, b_ada, norm1_g, norm2_g, w_in, w_out, hgrn_lb_param, hgrn_onorm_g, attn_qnorm_g, attn_knorm_g, lambda_q1, lambda_k1, lambda_q2, lambda_k2, attn_subln_g, router_group_w, router_group_b, router_expert_w, router_expert_b, expert_w_gate, expert_w_up, expert_w_down):
    depth = w_ada.shape[0]
    bp, tp, d = x_prompt.shape
    bs, ts, _ = x_sample.shape
    d_h = N_HEADS * HEAD_DIM
    lb_all = jnp.cumsum(jax.nn.softmax(hgrn_lb_param.astype(F32), axis=0), axis=0)
    bd = jnp.asarray(np.kron(np.eye(LANES // D_QK), np.ones((D_QK, D_QK))), BF16)

    xp, xs = x_prompt, x_sample
    kp, vp, sp, ksm, vsm, ssm = [], [], [], [], [], []
    for l in range(depth):
        wl = w_in[l]
        qk = wl[:, 4 * d_h:4 * d_h + 4 * N_HEADS * D_QK].reshape(d, 2, 2, N_HEADS, D_QK)
        qk = qk.transpose(0, 1, 3, 2, 4).reshape(d, 4 * N_HEADS * D_QK)
        w_in_b = jnp.concatenate([wl[:, :4 * d_h], qk, wl[:, 4 * d_h + 4 * N_HEADS * D_QK:]],
                                 axis=1).astype(BF16)
        rw = jnp.zeros((d, LANES), F32)
        rw = rw.at[:, :N_EXPERTS].set(router_expert_w[l])
        rw = rw.at[:, N_EXPERTS:N_EXPERTS + N_GROUPS].set(router_group_w[l])
        rb = jnp.zeros((1, LANES), F32)
        rb = rb.at[0, :N_EXPERTS].set(router_expert_b[l])
        rb = rb.at[0, N_EXPERTS:N_EXPERTS + N_GROUPS].set(router_group_b[l])
        lyr = dict(
            g1=norm1_g[l].reshape(1, d), g2=norm2_g[l].reshape(1, d),
            w_in=w_in_b, w_out=w_out[l].astype(BF16),
            qg=jnp.tile(attn_qnorm_g[l], 2).reshape(1, LANES),
            kg=jnp.tile(attn_knorm_g[l], 2).reshape(1, LANES),
            bd=bd, lb=lb_all[l].reshape(1, d_h), og=hgrn_onorm_g[l].reshape(1, HEAD_DIM),
            sg=attn_subln_g[l].reshape(1, HEAD_DIM),
            lam_params=[p[l].reshape(1, D_QK) for p in (lambda_q1, lambda_k1, lambda_q2, lambda_k2)],
            lam_init=0.8 - 0.6 * math.exp(-0.3 * l),
            rw=rw, rb=rb)

        c_all = jnp.concatenate([c_prompt, c_sample], axis=0)
        rows = c_all.shape[0]
        rows_pad = -(-rows // 8) * 8
        mod = _ada(jnp.pad(c_all, ((0, rows_pad - rows), (0, 0))), w_ada[l], b_ada[l])
        mod_p = mod[:bp].reshape(bp, 6, d)
        mod_s = mod[bp:bp + bs].reshape(bs, 6, d)

        s0_p = jnp.zeros((bp, N_HEADS, HEAD_DIM, HEAD_DIM), F32)
        x1p, h2p, rp, k_n, v_n, s_n = _mixer_group(xp, mod_p, None, None, s0_p, lyr)
        kp.append(k_n)
        vp.append(v_n)
        sp.append(s_n)
        p_len = cache_k_attn.shape[2]
        pk = cache_k_attn[l].reshape(bs, p_len, d_h)
        pv = cache_v_attn[l].reshape(bs, p_len, d_h)
        x1s, h2s, rs, k_n, v_n, s_n = _mixer_group(xs, mod_s, pk, pv, state_hgrn[l], lyr)
        ksm.append(k_n)
        vsm.append(v_n)
        ssm.append(s_n)

        h2 = jnp.concatenate([h2p, h2s], axis=0)
        route = jnp.concatenate([rp, rs], axis=0)
        rows2 = _moe(h2, route, expert_w_gate[l], expert_w_up[l], expert_w_down[l])
        n_p = bp * tp
        xp = _final(x1p, rows2, mod_p, bp, tp, 0)
        xs = _final(x1s, rows2, mod_s, bs, ts, n_p)
    return (xp, xs, jnp.stack(kp), jnp.stack(vp), jnp.stack(sp),
            jnp.stack(ksm), jnp.stack(vsm), jnp.stack(ssm))
```

```python
import functools
import math

import numpy as np
import jax
import jax.numpy as jnp
from jax import lax
from jax.experimental import pallas as pl
from jax.experimental.pallas import tpu as pltpu

F32 = jnp.float32
BF16 = jnp.bfloat16

N_HEADS = 8
HEAD_DIM = 128
D_QK = 64
CHUNK = 64
N_GROUPS = 4
EXPERTS_PER_GROUP = 8
N_EXPERTS = N_GROUPS * EXPERTS_PER_GROUP
TOP_K = 2
EPS = 1e-6
NEG_BIG = -1e30

LANES = 128
ROW_TILE = 512
ATTN_TILE = 256
SAMPLE_KEY_TILE = 512
MOE_ROWS = 256
VMEM_LIMIT = 56 * 1024 * 1024

HGRN_LEVELS = (32, 16, 8, 4, 2, 1)


def _cparams(sem, vmem=VMEM_LIMIT):
    return pltpu.CompilerParams(dimension_semantics=sem, vmem_limit_bytes=vmem)


def _dot(a, b):
    return jnp.dot(a, b, preferred_element_type=F32)


def _dot_nt(a, b):
    return lax.dot_general(a, b, (((1,), (1,)), ((), ())), preferred_element_type=F32)


def _dot_tn(a, b):
    return lax.dot_general(a, b, (((0,), (0,)), ((), ())), preferred_element_type=F32)


def _silu(x):
    return x / (1.0 + jnp.exp(-x))


def _row_tiling(batch, seq, target):
    if seq >= target:
        assert seq % target == 0
        return 1, target
    bb = max(1, min(batch, target // seq))
    while batch % bb:
        bb -= 1
    return bb, seq


def _ada_kernel(c_ref, w_ref, b_ref, o_ref):
    s = _silu(c_ref[...]).astype(BF16)
    o_ref[...] = _dot(s, w_ref[...].astype(BF16)) + b_ref[...]


def _ada(c_all, w_ada, b_ada):
    rows, d = c_all.shape
    n_out = w_ada.shape[1]
    tn = 1024
    return pl.pallas_call(
        _ada_kernel,
        out_shape=jax.ShapeDtypeStruct((rows, n_out), F32),
        grid=(n_out // tn,),
        in_specs=[pl.BlockSpec((rows, d), lambda j: (0, 0)),
                  pl.BlockSpec((d, tn), lambda j: (0, j)),
                  pl.BlockSpec((1, tn), lambda j: (0, j))],
        out_specs=pl.BlockSpec((rows, tn), lambda j: (0, j)),
        compiler_params=_cparams(("arbitrary",)),
        name="ada_mod",
    )(c_all, w_ada, b_ada.reshape(1, n_out))


def _group_norm64(acc, bd, gain):
    sq = (acc * acc).astype(BF16)
    parts = []
    for t in range(acc.shape[1] // LANES):
        ss = _dot(sq[:, t * LANES:(t + 1) * LANES], bd)
        a = acc[:, t * LANES:(t + 1) * LANES]
        parts.append(a * lax.rsqrt(ss * (1.0 / D_QK) + EPS) * gain)
    return jnp.concatenate(parts, axis=1)


def _in_kernel(x_ref, mod_ref, g1_ref, w_ref, qg_ref, kg_ref, bd_ref,
               q_ref, f_ref, i_ref, g_ref, qa_ref, kf_ref, kb_ref, vf_ref, vb_ref,
               h_scr):
    j = pl.program_id(1)
    tm, d = h_scr.shape

    @pl.when(j == 0)
    def _():
        x = x_ref[...]
        ms = jnp.mean(x * x, axis=-1, keepdims=True)
        xn = x * lax.rsqrt(ms + EPS) * g1_ref[...]
        m = mod_ref[...]
        h = xn * (1.0 + m[:, 1:2, :]) + m[:, 0:1, :]
        h_scr[...] = h.reshape(tm, d).astype(BF16)

    acc = _dot(h_scr[...], w_ref[...])

    @pl.when(j == 0)
    def _():
        q_ref[...] = _silu(acc).astype(BF16)

    @pl.when(j == 1)
    def _():
        f_ref[...] = acc

    @pl.when(j == 2)
    def _():
        i_ref[...] = acc.astype(BF16)

    @pl.when(j == 3)
    def _():
        g_ref[...] = _silu(acc).astype(BF16)

    @pl.when(j == 4)
    def _():
        qa_ref[...] = (_group_norm64(acc, bd_ref[...], qg_ref[...]) * (D_QK ** -0.5)).astype(BF16)

    @pl.when(j == 5)
    def _():
        kn = _group_norm64(acc, bd_ref[...], kg_ref[...])
        kf_ref[...] = kn
        kb_ref[...] = kn.astype(BF16)

    @pl.when(j == 6)
    def _():
        vf_ref[...] = acc
        vb_ref[...] = acc.astype(BF16)


def _in_proj(x, mod, g1, w_in_b, qg, kg, bd):
    batch, seq, d = x.shape
    n = batch * seq
    bb, t = _row_tiling(batch, seq, ROW_TILE)
    tm = bb * t
    tiles_per_mod = (seq // t) if bb == 1 else 1
    sec = 1024
    n_sec = w_in_b.shape[1] // sec
    xv = x.reshape(n // t, t, d)
    row = lambda i, j: (i, 0)
    out_dt = [BF16, F32, BF16, BF16, BF16, F32, BF16, F32, BF16]
    return pl.pallas_call(
        _in_kernel,
        out_shape=[jax.ShapeDtypeStruct((n, sec), dt) for dt in out_dt],
        grid=(n // tm, n_sec),
        in_specs=[pl.BlockSpec((bb, t, d), lambda i, j: (i, 0, 0)),
                  pl.BlockSpec((bb, 6, d), lambda i, j: (i // tiles_per_mod, 0, 0)),
                  pl.BlockSpec((1, d), lambda i, j: (0, 0)),
                  pl.BlockSpec((d, sec), lambda i, j: (0, j)),
                  pl.BlockSpec((1, LANES), lambda i, j: (0, 0)),
                  pl.BlockSpec((1, LANES), lambda i, j: (0, 0)),
                  pl.BlockSpec((LANES, LANES), lambda i, j: (0, 0))],
        out_specs=[pl.BlockSpec((tm, sec), row) for _ in out_dt],
        scratch_shapes=[pltpu.VMEM((tm, d), BF16)],
        compiler_params=_cparams(("arbitrary", "arbitrary")),
        name="in_proj",
    )(xv, mod, g1, w_in_b, qg, kg, bd)


def _hgrn_consts():
    c = CHUNK
    t = np.arange(c)[:, None]
    s = np.arange(c)[None, :]
    sums = [(s <= t), (s > t)]
    masks = []
    for m in HGRN_LEVELS:
        start = (t // (2 * m)) * (2 * m)
        ref = start + m - 1
        lower = (t % (2 * m)) >= m
        sums.append(np.where(lower, (s > ref) & (s <= t), (s > t) & (s <= ref)))
        s_start = (s // (2 * m)) * (2 * m)
        masks.append((s_start == start) & lower & ((s % (2 * m)) < m))
    masks.append(s == t)
    sum_mat = np.concatenate(sums, axis=0).astype(np.float32)
    sum_cat = np.concatenate([sum_mat, sum_mat], axis=1)
    mask_mat = np.concatenate(masks, axis=0).astype(np.float32)
    return jnp.asarray(sum_cat, BF16), jnp.asarray(mask_mat, F32)


def _hgrn_kernel(q_ref, f_ref, i_ref, g_ref, s0_ref, lb_ref, og_ref, sum_ref, mask_ref,
                 o_ref, sout_ref, st_scr, *, n_chunks):
    ci = pl.program_id(1)
    c = CHUNK
    hd = HEAD_DIM

    @pl.when(ci == 0)
    def _():
        for h in range(N_HEADS):
            st_scr[h] = s0_ref[0, h].T

    f = f_ref[...]
    lb = lb_ref[...]
    e = jnp.exp(-jnp.abs(f))
    r = 1.0 / (1.0 + e)
    pos = f >= 0.0
    sig = jnp.where(pos, r, e * r)
    nsig = jnp.where(pos, e * r, r)
    logf = jnp.log(lb + (1.0 - lb) * sig)
    kin = (1.0 - lb) * nsig

    hi = logf.astype(BF16)
    lo = (logf - hi.astype(F32)).astype(BF16)
    expo = _dot(sum_ref[...], jnp.concatenate([hi, lo], axis=0))
    dec = jnp.exp(expo)

    masks = mask_ref[...]
    og = og_ref[...]
    n_lv = len(HGRN_LEVELS)
    for h in range(N_HEADS):
        cs = slice(h * hd, (h + 1) * hd)
        q = q_ref[:, cs].astype(F32)
        k = kin[:, cs]
        v = i_ref[:, cs]
        d0 = dec[0:c, cs]
        d1 = dec[c:2 * c, cs]
        st = st_scr[h]
        o = _dot_nt((q * d0).astype(BF16), st.astype(BF16))
        a = _dot_nt(q.astype(BF16), k.astype(BF16)) * masks[n_lv * c:(n_lv + 1) * c]
        for lv in range(n_lv):
            dl = dec[(2 + lv) * c:(3 + lv) * c, cs]
            a = a + _dot_nt((q * dl).astype(BF16), (k * dl).astype(BF16)) * masks[lv * c:(lv + 1) * c]
        o = o + _dot(a.astype(BF16), v)
        st_scr[h] = st * d0[c - 1:c, :] + _dot_tn(v, (k * d1).astype(BF16))
        ms = jnp.mean(o * o, axis=-1, keepdims=True)
        o = o * lax.rsqrt(ms + EPS) * og * g_ref[:, cs].astype(F32)
        o_ref[:, cs] = o.astype(BF16)

    @pl.when(ci == n_chunks - 1)
    def _():
        for h in range(N_HEADS):
            sout_ref[0, h] = st_scr[h].T


def _hgrn(q, f, i, g, s0, lb, og, batch, seq):
    n, w = q.shape
    nc = seq // CHUNK
    sum_cat, mask_mat = _hgrn_consts()
    row = lambda b, c: (b * nc + c, 0)
    const = lambda b, c: (0, 0)
    return pl.pallas_call(
        functools.partial(_hgrn_kernel, n_chunks=nc),
        out_shape=[jax.ShapeDtypeStruct((n, w), BF16),
                   jax.ShapeDtypeStruct(s0.shape, F32)],
        grid=(batch, nc),
        in_specs=[pl.BlockSpec((CHUNK, w), row),
                  pl.BlockSpec((CHUNK, w), row),
                  pl.BlockSpec((CHUNK, w), row),
                  pl.BlockSpec((CHUNK, w), row),
                  pl.BlockSpec((1,) + s0.shape[1:], lambda b, c: (b, 0, 0, 0)),
                  pl.BlockSpec((1, w), const),
                  pl.BlockSpec((1, HEAD_DIM), const),
                  pl.BlockSpec(sum_cat.shape, const),
                  pl.BlockSpec(mask_mat.shape, const)],
        out_specs=[pl.BlockSpec((CHUNK, w), row),
                   pl.BlockSpec((1,) + s0.shape[1:], lambda b, c: (b, 0, 0, 0))],
        scratch_shapes=[pltpu.VMEM((N_HEADS, HEAD_DIM, HEAD_DIM), F32)],
        compiler_params=_cparams(("arbitrary", "arbitrary")),
        name="hgrn2",
    )(q, f, i, g, s0, lb, og, sum_cat, mask_mat)


def _attn_kernel(*refs, tq, tk, n_past_fn, base_fn, cached, lam_init):
    if cached:
        (q_ref, kn_ref, vn_ref, pk_ref, pv_ref, slope_ref, lq1_ref, lk1_ref, lq2_ref, lk2_ref,
         sg_ref, o_ref) = refs
    else:
        (q_ref, kn_ref, vn_ref, slope_ref, lq1_ref, lk1_ref, lq2_ref, lk2_ref,
         sg_ref, o_ref) = refs
    qi = pl.program_id(2)
    slope = slope_ref[0, :, 0:1]

    qt = q_ref[...]
    lane = lax.broadcasted_iota(jnp.int32, qt.shape, 1)
    zero = jnp.zeros_like(qt)
    qall = jnp.concatenate([jnp.where(lane < D_QK, qt, zero),
                            jnp.where(lane >= D_QK, qt, zero)], axis=0)

    def online(carry, s, v):
        m, l, acc = carry
        m_new = jnp.maximum(m, jnp.max(s, axis=1, keepdims=True))
        alpha = jnp.exp(m - m_new)
        p = jnp.exp(s - m_new)
        l = alpha * l + jnp.sum(p, axis=1, keepdims=True)
        acc = alpha * acc + _dot(p.astype(BF16), v)
        return m_new, l, acc

    base = base_fn(qi)
    col = lax.broadcasted_iota(jnp.int32, (1, tk), 1)

    def past_step(kj, carry):
        if cached:
            k = pk_ref[0, pl.ds(kj * tk, tk), :].astype(BF16)
            v = pv_ref[0, pl.ds(kj * tk, tk), :].astype(BF16)
        else:
            k = kn_ref[pl.ds(kj * tk, tk), :]
            v = vn_ref[pl.ds(kj * tk, tk), :]
        bias = slope * (col + (kj * tk - base)).astype(F32)
        return online(carry, _dot_nt(qall, k) + bias, v)

    init = (jnp.full((2 * tq, 1), NEG_BIG, F32), jnp.zeros((2 * tq, 1), F32),
            jnp.zeros((2 * tq, HEAD_DIM), F32))
    carry = lax.fori_loop(0, n_past_fn(qi), past_step, init)

    if cached:
        kd = kn_ref[...]
        vd = vn_ref[...]
    else:
        kd = kn_ref[pl.ds(qi * tq, tq), :]
        vd = vn_ref[pl.ds(qi * tq, tq), :]
    r = lax.broadcasted_iota(jnp.int32, (2 * tq, tq), 0)
    r = jnp.where(r >= tq, r - tq, r)
    c = lax.broadcasted_iota(jnp.int32, (2 * tq, tq), 1)
    allowed = (r // CHUNK) >= (c // CHUNK)
    bias = slope * (r - jnp.abs(r - c)).astype(F32)
    s = jnp.where(allowed, _dot_nt(qall, kd) + bias, NEG_BIG)
    m, l, acc = online(carry, s, vd)

    lam = (jnp.exp(jnp.sum(lq1_ref[...] * lk1_ref[...], axis=1, keepdims=True))
           - jnp.exp(jnp.sum(lq2_ref[...] * lk2_ref[...], axis=1, keepdims=True)) + lam_init)
    o = acc[:tq] / l[:tq] - lam * (acc[tq:] / l[tq:])
    ms = jnp.mean(o * o, axis=-1, keepdims=True)
    o = o * lax.rsqrt(ms + EPS) * sg_ref[...] * (1.0 - lam_init)
    o_ref[...] = o.astype(BF16)


def _attention(qa, kb, vb, batch, seq, past_k, past_v, lam_params, subln_g, lam_init):
    n, w = qa.shape
    cached = past_k is not None
    slopes = jnp.asarray(
        np.broadcast_to((2.0 ** (-8.0 * np.arange(1, N_HEADS + 1) / N_HEADS))[:, None, None],
                        (N_HEADS, 1, LANES)), F32)
    small = lambda b, h, qi: (0, 0)
    if cached:
        tq = seq
        nq = 1
        p_len = past_k.shape[1]
        tk = min(SAMPLE_KEY_TILE, p_len)
        assert p_len % tk == 0 and p_len % CHUNK == 0
        n_past_fn = lambda qi: p_len // tk
        base_fn = lambda qi: p_len
        kv_spec = pl.BlockSpec((seq, HEAD_DIM), lambda b, h, qi: (b, h))
        past_spec = pl.BlockSpec((1, p_len, HEAD_DIM), lambda b, h, qi: (b, 0, h))
        extra_in = [past_k, past_v]
        extra_specs = [past_spec, past_spec]
    else:
        tq = min(ATTN_TILE, seq)
        tk = tq
        nq = seq // tq
        n_past_fn = lambda qi: qi
        base_fn = lambda qi: qi * tq
        kv_spec = pl.BlockSpec((seq, HEAD_DIM), lambda b, h, qi: (b, h))
        extra_in = []
        extra_specs = []
    kern = functools.partial(_attn_kernel, tq=tq, tk=tk, n_past_fn=n_past_fn, base_fn=base_fn,
                             cached=cached, lam_init=lam_init)
    q_spec = pl.BlockSpec((tq, HEAD_DIM), lambda b, h, qi: (b * nq + qi, h))
    return pl.pallas_call(
        kern,
        out_shape=jax.ShapeDtypeStruct((n, w), BF16),
        grid=(batch, N_HEADS, nq),
        in_specs=[q_spec, kv_spec, kv_spec] + extra_specs + [
            pl.BlockSpec((1, 1, LANES), lambda b, h, qi: (h, 0, 0)),
            pl.BlockSpec((1, D_QK), small), pl.BlockSpec((1, D_QK), small),
            pl.BlockSpec((1, D_QK), small), pl.BlockSpec((1, D_QK), small),
            pl.BlockSpec((1, HEAD_DIM), small)],
        out_specs=q_spec,
        compiler_params=_cparams(("arbitrary", "arbitrary", "arbitrary")),
        name="diff_attn_cached" if cached else "diff_attn",
    )(qa, kb, vb, *extra_in, slopes, *lam_params, subln_g)


def _split_bf16(x):
    hi = x.astype(BF16)
    lo = (x - hi.astype(F32)).astype(BF16)
    return hi, lo


def _out_kernel(orp_ref, oap_ref, xp_ref, modp_ref, ors_ref, oas_ref, xs_ref, mods_ref,
                g2_ref, w_ref, rw_ref, rb_ref, tri_ref,
                x1_ref, h2_ref, route_ref, cnt_ref, cnt_scr, *, n_tiles_p):
    i = pl.program_id(0)

    @pl.when(i == 0)
    def _():
        cnt_scr[...] = jnp.zeros_like(cnt_scr)

    args = (g2_ref, w_ref, rw_ref, rb_ref, tri_ref, x1_ref, h2_ref, route_ref, cnt_scr)

    @pl.when(i < n_tiles_p)
    def _():
        _out_tile(orp_ref, oap_ref, xp_ref, modp_ref, *args)

    @pl.when(i >= n_tiles_p)
    def _():
        _out_tile(ors_ref, oas_ref, xs_ref, mods_ref, *args)

    cnt_ref[...] = jnp.broadcast_to(cnt_scr[...], cnt_ref.shape)


def _out_tile(or_ref, oa_ref, x_ref, mod_ref, g2_ref, w_ref, rw_ref, rb_ref, tri_ref,
              x1_ref, h2_ref, route_ref, cnt_scr):
    bb, t, d = x_ref.shape
    tm = bb * t
    half = or_ref.shape[1]
    mix = _dot(or_ref[...], w_ref[0:half, :]) + _dot(oa_ref[...], w_ref[half:, :])
    m = mod_ref[...]
    x1 = x_ref[...] + m[:, 2:3, :] * mix.reshape(bb, t, d)
    ms = jnp.mean(x1 * x1, axis=-1, keepdims=True)
    h2 = x1 * lax.rsqrt(ms + EPS) * g2_ref[...] * (1.0 + m[:, 4:5, :]) + m[:, 3:4, :]
    x1_ref[...] = x1.reshape(tm, d)
    h2 = h2.reshape(tm, d)
    h2_ref[...] = h2

    hh, hl = _split_bf16(h2)
    wh, wl = _split_bf16(rw_ref[...])
    logits = _dot(hh, wh) + _dot(hl, wh) + _dot(hh, wl) + rb_ref[...]
    lane = lax.broadcasted_iota(jnp.int32, logits.shape, 1)
    neg = jnp.float32(-jnp.inf)
    big = jnp.int32(1 << 20)

    is_g = (lane >= N_EXPERTS) & (lane < N_EXPERTS + N_GROUPS)
    gl = jnp.where(is_g, logits, neg)
    gmax = jnp.max(gl, axis=1, keepdims=True)
    gidx = jnp.min(jnp.where(gl == gmax, lane - N_EXPERTS, big), axis=1, keepdims=True)
    g_prob = 1.0 / jnp.sum(jnp.exp(gl - gmax), axis=1, keepdims=True)

    in_grp = (lane < N_EXPERTS) & ((lane // EXPERTS_PER_GROUP) == gidx)
    el = jnp.where(in_grp, logits, neg)
    m1 = jnp.max(el, axis=1, keepdims=True)
    i1 = jnp.min(jnp.where(el == m1, lane, big), axis=1, keepdims=True)
    el2 = jnp.where(lane == i1, neg, el)
    m2 = jnp.max(el2, axis=1, keepdims=True)
    i2 = jnp.min(jnp.where(el2 == m2, lane, big), axis=1, keepdims=True)
    e21 = jnp.exp(m2 - m1)
    w1 = g_prob / (1.0 + e21)
    w2 = g_prob * e21 / (1.0 + e21)
    tri = tri_ref[...]
    oh1 = jnp.where(lane == i1, 1.0, 0.0)
    oh2 = jnp.where(lane == i2, 1.0, 0.0)
    cnt = cnt_scr[...]
    c1 = jnp.sum(oh1, axis=0, keepdims=True)
    c2 = jnp.sum(oh2, axis=0, keepdims=True)
    r1 = jnp.sum(oh1 * (_dot(tri, oh1.astype(BF16)) + cnt), axis=1, keepdims=True)
    r2 = jnp.sum(oh2 * (_dot(tri, oh2.astype(BF16)) + cnt + c1), axis=1, keepdims=True)
    cnt_scr[...] = cnt + c1 + c2

    vals = (i1.astype(F32), i2.astype(F32), w1, w2, r1, r2)
    route = jnp.zeros(logits.shape, F32)
    for k, v in enumerate(vals):
        route = jnp.where(lane == k, v, route)
    route_ref[...] = route


def _out_proj(o_rp, o_ap, xp, mod_p, o_rs, o_as, xs, mod_s, g2, w_out_b, rw, rb):
    d = xp.shape[2]
    half = o_rp.shape[1]
    n_p = xp.shape[0] * xp.shape[1]
    n_s = xs.shape[0] * xs.shape[1]
    n = n_p + n_s
    bbp, tp = _row_tiling(xp.shape[0], xp.shape[1], ROW_TILE)
    bbs, ts = _row_tiling(xs.shape[0], xs.shape[1], ROW_TILE)
    tm = bbp * tp
    assert bbs * ts == tm and n_p % tm == 0 and n_s % tm == 0
    ntp = n_p // tm
    nts = n_s // tm
    modp_tiles = (xp.shape[1] // tp) if bbp == 1 else 1
    mods_tiles = (xs.shape[1] // ts) if bbs == 1 else 1
    pi = lambda i: jnp.minimum(i, ntp - 1)
    si = lambda i: jnp.maximum(i - ntp, 0)
    tri = jnp.asarray(np.tril(np.ones((tm, tm), np.float32), -1), BF16)
    row = lambda i: (i, 0)
    const = lambda i: (0, 0)
    return pl.pallas_call(
        functools.partial(_out_kernel, n_tiles_p=ntp),
        out_shape=[jax.ShapeDtypeStruct((n, d), F32), jax.ShapeDtypeStruct((n, d), F32),
                   jax.ShapeDtypeStruct((n, LANES), F32), jax.ShapeDtypeStruct((8, LANES), F32)],
        grid=(ntp + nts,),
        in_specs=[pl.BlockSpec((tm, half), lambda i: (pi(i), 0)),
                  pl.BlockSpec((tm, half), lambda i: (pi(i), 0)),
                  pl.BlockSpec((bbp, tp, d), lambda i: (pi(i), 0, 0)),
                  pl.BlockSpec((bbp, 6, d), lambda i: (pi(i) // modp_tiles, 0, 0)),
                  pl.BlockSpec((tm, half), lambda i: (si(i), 0)),
                  pl.BlockSpec((tm, half), lambda i: (si(i), 0)),
                  pl.BlockSpec((bbs, ts, d), lambda i: (si(i), 0, 0)),
                  pl.BlockSpec((bbs, 6, d), lambda i: (si(i) // mods_tiles, 0, 0)),
                  pl.BlockSpec((1, d), const),
                  pl.BlockSpec(w_out_b.shape, const),
                  pl.BlockSpec(rw.shape, const),
                  pl.BlockSpec((1, LANES), const),
                  pl.BlockSpec((tm, tm), const)],
        out_specs=[pl.BlockSpec((tm, d), row), pl.BlockSpec((tm, d), row),
                   pl.BlockSpec((tm, LANES), row), pl.BlockSpec((8, LANES), const)],
        scratch_shapes=[pltpu.VMEM((1, LANES), F32)],
        compiler_params=_cparams(("arbitrary",)),
        name="out_proj_router",
    )(o_rp, o_ap, xp.reshape(n_p // tp, tp, d), mod_p,
      o_rs, o_as, xs.reshape(n_s // ts, ts, d), mod_s, g2, w_out_b, rw, rb, tri)


ROW_PACK_SHIFT = 16


def _moe_kernel(pack_ref, be_ref, nu_ref,
                h2_hbm, wg_ref, wu_ref, wd_ref, out_hbm,
                xbuf, obuf, wgb, wub, wdb, gsem, ssem):
    i = pl.program_id(0)
    n_used = nu_ref[0]
    bm = xbuf.shape[1]
    slot = lax.rem(i, 2)

    def gather_copy(tok, r, s):
        return pltpu.make_async_copy(h2_hbm.at[pl.ds(tok, 1)], xbuf.at[s, pl.ds(r, 1)], gsem.at[s])

    def scatter_copy(dst, r, s):
        return pltpu.make_async_copy(obuf.at[s, pl.ds(r, 1)], out_hbm.at[pl.ds(dst, 1)], ssem.at[s])

    def start_gather(blk, s):
        for r in range(bm):
            tok = lax.shift_right_logical(pack_ref[blk * bm + r], ROW_PACK_SHIFT)
            gather_copy(tok, r, s).start()

    def wait_gather(s):
        for r in range(bm):
            gather_copy(0, r, s).wait()

    def start_scatter(blk, s):
        for r in range(bm):
            dst = pack_ref[blk * bm + r] & ((1 << ROW_PACK_SHIFT) - 1)
            scatter_copy(dst, r, s).start()

    def wait_scatter(s):
        for r in range(bm):
            scatter_copy(0, r, s).wait()

    @pl.when(i < n_used)
    def _():
        @pl.when(i == 0)
        def _():
            start_gather(0, 0)
            obuf[...] = jnp.zeros_like(obuf)
            spare0 = out_hbm.shape[0] - 2 * bm
            fills = [pltpu.make_async_copy(obuf.at[s], out_hbm.at[pl.ds(spare0 + s * bm, bm)],
                                           ssem.at[s]) for s in range(2)]
            for c in fills:
                c.start()
            for c in fills:
                c.wait()

        @pl.when(i + 1 < n_used)
        def _():
            start_gather(i + 1, 1 - slot)

        new_expert = jnp.logical_or(i == 0, be_ref[i] != be_ref[jnp.maximum(i - 1, 0)])

        @pl.when(new_expert)
        def _():
            wgb[...] = wg_ref[0].astype(BF16)
            wub[...] = wu_ref[0].astype(BF16)
            wdb[...] = wd_ref[0].astype(BF16)

        wait_gather(slot)
        x = xbuf[slot].astype(BF16)
        hid = _silu(_dot(x, wgb[...])) * _dot(x, wub[...])
        y = _dot(hid.astype(BF16), wdb[...])

        @pl.when(i >= 2)
        def _():
            wait_scatter(slot)

        obuf[slot] = y
        start_scatter(i, slot)

        @pl.when(i == n_used - 1)
        def _():
            wait_scatter(slot)

            @pl.when(i >= 1)
            def _():
                wait_scatter(1 - slot)


def _moe(h2, route, counts, wg, wu, wd):
    n, d = h2.shape
    assert TOP_K == 2
    s_rows = n * TOP_K
    bm = MOE_ROWS
    n_blk = -(-s_rows // bm) + N_EXPERTS
    n_pad = n_blk * bm
    n_out = s_rows + 2 * bm
    assert n < (1 << (31 - ROW_PACK_SHIFT)) and n_out <= (1 << ROW_PACK_SHIFT)

    counts = counts[0, :N_EXPERTS].astype(jnp.int32)
    padded = (counts + bm - 1) // bm * bm
    pad_end = jnp.cumsum(padded)
    pad_start = pad_end - padded
    expert = route[:, 0:TOP_K].astype(jnp.int32)
    rank = route[:, 2 * TOP_K:3 * TOP_K].astype(jnp.int32)
    onehot = expert[:, :, None] == jnp.arange(N_EXPERTS, dtype=jnp.int32)[None, None, :]
    dest = jnp.sum(jnp.where(onehot, pad_start[None, None, :], 0), axis=2) + rank
    tok = jnp.arange(n, dtype=jnp.int32)
    valid = (tok[:, None] << ROW_PACK_SHIFT) | (jnp.arange(TOP_K, dtype=jnp.int32)[None, :] * n
                                                + tok[:, None])
    pos = jnp.arange(n_pad, dtype=jnp.int32)
    spare = s_rows + ((pos // bm) % 2) * bm + pos % bm
    row_pack = spare.at[dest.reshape(-1)].set(valid.reshape(-1))
    blk_start = jnp.arange(n_blk, dtype=jnp.int32) * bm
    blk_e = jnp.minimum(jnp.sum((pad_end[None, :] <= blk_start[:, None]).astype(jnp.int32), axis=1),
                        N_EXPERTS - 1)
    n_used = (pad_end[-1] // bm).astype(jnp.int32).reshape(1)

    de = wg.shape[2]
    wmap = lambda i, pack, be, nu: (be[i], 0, 0)
    grid_spec = pltpu.PrefetchScalarGridSpec(
        num_scalar_prefetch=3,
        grid=(n_blk,),
        in_specs=[pl.BlockSpec(memory_space=pl.ANY),
                  pl.BlockSpec((1, d, de), wmap),
                  pl.BlockSpec((1, d, de), wmap),
                  pl.BlockSpec((1, de, d), wmap)],
        out_specs=pl.BlockSpec(memory_space=pl.ANY),
        scratch_shapes=[pltpu.VMEM((2, bm, d), F32), pltpu.VMEM((2, bm, d), F32),
                        pltpu.VMEM((d, de), BF16), pltpu.VMEM((d, de), BF16),
                        pltpu.VMEM((de, d), BF16),
                        pltpu.SemaphoreType.DMA((2,)), pltpu.SemaphoreType.DMA((2,))])
    return pl.pallas_call(
        _moe_kernel,
        out_shape=jax.ShapeDtypeStruct((n_out, d), F32),
        grid_spec=grid_spec,
        compiler_params=_cparams(("arbitrary",)),
        name="moe_experts",
    )(row_pack, blk_e, n_used, h2, wg, wu, wd)


def _final_kernel(x1_ref, r0_ref, r1_ref, route_ref, mod_ref, y_ref):
    w = route_ref[...]
    moe = w[:, :, TOP_K:TOP_K + 1] * r0_ref[...] + w[:, :, TOP_K + 1:TOP_K + 2] * r1_ref[...]
    y_ref[...] = x1_ref[...] + mod_ref[...][:, 5:6, :] * moe


def _final(x1, rows, route, mod, batch, seq, row_off):
    n_all, d = x1.shape
    n = batch * seq
    bb, t = _row_tiling(batch, seq, ROW_TILE)
    tm = bb * t
    tiles_per_mod = (seq // t) if bb == 1 else 1
    assert row_off % tm == 0 and n_all % tm == 0 and rows.shape[0] % t == 0
    off = row_off // tm
    slot1 = n_all // tm
    rows_v = rows.reshape(-1, t, d)
    return pl.pallas_call(
        _final_kernel,
        out_shape=jax.ShapeDtypeStruct((n // t, t, d), F32),
        grid=(n // tm,),
        in_specs=[pl.BlockSpec((bb, t, d), lambda i: (i + off, 0, 0)),
                  pl.BlockSpec((bb, t, d), lambda i: (i + off, 0, 0)),
                  pl.BlockSpec((bb, t, d), lambda i: (i + off + slot1, 0, 0)),
                  pl.BlockSpec((bb, t, LANES), lambda i: (i + off, 0, 0)),
                  pl.BlockSpec((bb, 6, d), lambda i: (i // tiles_per_mod, 0, 0))],
        out_specs=pl.BlockSpec((bb, t, d), lambda i: (i, 0, 0)),
        compiler_params=_cparams(("arbitrary",)),
        name="final_residual",
    )(x1.reshape(-1, t, d), rows_v, rows_v, route.reshape(-1, t, LANES), mod).reshape(batch, seq, d)


def _mixer_group(x, mod, past_k, past_v, s0, lyr):
    batch, seq, d = x.shape
    q, f, i, g, qa, kf, kb, vf, vb = _in_proj(x, mod, lyr["g1"], lyr["w_in"], lyr["qg"], lyr["kg"],
                                               lyr["bd"])
    o_r, s_new = _hgrn(q, f, i, g, s0, lyr["lb"], lyr["og"], batch, seq)
    o_a = _attention(qa, kb, vb, batch, seq, past_k, past_v, lyr["lam_params"], lyr["sg"],
                     lyr["lam_init"])
    k_new = kf.reshape(batch, seq, N_HEADS, HEAD_DIM)
    v_new = vf.reshape(batch, seq, N_HEADS, HEAD_DIM)
    return o_r, o_a, k_new, v_new, s_new


def kernel(x_prompt, x_sample, cache_k_attn, cache_v_attn, state_hgrn, c_prompt, c_sample, w_ada, b_ada, norm1_g, norm2_g, w_in, w_out, hgrn_lb_param, hgrn_onorm_g, attn_qnorm_g, attn_knorm_g, lambda_q1, lambda_k1, lambda_q2, lambda_k2, attn_subln_g, router_group_w, router_group_b, router_expert_w, router_expert_b, expert_w_gate, expert_w_up, expert_w_down):
    depth = w_ada.shape[0]
    bp, tp, d = x_prompt.shape
    bs, ts, _ = x_sample.shape
    d_h = N_HEADS * HEAD_DIM
    lb_all = jnp.cumsum(jax.nn.softmax(hgrn_lb_param.astype(F32), axis=0), axis=0)
    bd = jnp.asarray(np.kron(np.eye(LANES // D_QK), np.ones((D_QK, D_QK))), BF16)

    xp, xs = x_prompt, x_sample
    kp, vp, sp, ksm, vsm, ssm = [], [], [], [], [], []
    for l in range(depth):
        wl = w_in[l]
        qk = wl[:, 4 * d_h:4 * d_h + 4 * N_HEADS * D_QK].reshape(d, 2, 2, N_HEADS, D_QK)
        qk = qk.transpose(0, 1, 3, 2, 4).reshape(d, 4 * N_HEADS * D_QK)
        w_in_b = jnp.concatenate([wl[:, :4 * d_h], qk, wl[:, 4 * d_h + 4 * N_HEADS * D_QK:]],
                                 axis=1).astype(BF16)
        rw = jnp.zeros((d, LANES), F32)
        rw = rw.at[:, :N_EXPERTS].set(router_expert_w[l])
        rw = rw.at[:, N_EXPERTS:N_EXPERTS + N_GROUPS].set(router_group_w[l])
        rb = jnp.zeros((1, LANES), F32)
        rb = rb.at[0, :N_EXPERTS].set(router_expert_b[l])
        rb = rb.at[0, N_EXPERTS:N_EXPERTS + N_GROUPS].set(router_group_b[l])
        lyr = dict(
            g1=norm1_g[l].reshape(1, d), g2=norm2_g[l].reshape(1, d),
            w_in=w_in_b, w_out=w_out[l].astype(BF16),
            qg=jnp.tile(attn_qnorm_g[l], 2).reshape(1, LANES),
            kg=jnp.tile(attn_knorm_g[l], 2).reshape(1, LANES),
            bd=bd, lb=lb_all[l].reshape(1, d_h), og=hgrn_onorm_g[l].reshape(1, HEAD_DIM),
            sg=attn_subln_g[l].reshape(1, HEAD_DIM),
            lam_params=[p[l].reshape(1, D_QK) for p in (lambda_q1, lambda_k1, lambda_q2, lambda_k2)],
            lam_init=0.8 - 0.6 * math.exp(-0.3 * l),
            rw=rw, rb=rb)

        c_all = jnp.concatenate([c_prompt, c_sample], axis=0)
        rows = c_all.shape[0]
        rows_pad = -(-rows // 8) * 8
        mod = _ada(jnp.pad(c_all, ((0, rows_pad - rows), (0, 0))), w_ada[l], b_ada[l])
        mod_p = mod[:bp].reshape(bp, 6, d)
        mod_s = mod[bp:bp + bs].reshape(bs, 6, d)

        s0_p = jnp.zeros((bp, N_HEADS, HEAD_DIM, HEAD_DIM), F32)
        o_rp, o_ap, k_n, v_n, s_n = _mixer_group(xp, mod_p, None, None, s0_p, lyr)
        kp.append(k_n)
        vp.append(v_n)
        sp.append(s_n)
        p_len = cache_k_attn.shape[2]
        pk = cache_k_attn[l].reshape(bs, p_len, d_h)
        pv = cache_v_attn[l].reshape(bs, p_len, d_h)
        o_rs, o_as, k_n, v_n, s_n = _mixer_group(xs, mod_s, pk, pv, state_hgrn[l], lyr)
        ksm.append(k_n)
        vsm.append(v_n)
        ssm.append(s_n)

        x1, h2, route, counts = _out_proj(o_rp, o_ap, xp, mod_p, o_rs, o_as, xs, mod_s,
                                          lyr["g2"], lyr["w_out"], lyr["rw"], lyr["rb"])
        rows = _moe(h2, route, counts, expert_w_gate[l], expert_w_up[l], expert_w_down[l])
        xp = _final(x1, rows, route, mod_p, bp, tp, 0)
        xs = _final(x1, rows, route, mod_s, bs, ts, bp * tp)
    return (xp, xs, jnp.stack(kp), jnp.stack(vp), jnp.stack(sp),
            jnp.stack(ksm), jnp.stack(vsm), jnp.stack(ssm))
```

```python
import functools
import math

import numpy as np
import jax
import jax.numpy as jnp
from jax import lax
from jax.experimental import pallas as pl
from jax.experimental.pallas import tpu as pltpu

F32 = jnp.float32
BF16 = jnp.bfloat16

N_HEADS = 8
HEAD_DIM = 128
D_QK = 64
CHUNK = 64
N_GROUPS = 4
EXPERTS_PER_GROUP = 8
N_EXPERTS = N_GROUPS * EXPERTS_PER_GROUP
TOP_K = 2
EPS = 1e-6
NEG_BIG = -1e30
LOG2E = math.log2(math.e)

LANES = 128
ROW_TILE = 512
ATTN_TILE = 512
ATTN_ROWS = 64
SAMPLE_KEY_TILE = 512
MOE_ROWS = 256
VMEM_LIMIT = 56 * 1024 * 1024

HGRN_LEVELS = (32, 16, 8, 4, 2, 1)


def _cparams(sem, vmem=VMEM_LIMIT):
    return pltpu.CompilerParams(dimension_semantics=sem, vmem_limit_bytes=vmem)


def _dot(a, b):
    return jnp.dot(a, b, preferred_element_type=F32)


def _dot_nt(a, b):
    return lax.dot_general(a, b, (((1,), (1,)), ((), ())), preferred_element_type=F32)


def _dot_tn(a, b):
    return lax.dot_general(a, b, (((0,), (0,)), ((), ())), preferred_element_type=F32)


def _silu(x):
    return x / (1.0 + jnp.exp(-x))


def _row_tiling(batch, seq, target):
    if seq >= target:
        assert seq % target == 0
        return 1, target
    bb = max(1, min(batch, target // seq))
    while batch % bb:
        bb -= 1
    return bb, seq


def _ada_kernel(c_ref, w_ref, b_ref, o_ref):
    s = _silu(c_ref[...]).astype(BF16)
    o_ref[...] = _dot(s, w_ref[...].astype(BF16)) + b_ref[...]


def _ada(c_all, w_ada, b_ada):
    rows, d = c_all.shape
    n_out = w_ada.shape[1]
    tn = 1024
    return pl.pallas_call(
        _ada_kernel,
        out_shape=jax.ShapeDtypeStruct((rows, n_out), F32),
        grid=(n_out // tn,),
        in_specs=[pl.BlockSpec((rows, d), lambda j: (0, 0)),
                  pl.BlockSpec((d, tn), lambda j: (0, j)),
                  pl.BlockSpec((1, tn), lambda j: (0, j))],
        out_specs=pl.BlockSpec((rows, tn), lambda j: (0, j)),
        compiler_params=_cparams(("arbitrary",)),
        name="ada_mod",
    )(c_all, w_ada, b_ada.reshape(1, n_out))


def _group_norm64(acc, bd, gain):
    sq = (acc * acc).astype(BF16)
    parts = []
    for t in range(acc.shape[1] // LANES):
        ss = _dot(sq[:, t * LANES:(t + 1) * LANES], bd)
        a = acc[:, t * LANES:(t + 1) * LANES]
        parts.append(a * lax.rsqrt(ss * (1.0 / D_QK) + EPS) * gain)
    return jnp.concatenate(parts, axis=1)


def _in_kernel(x_ref, mod_ref, g1_ref, w_ref, qg_ref, kg_ref, bd_ref,
               q_ref, f_ref, i_ref, g_ref, qa_ref, kf_ref, kb_ref, vf_ref, vb_ref,
               h_scr):
    j = pl.program_id(1)
    tm, d = h_scr.shape

    @pl.when(j == 0)
    def _():
        x = x_ref[...]
        ms = jnp.mean(x * x, axis=-1, keepdims=True)
        xn = x * lax.rsqrt(ms + EPS) * g1_ref[...]
        m = mod_ref[...]
        h = xn * (1.0 + m[:, 1:2, :]) + m[:, 0:1, :]
        h_scr[...] = h.reshape(tm, d).astype(BF16)

    acc = _dot(h_scr[...], w_ref[...])

    @pl.when(j == 0)
    def _():
        q_ref[...] = _silu(acc).astype(BF16)

    @pl.when(j == 1)
    def _():
        f_ref[...] = acc

    @pl.when(j == 2)
    def _():
        i_ref[...] = acc.astype(BF16)

    @pl.when(j == 3)
    def _():
        g_ref[...] = _silu(acc).astype(BF16)

    @pl.when(j == 4)
    def _():
        qa_ref[...] = (_group_norm64(acc, bd_ref[...], qg_ref[...]) * (LOG2E * D_QK ** -0.5)).astype(BF16)

    @pl.when(j == 5)
    def _():
        kn = _group_norm64(acc, bd_ref[...], kg_ref[...])
        kf_ref[...] = kn
        kb_ref[...] = kn.astype(BF16)

    @pl.when(j == 6)
    def _():
        vf_ref[...] = acc
        vb_ref[...] = acc.astype(BF16)


def _in_proj(x, mod, g1, w_in_b, qg, kg, bd):
    batch, seq, d = x.shape
    n = batch * seq
    bb, t = _row_tiling(batch, seq, ROW_TILE)
    tm = bb * t
    tiles_per_mod = (seq // t) if bb == 1 else 1
    sec = 1024
    n_sec = w_in_b.shape[1] // sec
    xv = x.reshape(n // t, t, d)
    row = lambda i, j: (i, 0)
    out_dt = [BF16, F32, BF16, BF16, BF16, F32, BF16, F32, BF16]
    return pl.pallas_call(
        _in_kernel,
        out_shape=[jax.ShapeDtypeStruct((n, sec), dt) for dt in out_dt],
        grid=(n // tm, n_sec),
        in_specs=[pl.BlockSpec((bb, t, d), lambda i, j: (i, 0, 0)),
                  pl.BlockSpec((bb, 6, d), lambda i, j: (i // tiles_per_mod, 0, 0)),
                  pl.BlockSpec((1, d), lambda i, j: (0, 0)),
                  pl.BlockSpec((d, sec), lambda i, j: (0, j)),
                  pl.BlockSpec((1, LANES), lambda i, j: (0, 0)),
                  pl.BlockSpec((1, LANES), lambda i, j: (0, 0)),
                  pl.BlockSpec((LANES, LANES), lambda i, j: (0, 0))],
        out_specs=[pl.BlockSpec((tm, sec), row) for _ in out_dt],
        scratch_shapes=[pltpu.VMEM((tm, d), BF16)],
        compiler_params=_cparams(("arbitrary", "arbitrary")),
        name="in_proj",
    )(xv, mod, g1, w_in_b, qg, kg, bd)


def _hgrn_consts():
    c = CHUNK
    t = np.arange(c)[:, None]
    s = np.arange(c)[None, :]
    sums = [(s <= t), (s > t)]
    masks = []
    for m in HGRN_LEVELS:
        start = (t // (2 * m)) * (2 * m)
        ref = start + m - 1
        lower = (t % (2 * m)) >= m
        sums.append(np.where(lower, (s > ref) & (s <= t), (s > t) & (s <= ref)))
        s_start = (s // (2 * m)) * (2 * m)
        masks.append((s_start == start) & lower & ((s % (2 * m)) < m))
    masks.append(s == t)
    sum_mat = np.concatenate(sums, axis=0).astype(np.float32)
    sum_cat = np.concatenate([sum_mat, sum_mat], axis=1)
    mask_mat = np.concatenate(masks, axis=0).astype(np.float32)
    return jnp.asarray(sum_cat, BF16), jnp.asarray(mask_mat, F32)


def _hgrn_kernel(q_ref, f_ref, i_ref, g_ref, s0_ref, lb_ref, og_ref, sum_ref, mask_ref,
                 o_ref, sout_ref, st_scr, *, n_chunks):
    ci = pl.program_id(1)
    c = CHUNK
    hd = HEAD_DIM

    @pl.when(ci == 0)
    def _():
        for h in range(N_HEADS):
            st_scr[h] = s0_ref[0, h].T

    f = f_ref[...]
    lb = lb_ref[...]
    e = jnp.exp(-jnp.abs(f))
    r = 1.0 / (1.0 + e)
    pos = f >= 0.0
    sig = jnp.where(pos, r, e * r)
    nsig = jnp.where(pos, e * r, r)
    logf = jnp.log(lb + (1.0 - lb) * sig)
    kin = (1.0 - lb) * nsig

    hi = logf.astype(BF16)
    lo = (logf - hi.astype(F32)).astype(BF16)
    expo = _dot(sum_ref[...], jnp.concatenate([hi, lo], axis=0))
    dec = jnp.exp(expo)

    masks = mask_ref[...]
    og = og_ref[...]
    n_lv = len(HGRN_LEVELS)
    for h in range(N_HEADS):
        cs = slice(h * hd, (h + 1) * hd)
        q = q_ref[:, cs].astype(F32)
        k = kin[:, cs]
        v = i_ref[:, cs]
        d0 = dec[0:c, cs]
        d1 = dec[c:2 * c, cs]
        st = st_scr[h]
        o = _dot_nt((q * d0).astype(BF16), st.astype(BF16))
        a = _dot_nt(q.astype(BF16), k.astype(BF16)) * masks[n_lv * c:(n_lv + 1) * c]
        for lv in range(n_lv):
            dl = dec[(2 + lv) * c:(3 + lv) * c, cs]
            a = a + _dot_nt((q * dl).astype(BF16), (k * dl).astype(BF16)) * masks[lv * c:(lv + 1) * c]
        o = o + _dot(a.astype(BF16), v)
        st_scr[h] = st * d0[c - 1:c, :] + _dot_tn(v, (k * d1).astype(BF16))
        ms = jnp.mean(o * o, axis=-1, keepdims=True)
        o = o * lax.rsqrt(ms + EPS) * og * g_ref[:, cs].astype(F32)
        o_ref[:, cs] = o.astype(BF16)

    @pl.when(ci == n_chunks - 1)
    def _():
        for h in range(N_HEADS):
            sout_ref[0, h] = st_scr[h].T


def _hgrn(q, f, i, g, s0, lb, og, batch, seq):
    n, w = q.shape
    nc = seq // CHUNK
    sum_cat, mask_mat = _hgrn_consts()
    row = lambda b, c: (b * nc + c, 0)
    const = lambda b, c: (0, 0)
    return pl.pallas_call(
        functools.partial(_hgrn_kernel, n_chunks=nc),
        out_shape=[jax.ShapeDtypeStruct((n, w), BF16),
                   jax.ShapeDtypeStruct(s0.shape, F32)],
        grid=(batch, nc),
        in_specs=[pl.BlockSpec((CHUNK, w), row),
                  pl.BlockSpec((CHUNK, w), row),
                  pl.BlockSpec((CHUNK, w), row),
                  pl.BlockSpec((CHUNK, w), row),
                  pl.BlockSpec((1,) + s0.shape[1:], lambda b, c: (b, 0, 0, 0)),
                  pl.BlockSpec((1, w), const),
                  pl.BlockSpec((1, HEAD_DIM), const),
                  pl.BlockSpec(sum_cat.shape, const),
                  pl.BlockSpec(mask_mat.shape, const)],
        out_specs=[pl.BlockSpec((CHUNK, w), row),
                   pl.BlockSpec((1,) + s0.shape[1:], lambda b, c: (b, 0, 0, 0))],
        scratch_shapes=[pltpu.VMEM((N_HEADS, HEAD_DIM, HEAD_DIM), F32)],
        compiler_params=_cparams(("arbitrary", "arbitrary")),
        name="hgrn2",
    )(q, f, i, g, s0, lb, og, sum_cat, mask_mat)


def _stack_maps(qt):
    lane = lax.broadcasted_iota(jnp.int32, qt.shape, 1)
    zero = jnp.zeros_like(qt)
    return jnp.concatenate([jnp.where(lane < D_QK, qt, zero), jnp.where(lane >= D_QK, qt, zero)], axis=0)


def _lane_tile(x, width):
    if width < LANES:
        return x[:, :width]
    return jnp.concatenate([x] * (width // LANES), axis=1)


def _softmax_update(s, v, m_ref, l_ref, acc_ref, rows):
    m_prev = m_ref[rows, :]
    m_next = jnp.maximum(m_prev, jnp.max(s, axis=1, keepdims=True))
    alpha = jnp.exp2(m_prev - m_next)
    p = jnp.exp2(s - _lane_tile(m_next, s.shape[1]))
    l_ref[rows, :] = alpha * l_ref[rows, :] + jnp.sum(p, axis=1, keepdims=True)
    acc_ref[rows, :] = alpha * acc_ref[rows, :] + _dot(p.astype(BF16), v)
    m_ref[rows, :] = m_next


def _diag_bias(slope, r0, rows, tq):
    r = r0 + lax.broadcasted_iota(jnp.int32, (rows, tq), 0)
    c = lax.broadcasted_iota(jnp.int32, (rows, tq), 1)
    allowed = (r // CHUNK) >= (c // CHUNK)
    return allowed, slope * (r - jnp.abs(r - c)).astype(F32)


def _lambda(lq1_ref, lk1_ref, lq2_ref, lk2_ref, lam_init):
    return (jnp.exp(jnp.sum(lq1_ref[...] * lk1_ref[...], axis=1, keepdims=True))
            - jnp.exp(jnp.sum(lq2_ref[...] * lk2_ref[...], axis=1, keepdims=True)) + lam_init)


def _attn_finish(acc_ref, l_ref, row0, tq, lam, sg, lam_init):
    a1 = acc_ref[row0:row0 + tq, :] / l_ref[row0:row0 + tq, :]
    a2 = acc_ref[row0 + tq:row0 + 2 * tq, :] / l_ref[row0 + tq:row0 + 2 * tq, :]
    o = a1 - lam * a2
    ms = jnp.mean(o * o, axis=-1, keepdims=True)
    return (o * lax.rsqrt(ms + EPS) * sg * (1.0 - lam_init)).astype(BF16)


def _attn_prompt_kernel(q_ref, k_ref, v_ref, slope_ref, lq1_ref, lk1_ref, lq2_ref, lk2_ref, sg_ref,
                        o_ref, qs, s_a, s_b, p_a, p_b, vx_a, vx_b, al_a, al_b, m_scr, l_scr, acc_scr,
                        *, tq, lam_init):
    qi = pl.program_id(2)
    slope = slope_ref[0, :, 0:1]
    rb = ATTN_ROWS
    qs[...] = _stack_maps(q_ref[...])
    bufs = ((s_a, p_a, vx_a, al_a), (s_b, p_b, vx_b, al_b))
    for _, _, vx, _ in bufs:
        vx[:, HEAD_DIM:] = jnp.ones((tq, LANES), BF16)
    m_scr[...] = jnp.full(m_scr.shape, NEG_BIG, F32)
    l_scr[...] = jnp.zeros(l_scr.shape, F32)
    acc_scr[...] = jnp.zeros(acc_scr.shape, F32)

    def scores(kj):
        return _dot_nt(qs[...], k_ref[pl.ds(kj * tq, tq), :])

    def update(buf, kj, diag):
        s_ref, p_scr, vx, al_scr = bufs[buf]
        if not diag:
            col = lax.broadcasted_iota(jnp.int32, (1, tq), 1)
            bias = slope * (col + (kj - qi) * tq).astype(F32)
        def biased(r0):
            rows = slice(r0, r0 + rb)
            if diag:
                allowed, bias_d = _diag_bias(slope, r0 % tq, rb, tq)
                return jnp.where(allowed, s_ref[rows, :] + bias_d, NEG_BIG)
            return s_ref[rows, :] + bias

        for r0 in range(0, 2 * tq, rb):
            rows = slice(r0, r0 + rb)
            m_prev = m_scr[rows, :]
            m_next = jnp.maximum(m_prev, jnp.max(biased(r0), axis=1, keepdims=True))
            al_scr[rows, :] = jnp.exp2(m_prev - m_next)
            m_scr[rows, :] = m_next
        for r0 in range(0, 2 * tq, rb):
            rows = slice(r0, r0 + rb)
            p_scr[rows, :] = jnp.exp2(biased(r0) - _lane_tile(m_scr[rows, :], tq)).astype(BF16)
        vx[:, 0:HEAD_DIM] = v_ref[pl.ds(kj * tq, tq), :]
        pv = _dot(p_scr[...], vx[...])
        alpha = al_scr[...]
        acc_scr[...] = alpha * acc_scr[...] + pv[:, 0:HEAD_DIM]
        l_scr[...] = alpha * l_scr[...] + pv[:, HEAD_DIM:]

    s_a[...] = scores(0)

    def pair_step(pj, carry):
        kj = 2 * pj
        s_b[...] = scores(kj + 1)
        update(0, kj, False)
        s_a[...] = scores(kj + 2)
        update(1, kj + 1, False)
        return carry

    lax.fori_loop(0, lax.shift_right_logical(qi, 1), pair_step, 0)
    odd = (qi & 1) == 1

    @pl.when(jnp.logical_not(odd))
    def _():
        update(0, qi, True)

    @pl.when(odd)
    def _():
        s_b[...] = scores(qi)
        update(0, qi - 1, False)
        update(1, qi, True)

    lam = _lambda(lq1_ref, lk1_ref, lq2_ref, lk2_ref, lam_init)
    o_ref[...] = _attn_finish(acc_scr, l_scr, 0, tq, lam, sg_ref[...], lam_init)


def _attn_cached_kernel(q_ref, kn_ref, vn_ref, pk_ref, pv_ref, slope_ref,
                        lq1_ref, lk1_ref, lq2_ref, lk2_ref, sg_ref,
                        o_ref, qs, m_scr, l_scr, acc_scr, *, tq, tk, p_len, lam_init):
    hd = HEAD_DIM
    for h in range(N_HEADS):
        qs[2 * tq * h:2 * tq * (h + 1), :] = _stack_maps(q_ref[:, h * hd:(h + 1) * hd])
    m_scr[...] = jnp.full(m_scr.shape, NEG_BIG, F32)
    l_scr[...] = jnp.zeros(l_scr.shape, F32)
    acc_scr[...] = jnp.zeros(acc_scr.shape, F32)
    col = lax.broadcasted_iota(jnp.int32, (1, tk), 1)

    def past_step(kj, carry):
        pos = (col + (kj * tk - p_len)).astype(F32)
        for h in range(N_HEADS):
            rows = slice(2 * tq * h, 2 * tq * (h + 1))
            k = pk_ref[0, pl.ds(kj * (tk * N_HEADS) + h, tk, stride=N_HEADS), :].astype(BF16)
            v = pv_ref[0, pl.ds(kj * (tk * N_HEADS) + h, tk, stride=N_HEADS), :].astype(BF16)
            s = _dot_nt(qs[rows, :], k) + slope_ref[h, :, 0:1] * pos
            _softmax_update(s, v, m_scr, l_scr, acc_scr, rows)
        return carry

    lax.fori_loop(0, p_len // tk, past_step, 0)

    lam = _lambda(lq1_ref, lk1_ref, lq2_ref, lk2_ref, lam_init)
    for h in range(N_HEADS):
        kd = kn_ref[:, h * hd:(h + 1) * hd]
        vd = vn_ref[:, h * hd:(h + 1) * hd]
        allowed, bias = _diag_bias(slope_ref[h, :, 0:1], 0, tq, tq)
        for half in range(2):
            rows = slice(2 * tq * h + half * tq, 2 * tq * h + (half + 1) * tq)
            s = jnp.where(allowed, _dot_nt(qs[rows, :], kd) + bias, NEG_BIG)
            _softmax_update(s, vd, m_scr, l_scr, acc_scr, rows)
        o_ref[:, h * hd:(h + 1) * hd] = _attn_finish(acc_scr, l_scr, 2 * tq * h, tq, lam, sg_ref[...],
                                                     lam_init)


def _alibi_slopes():
    s = (2.0 ** (-8.0 * np.arange(1, N_HEADS + 1) / N_HEADS)) * LOG2E
    return jnp.asarray(np.broadcast_to(s[:, None, None], (N_HEADS, 1, LANES)), F32)


def _attention_prompt(qa, kb, vb, batch, seq, lam_params, subln_g, lam_init):
    n, w = qa.shape
    tq = min(ATTN_TILE, seq)
    nq = seq // tq
    assert seq % tq == 0 and tq % CHUNK == 0
    small = lambda b, h, qi: (0, 0)
    q_spec = pl.BlockSpec((tq, HEAD_DIM), lambda b, h, qi: (b * nq + qi, h))
    kv_spec = pl.BlockSpec((seq, HEAD_DIM), lambda b, h, qi: (b, h))
    return pl.pallas_call(
        functools.partial(_attn_prompt_kernel, tq=tq, lam_init=lam_init),
        out_shape=jax.ShapeDtypeStruct((n, w), BF16),
        grid=(batch, N_HEADS, nq),
        in_specs=[q_spec, kv_spec, kv_spec,
                  pl.BlockSpec((1, 1, LANES), lambda b, h, qi: (h, 0, 0)),
                  pl.BlockSpec((1, D_QK), small), pl.BlockSpec((1, D_QK), small),
                  pl.BlockSpec((1, D_QK), small), pl.BlockSpec((1, D_QK), small),
                  pl.BlockSpec((1, HEAD_DIM), small)],
        out_specs=q_spec,
        scratch_shapes=[pltpu.VMEM((2 * tq, HEAD_DIM), BF16),
                        pltpu.VMEM((2 * tq, tq), F32), pltpu.VMEM((2 * tq, tq), F32),
                        pltpu.VMEM((2 * tq, tq), BF16), pltpu.VMEM((2 * tq, tq), BF16),
                        pltpu.VMEM((tq, HEAD_DIM + LANES), BF16), pltpu.VMEM((tq, HEAD_DIM + LANES), BF16),
                        pltpu.VMEM((2 * tq, LANES), F32), pltpu.VMEM((2 * tq, LANES), F32),
                        pltpu.VMEM((2 * tq, LANES), F32), pltpu.VMEM((2 * tq, LANES), F32),
                        pltpu.VMEM((2 * tq, HEAD_DIM), F32)],
        compiler_params=_cparams(("arbitrary", "arbitrary", "arbitrary")),
        name="diff_attn",
    )(qa, kb, vb, _alibi_slopes(), *lam_params, subln_g)


def _attention_cached(qa, kb, vb, batch, seq, past_k, past_v, lam_params, subln_g, lam_init):
    n, w = qa.shape
    p_len = past_k.shape[1]
    tk = min(SAMPLE_KEY_TILE, p_len)
    assert p_len % tk == 0 and p_len % CHUNK == 0 and seq == CHUNK
    rows = 2 * seq * N_HEADS
    small = lambda b: (0, 0)
    tok_spec = pl.BlockSpec((seq, w), lambda b: (b, 0))
    past_spec = pl.BlockSpec((1, p_len * N_HEADS, HEAD_DIM), lambda b: (b, 0, 0))
    return pl.pallas_call(
        functools.partial(_attn_cached_kernel, tq=seq, tk=tk, p_len=p_len, lam_init=lam_init),
        out_shape=jax.ShapeDtypeStruct((n, w), BF16),
        grid=(batch,),
        in_specs=[tok_spec, tok_spec, tok_spec, past_spec, past_spec,
                  pl.BlockSpec((N_HEADS, 1, LANES), lambda b: (0, 0, 0)),
                  pl.BlockSpec((1, D_QK), small), pl.BlockSpec((1, D_QK), small),
                  pl.BlockSpec((1, D_QK), small), pl.BlockSpec((1, D_QK), small),
                  pl.BlockSpec((1, HEAD_DIM), small)],
        out_specs=tok_spec,
        scratch_shapes=[pltpu.VMEM((rows, HEAD_DIM), BF16), pltpu.VMEM((rows, LANES), F32),
                        pltpu.VMEM((rows, LANES), F32), pltpu.VMEM((rows, HEAD_DIM), F32)],
        compiler_params=_cparams(("arbitrary",)),
        name="diff_attn_cached",
    )(qa, kb, vb, past_k.reshape(batch, p_len * N_HEADS, HEAD_DIM),
      past_v.reshape(batch, p_len * N_HEADS, HEAD_DIM), _alibi_slopes(), *lam_params, subln_g)


def _split_bf16(x):
    hi = x.astype(BF16)
    lo = (x - hi.astype(F32)).astype(BF16)
    return hi, lo


def _out_kernel(orp_ref, oap_ref, xp_ref, modp_ref, ors_ref, oas_ref, xs_ref, mods_ref,
                g2_ref, w_ref, rw_ref, rb_ref, tri_ref,
                x1_ref, h2_ref, route_ref, cnt_ref, cnt_scr, *, n_tiles_p):
    i = pl.program_id(0)

    @pl.when(i == 0)
    def _():
        cnt_scr[...] = jnp.zeros_like(cnt_scr)

    args = (g2_ref, w_ref, rw_ref, rb_ref, tri_ref, x1_ref, h2_ref, route_ref, cnt_scr)

    @pl.when(i < n_tiles_p)
    def _():
        _out_tile(orp_ref, oap_ref, xp_ref, modp_ref, *args)

    @pl.when(i >= n_tiles_p)
    def _():
        _out_tile(ors_ref, oas_ref, xs_ref, mods_ref, *args)

    cnt_ref[...] = jnp.broadcast_to(cnt_scr[...], cnt_ref.shape)


def _out_tile(or_ref, oa_ref, x_ref, mod_ref, g2_ref, w_ref, rw_ref, rb_ref, tri_ref,
              x1_ref, h2_ref, route_ref, cnt_scr):
    bb, t, d = x_ref.shape
    tm = bb * t
    half = or_ref.shape[1]
    mix = _dot(or_ref[...], w_ref[0:half, :]) + _dot(oa_ref[...], w_ref[half:, :])
    m = mod_ref[...]
    x1 = x_ref[...] + m[:, 2:3, :] * mix.reshape(bb, t, d)
    ms = jnp.mean(x1 * x1, axis=-1, keepdims=True)
    h2 = x1 * lax.rsqrt(ms + EPS) * g2_ref[...] * (1.0 + m[:, 4:5, :]) + m[:, 3:4, :]
    x1_ref[...] = x1.reshape(tm, d)
    h2 = h2.reshape(tm, d)
    h2_ref[...] = h2

    hh, hl = _split_bf16(h2)
    wh, wl = _split_bf16(rw_ref[...])
    logits = _dot(hh, wh) + _dot(hl, wh) + _dot(hh, wl) + rb_ref[...]
    lane = lax.broadcasted_iota(jnp.int32, logits.shape, 1)
    neg = jnp.float32(-jnp.inf)
    big = jnp.int32(1 << 20)

    is_g = (lane >= N_EXPERTS) & (lane < N_EXPERTS + N_GROUPS)
    gl = jnp.where(is_g, logits, neg)
    gmax = jnp.max(gl, axis=1, keepdims=True)
    gidx = jnp.min(jnp.where(gl == gmax, lane - N_EXPERTS, big), axis=1, keepdims=True)
    g_prob = 1.0 / jnp.sum(jnp.exp(gl - gmax), axis=1, keepdims=True)

    in_grp = (lane < N_EXPERTS) & ((lane // EXPERTS_PER_GROUP) == gidx)
    el = jnp.where(in_grp, logits, neg)
    m1 = jnp.max(el, axis=1, keepdims=True)
    i1 = jnp.min(jnp.where(el == m1, lane, big), axis=1, keepdims=True)
    el2 = jnp.where(lane == i1, neg, el)
    m2 = jnp.max(el2, axis=1, keepdims=True)
    i2 = jnp.min(jnp.where(el2 == m2, lane, big), axis=1, keepdims=True)
    e21 = jnp.exp(m2 - m1)
    w1 = g_prob / (1.0 + e21)
    w2 = g_prob * e21 / (1.0 + e21)
    tri = tri_ref[...]
    oh1 = jnp.where(lane == i1, 1.0, 0.0)
    oh2 = jnp.where(lane == i2, 1.0, 0.0)
    cnt = cnt_scr[...]
    c1 = jnp.sum(oh1, axis=0, keepdims=True)
    c2 = jnp.sum(oh2, axis=0, keepdims=True)
    r1 = jnp.sum(oh1 * (_dot(tri, oh1.astype(BF16)) + cnt), axis=1, keepdims=True)
    r2 = jnp.sum(oh2 * (_dot(tri, oh2.astype(BF16)) + cnt + c1), axis=1, keepdims=True)
    cnt_scr[...] = cnt + c1 + c2

    vals = (i1.astype(F32), i2.astype(F32), w1, w2, r1, r2)
    route = jnp.zeros(logits.shape, F32)
    for k, v in enumerate(vals):
        route = jnp.where(lane == k, v, route)
    route_ref[...] = route


def _out_proj(o_rp, o_ap, xp, mod_p, o_rs, o_as, xs, mod_s, g2, w_out_b, rw, rb):
    d = xp.shape[2]
    half = o_rp.shape[1]
    n_p = xp.shape[0] * xp.shape[1]
    n_s = xs.shape[0] * xs.shape[1]
    n = n_p + n_s
    bbp, tp = _row_tiling(xp.shape[0], xp.shape[1], ROW_TILE)
    bbs, ts = _row_tiling(xs.shape[0], xs.shape[1], ROW_TILE)
    tm = bbp * tp
    assert bbs * ts == tm and n_p % tm == 0 and n_s % tm == 0
    ntp = n_p // tm
    nts = n_s // tm
    modp_tiles = (xp.shape[1] // tp) if bbp == 1 else 1
    mods_tiles = (xs.shape[1] // ts) if bbs == 1 else 1
    pi = lambda i: jnp.minimum(i, ntp - 1)
    si = lambda i: jnp.maximum(i - ntp, 0)
    tri = jnp.asarray(np.tril(np.ones((tm, tm), np.float32), -1), BF16)
    row = lambda i: (i, 0)
    const = lambda i: (0, 0)
    return pl.pallas_call(
        functools.partial(_out_kernel, n_tiles_p=ntp),
        out_shape=[jax.ShapeDtypeStruct((n, d), F32), jax.ShapeDtypeStruct((n, d), F32),
                   jax.ShapeDtypeStruct((n, LANES), F32), jax.ShapeDtypeStruct((8, LANES), F32)],
        grid=(ntp + nts,),
        in_specs=[pl.BlockSpec((tm, half), lambda i: (pi(i), 0)),
                  pl.BlockSpec((tm, half), lambda i: (pi(i), 0)),
                  pl.BlockSpec((bbp, tp, d), lambda i: (pi(i), 0, 0)),
                  pl.BlockSpec((bbp, 6, d), lambda i: (pi(i) // modp_tiles, 0, 0)),
                  pl.BlockSpec((tm, half), lambda i: (si(i), 0)),
                  pl.BlockSpec((tm, half), lambda i: (si(i), 0)),
                  pl.BlockSpec((bbs, ts, d), lambda i: (si(i), 0, 0)),
                  pl.BlockSpec((bbs, 6, d), lambda i: (si(i) // mods_tiles, 0, 0)),
                  pl.BlockSpec((1, d), const),
                  pl.BlockSpec(w_out_b.shape, const),
                  pl.BlockSpec(rw.shape, const),
                  pl.BlockSpec((1, LANES), const),
                  pl.BlockSpec((tm, tm), const)],
        out_specs=[pl.BlockSpec((tm, d), row), pl.BlockSpec((tm, d), row),
                   pl.BlockSpec((tm, LANES), row), pl.BlockSpec((8, LANES), const)],
        scratch_shapes=[pltpu.VMEM((1, LANES), F32)],
        compiler_params=_cparams(("arbitrary",)),
        name="out_proj_router",
    )(o_rp, o_ap, xp.reshape(n_p // tp, tp, d), mod_p,
      o_rs, o_as, xs.reshape(n_s // ts, ts, d), mod_s, g2, w_out_b, rw, rb, tri)


ROW_PACK_SHIFT = 16


def _moe_kernel(pack_ref, be_ref, nu_ref,
                h2_hbm, wg_ref, wu_ref, wd_ref, out_hbm,
                xbuf, obuf, wgb, wub, wdb, gsem, ssem):
    i = pl.program_id(0)
    n_used = nu_ref[0]
    bm = xbuf.shape[1]
    slot = lax.rem(i, 2)

    def gather_copy(tok, r, s):
        return pltpu.make_async_copy(h2_hbm.at[pl.ds(tok, 1)], xbuf.at[s, pl.ds(r, 1)], gsem.at[s])

    def scatter_copy(dst, r, s):
        return pltpu.make_async_copy(obuf.at[s, pl.ds(r, 1)], out_hbm.at[pl.ds(dst, 1)], ssem.at[s])

    def start_gather(blk, s):
        for r in range(bm):
            tok = lax.shift_right_logical(pack_ref[blk * bm + r], ROW_PACK_SHIFT)
            gather_copy(tok, r, s).start()

    def wait_gather(s):
        for r in range(bm):
            gather_copy(0, r, s).wait()

    def start_scatter(blk, s):
        for r in range(bm):
            dst = pack_ref[blk * bm + r] & ((1 << ROW_PACK_SHIFT) - 1)
            scatter_copy(dst, r, s).start()

    def wait_scatter(s):
        for r in range(bm):
            scatter_copy(0, r, s).wait()

    @pl.when(i < n_used)
    def _():
        @pl.when(i == 0)
        def _():
            start_gather(0, 0)
            obuf[...] = jnp.zeros_like(obuf)
            spare0 = out_hbm.shape[0] - 2 * bm
            fills = [pltpu.make_async_copy(obuf.at[s], out_hbm.at[pl.ds(spare0 + s * bm, bm)],
                                           ssem.at[s]) for s in range(2)]
            for c in fills:
                c.start()
            for c in fills:
                c.wait()

        @pl.when(i + 1 < n_used)
        def _():
            start_gather(i + 1, 1 - slot)

        new_expert = jnp.logical_or(i == 0, be_ref[i] != be_ref[jnp.maximum(i - 1, 0)])

        @pl.when(new_expert)
        def _():
            wgb[...] = wg_ref[0].astype(BF16)
            wub[...] = wu_ref[0].astype(BF16)
            wdb[...] = wd_ref[0].astype(BF16)

        wait_gather(slot)
        x = xbuf[slot].astype(BF16)
        hid = _silu(_dot(x, wgb[...])) * _dot(x, wub[...])
        y = _dot(hid.astype(BF16), wdb[...])

        @pl.when(i >= 2)
        def _():
            wait_scatter(slot)

        obuf[slot] = y
        start_scatter(i, slot)

        @pl.when(i == n_used - 1)
        def _():
            wait_scatter(slot)

            @pl.when(i >= 1)
            def _():
                wait_scatter(1 - slot)


def _moe(h2, route, counts, wg, wu, wd):
    n, d = h2.shape
    assert TOP_K == 2
    s_rows = n * TOP_K
    bm = MOE_ROWS
    n_blk = -(-s_rows // bm) + N_EXPERTS
    n_pad = n_blk * bm
    n_out = s_rows + 2 * bm
    assert n < (1 << (31 - ROW_PACK_SHIFT)) and n_out <= (1 << ROW_PACK_SHIFT)

    counts = counts[0, :N_EXPERTS].astype(jnp.int32)
    padded = (counts + bm - 1) // bm * bm
    pad_end = jnp.cumsum(padded)
    pad_start = pad_end - padded
    expert = route[:, 0:TOP_K].astype(jnp.int32)
    rank = route[:, 2 * TOP_K:3 * TOP_K].astype(jnp.int32)
    onehot = expert[:, :, None] == jnp.arange(N_EXPERTS, dtype=jnp.int32)[None, None, :]
    dest = jnp.sum(jnp.where(onehot, pad_start[None, None, :], 0), axis=2) + rank
    tok = jnp.arange(n, dtype=jnp.int32)
    valid = (tok[:, None] << ROW_PACK_SHIFT) | (jnp.arange(TOP_K, dtype=jnp.int32)[None, :] * n
                                                + tok[:, None])
    pos = jnp.arange(n_pad, dtype=jnp.int32)
    spare = s_rows + ((pos // bm) % 2) * bm + pos % bm
    row_pack = spare.at[dest.reshape(-1)].set(valid.reshape(-1))
    blk_start = jnp.arange(n_blk, dtype=jnp.int32) * bm
    blk_e = jnp.minimum(jnp.sum((pad_end[None, :] <= blk_start[:, None]).astype(jnp.int32), axis=1),
                        N_EXPERTS - 1)
    n_used = (pad_end[-1] // bm).astype(jnp.int32).reshape(1)

    de = wg.shape[2]
    wmap = lambda i, pack, be, nu: (be[i], 0, 0)
    grid_spec = pltpu.PrefetchScalarGridSpec(
        num_scalar_prefetch=3,
        grid=(n_blk,),
        in_specs=[pl.BlockSpec(memory_space=pl.ANY),
                  pl.BlockSpec((1, d, de), wmap),
                  pl.BlockSpec((1, d, de), wmap),
                  pl.BlockSpec((1, de, d), wmap)],
        out_specs=pl.BlockSpec(memory_space=pl.ANY),
        scratch_shapes=[pltpu.VMEM((2, bm, d), F32), pltpu.VMEM((2, bm, d), F32),
                        pltpu.VMEM((d, de), BF16), pltpu.VMEM((d, de), BF16),
                        pltpu.VMEM((de, d), BF16),
                        pltpu.SemaphoreType.DMA((2,)), pltpu.SemaphoreType.DMA((2,))])
    return pl.pallas_call(
        _moe_kernel,
        out_shape=jax.ShapeDtypeStruct((n_out, d), F32),
        grid_spec=grid_spec,
        compiler_params=_cparams(("arbitrary",)),
        name="moe_experts",
    )(row_pack, blk_e, n_used, h2, wg, wu, wd)


def _final_kernel(x1_ref, r0_ref, r1_ref, route_ref, mod_ref, y_ref):
    w = route_ref[...]
    moe = w[:, :, TOP_K:TOP_K + 1] * r0_ref[...] + w[:, :, TOP_K + 1:TOP_K + 2] * r1_ref[...]
    y_ref[...] = x1_ref[...] + mod_ref[...][:, 5:6, :] * moe


def _final(x1, rows, route, mod, batch, seq, row_off):
    n_all, d = x1.shape
    n = batch * seq
    bb, t = _row_tiling(batch, seq, ROW_TILE)
    tm = bb * t
    tiles_per_mod = (seq // t) if bb == 1 else 1
    assert row_off % tm == 0 and n_all % tm == 0 and rows.shape[0] % t == 0
    off = row_off // tm
    slot1 = n_all // tm
    rows_v = rows.reshape(-1, t, d)
    return pl.pallas_call(
        _final_kernel,
        out_shape=jax.ShapeDtypeStruct((n // t, t, d), F32),
        grid=(n // tm,),
        in_specs=[pl.BlockSpec((bb, t, d), lambda i: (i + off, 0, 0)),
                  pl.BlockSpec((bb, t, d), lambda i: (i + off, 0, 0)),
                  pl.BlockSpec((bb, t, d), lambda i: (i + off + slot1, 0, 0)),
                  pl.BlockSpec((bb, t, LANES), lambda i: (i + off, 0, 0)),
                  pl.BlockSpec((bb, 6, d), lambda i: (i // tiles_per_mod, 0, 0))],
        out_specs=pl.BlockSpec((bb, t, d), lambda i: (i, 0, 0)),
        compiler_params=_cparams(("arbitrary",)),
        name="final_residual",
    )(x1.reshape(-1, t, d), rows_v, rows_v, route.reshape(-1, t, LANES), mod).reshape(batch, seq, d)


def _mixer_group(x, mod, past_k, past_v, s0, lyr):
    batch, seq, d = x.shape
    q, f, i, g, qa, kf, kb, vf, vb = _in_proj(x, mod, lyr["g1"], lyr["w_in"], lyr["qg"], lyr["kg"],
                                               lyr["bd"])
    o_r, s_new = _hgrn(q, f, i, g, s0, lyr["lb"], lyr["og"], batch, seq)
    if past_k is None:
        o_a = _attention_prompt(qa, kb, vb, batch, seq, lyr["lam_params"], lyr["sg"], lyr["lam_init"])
    else:
        o_a = _attention_cached(qa, kb, vb, batch, seq, past_k, past_v, lyr["lam_params"], lyr["sg"],
                                lyr["lam_init"])
    k_new = kf.reshape(batch, seq, N_HEADS, HEAD_DIM)
    v_new = vf.reshape(batch, seq, N_HEADS, HEAD_DIM)
    return o_r, o_a, k_new, v_new, s_new


def kernel(x_prompt, x_sample, cache_k_attn, cache_v_attn, state_hgrn, c_prompt, c_sample, w_ada, b_ada, norm1_g, norm2_g, w_in, w_out, hgrn_lb_param, hgrn_onorm_g, attn_qnorm_g, attn_knorm_g, lambda_q1, lambda_k1, lambda_q2, lambda_k2, attn_subln_g, router_group_w, router_group_b, router_expert_w, router_expert_b, expert_w_gate, expert_w_up, expert_w_down):
    depth = w_ada.shape[0]
    bp, tp, d = x_prompt.shape
    bs, ts, _ = x_sample.shape
    d_h = N_HEADS * HEAD_DIM
    lb_all = jnp.cumsum(jax.nn.softmax(hgrn_lb_param.astype(F32), axis=0), axis=0)
    bd = jnp.asarray(np.kron(np.eye(LANES // D_QK), np.ones((D_QK, D_QK))), BF16)

    xp, xs = x_prompt, x_sample
    kp, vp, sp, ksm, vsm, ssm = [], [], [], [], [], []
    for l in range(depth):
        wl = w_in[l]
        qk = wl[:, 4 * d_h:4 * d_h + 4 * N_HEADS * D_QK].reshape(d, 2, 2, N_HEADS, D_QK)
        qk = qk.transpose(0, 1, 3, 2, 4).reshape(d, 4 * N_HEADS * D_QK)
        w_in_b = jnp.concatenate([wl[:, :4 * d_h], qk, wl[:, 4 * d_h + 4 * N_HEADS * D_QK:]],
                                 axis=1).astype(BF16)
        rw = jnp.zeros((d, LANES), F32)
        rw = rw.at[:, :N_EXPERTS].set(router_expert_w[l])
        rw = rw.at[:, N_EXPERTS:N_EXPERTS + N_GROUPS].set(router_group_w[l])
        rb = jnp.zeros((1, LANES), F32)
        rb = rb.at[0, :N_EXPERTS].set(router_expert_b[l])
        rb = rb.at[0, N_EXPERTS:N_EXPERTS + N_GROUPS].set(router_group_b[l])
        lyr = dict(
            g1=norm1_g[l].reshape(1, d), g2=norm2_g[l].reshape(1, d),
            w_in=w_in_b, w_out=w_out[l].astype(BF16),
            qg=jnp.tile(attn_qnorm_g[l], 2).reshape(1, LANES),
            kg=jnp.tile(attn_knorm_g[l], 2).reshape(1, LANES),
            bd=bd, lb=lb_all[l].reshape(1, d_h), og=hgrn_onorm_g[l].reshape(1, HEAD_DIM),
            sg=attn_subln_g[l].reshape(1, HEAD_DIM),
            lam_params=[p[l].reshape(1, D_QK) for p in (lambda_q1, lambda_k1, lambda_q2, lambda_k2)],
            lam_init=0.8 - 0.6 * math.exp(-0.3 * l),
            rw=rw, rb=rb)

        c_all = jnp.concatenate([c_prompt, c_sample], axis=0)
        rows = c_all.shape[0]
        rows_pad = -(-rows // 8) * 8
        mod = _ada(jnp.pad(c_all, ((0, rows_pad - rows), (0, 0))), w_ada[l], b_ada[l])
        mod_p = mod[:bp].reshape(bp, 6, d)
        mod_s = mod[bp:bp + bs].reshape(bs, 6, d)

        s0_p = jnp.zeros((bp, N_HEADS, HEAD_DIM, HEAD_DIM), F32)
        o_rp, o_ap, k_n, v_n, s_n = _mixer_group(xp, mod_p, None, None, s0_p, lyr)
        kp.append(k_n)
        vp.append(v_n)
        sp.append(s_n)
        o_rs, o_as, k_n, v_n, s_n = _mixer_group(xs, mod_s, cache_k_attn[l], cache_v_attn[l],
                                                 state_hgrn[l], lyr)
        ksm.append(k_n)
        vsm.append(v_n)
        ssm.append(s_n)

        x1, h2, route, counts = _out_proj(o_rp, o_ap, xp, mod_p, o_rs, o_as, xs, mod_s,
                                          lyr["g2"], lyr["w_out"], lyr["rw"], lyr["rb"])
        rows = _moe(h2, route, counts, expert_w_gate[l], expert_w_up[l], expert_w_down[l])
        xp = _final(x1, rows, route, mod_p, bp, tp, 0)
        xs = _final(x1, rows, route, mod_s, bs, ts, bp * tp)
    return (xp, xs, jnp.stack(kp), jnp.stack(vp), jnp.stack(sp),
            jnp.stack(ksm), jnp.stack(vsm), jnp.stack(ssm))
```

```python
import functools
import math

import numpy as np
import jax
import jax.numpy as jnp
from jax import lax
from jax.experimental import pallas as pl
from jax.experimental.pallas import tpu as pltpu

F32 = jnp.float32
BF16 = jnp.bfloat16

N_HEADS = 8
HEAD_DIM = 128
D_QK = 64
CHUNK = 64
N_GROUPS = 4
EXPERTS_PER_GROUP = 8
N_EXPERTS = N_GROUPS * EXPERTS_PER_GROUP
TOP_K = 2
EPS = 1e-6
NEG_BIG = -1e30
LOG2E = math.log2(math.e)

LANES = 128
ROW_TILE = 512
ATTN_TILE = 512
ATTN_ROWS = 64
SAMPLE_KEY_TILE = 512
MOE_ROWS = 256
VMEM_LIMIT = 56 * 1024 * 1024

HGRN_LEVELS = (32, 16, 8, 4, 2, 1)


def _cparams(sem, vmem=VMEM_LIMIT):
    return pltpu.CompilerParams(dimension_semantics=sem, vmem_limit_bytes=vmem)


def _dot(a, b):
    return jnp.dot(a, b, preferred_element_type=F32)


def _dot_nt(a, b):
    return lax.dot_general(a, b, (((1,), (1,)), ((), ())), preferred_element_type=F32)


def _dot_tn(a, b):
    return lax.dot_general(a, b, (((0,), (0,)), ((), ())), preferred_element_type=F32)


def _silu(x):
    return x / (1.0 + jnp.exp(-x))


def _row_tiling(batch, seq, target):
    if seq >= target:
        assert seq % target == 0
        return 1, target
    bb = max(1, min(batch, target // seq))
    while batch % bb:
        bb -= 1
    return bb, seq


def _ada_kernel(c_ref, w_ref, b_ref, o_ref):
    s = _silu(c_ref[...]).astype(BF16)
    o_ref[...] = _dot(s, w_ref[...].astype(BF16)) + b_ref[...]


def _ada(c_all, w_ada, b_ada):
    rows, d = c_all.shape
    n_out = w_ada.shape[1]
    tn = 1024
    return pl.pallas_call(
        _ada_kernel,
        out_shape=jax.ShapeDtypeStruct((rows, n_out), F32),
        grid=(n_out // tn,),
        in_specs=[pl.BlockSpec((rows, d), lambda j: (0, 0)),
                  pl.BlockSpec((d, tn), lambda j: (0, j)),
                  pl.BlockSpec((1, tn), lambda j: (0, j))],
        out_specs=pl.BlockSpec((rows, tn), lambda j: (0, j)),
        compiler_params=_cparams(("arbitrary",)),
        name="ada_mod",
    )(c_all, w_ada, b_ada.reshape(1, n_out))


def _group_norm64(acc, bd, gain):
    sq = (acc * acc).astype(BF16)
    parts = []
    for t in range(acc.shape[1] // LANES):
        ss = _dot(sq[:, t * LANES:(t + 1) * LANES], bd)
        a = acc[:, t * LANES:(t + 1) * LANES]
        parts.append(a * lax.rsqrt(ss * (1.0 / D_QK) + EPS) * gain)
    return jnp.concatenate(parts, axis=1)


def _in_kernel(x_ref, mod_ref, g1_ref, w_ref, qg_ref, kg_ref, bd_ref,
               q_ref, f_ref, i_ref, g_ref, qa_ref, kf_ref, kb_ref, vf_ref, vb_ref,
               h_scr):
    j = pl.program_id(1)
    tm, d = h_scr.shape

    @pl.when(j == 0)
    def _():
        x = x_ref[...]
        ms = jnp.mean(x * x, axis=-1, keepdims=True)
        xn = x * lax.rsqrt(ms + EPS) * g1_ref[...]
        m = mod_ref[...]
        h = xn * (1.0 + m[:, 1:2, :]) + m[:, 0:1, :]
        h_scr[...] = h.reshape(tm, d).astype(BF16)

    acc = _dot(h_scr[...], w_ref[...])

    @pl.when(j == 0)
    def _():
        q_ref[...] = _silu(acc).astype(BF16)

    @pl.when(j == 1)
    def _():
        f_ref[...] = acc

    @pl.when(j == 2)
    def _():
        i_ref[...] = acc.astype(BF16)

    @pl.when(j == 3)
    def _():
        g_ref[...] = _silu(acc).astype(BF16)

    @pl.when(j == 4)
    def _():
        qa_ref[...] = (_group_norm64(acc, bd_ref[...], qg_ref[...]) * (LOG2E * D_QK ** -0.5)).astype(BF16)

    @pl.when(j == 5)
    def _():
        kn = _group_norm64(acc, bd_ref[...], kg_ref[...])
        kf_ref[...] = kn
        kb_ref[...] = kn.astype(BF16)

    @pl.when(j == 6)
    def _():
        vf_ref[...] = acc
        vb_ref[...] = acc.astype(BF16)


def _in_proj(x, mod, g1, w_in_b, qg, kg, bd):
    batch, seq, d = x.shape
    n = batch * seq
    bb, t = _row_tiling(batch, seq, ROW_TILE)
    tm = bb * t
    tiles_per_mod = (seq // t) if bb == 1 else 1
    sec = 1024
    n_sec = w_in_b.shape[1] // sec
    xv = x.reshape(n // t, t, d)
    row = lambda i, j: (i, 0)
    out_dt = [BF16, F32, BF16, BF16, BF16, F32, BF16, F32, BF16]
    return pl.pallas_call(
        _in_kernel,
        out_shape=[jax.ShapeDtypeStruct((n, sec), dt) for dt in out_dt],
        grid=(n // tm, n_sec),
        in_specs=[pl.BlockSpec((bb, t, d), lambda i, j: (i, 0, 0)),
                  pl.BlockSpec((bb, 6, d), lambda i, j: (i // tiles_per_mod, 0, 0)),
                  pl.BlockSpec((1, d), lambda i, j: (0, 0)),
                  pl.BlockSpec((d, sec), lambda i, j: (0, j)),
                  pl.BlockSpec((1, LANES), lambda i, j: (0, 0)),
                  pl.BlockSpec((1, LANES), lambda i, j: (0, 0)),
                  pl.BlockSpec((LANES, LANES), lambda i, j: (0, 0))],
        out_specs=[pl.BlockSpec((tm, sec), row) for _ in out_dt],
        scratch_shapes=[pltpu.VMEM((tm, d), BF16)],
        compiler_params=_cparams(("arbitrary", "arbitrary")),
        name="in_proj",
    )(xv, mod, g1, w_in_b, qg, kg, bd)


def _hgrn_consts():
    c = CHUNK
    t = np.arange(c)[:, None]
    s = np.arange(c)[None, :]
    sums = [(s <= t), (s > t)]
    masks = []
    for m in HGRN_LEVELS:
        start = (t // (2 * m)) * (2 * m)
        ref = start + m - 1
        lower = (t % (2 * m)) >= m
        sums.append(np.where(lower, (s > ref) & (s <= t), (s > t) & (s <= ref)))
        s_start = (s // (2 * m)) * (2 * m)
        masks.append((s_start == start) & lower & ((s % (2 * m)) < m))
    masks.append(s == t)
    sum_mat = np.concatenate(sums, axis=0).astype(np.float32)
    sum_cat = np.concatenate([sum_mat, sum_mat], axis=1)
    mask_mat = np.concatenate(masks, axis=0).astype(np.float32)
    return jnp.asarray(sum_cat, BF16), jnp.asarray(mask_mat, F32)


def _hgrn_kernel(q_ref, f_ref, i_ref, g_ref, s0_ref, lb_ref, og_ref, sum_ref, mask_ref,
                 o_ref, sout_ref, st_scr, *, n_chunks):
    ci = pl.program_id(1)
    c = CHUNK
    hd = HEAD_DIM

    @pl.when(ci == 0)
    def _():
        for h in range(N_HEADS):
            st_scr[h] = s0_ref[0, h].T

    f = f_ref[...]
    lb = lb_ref[...]
    e = jnp.exp(-jnp.abs(f))
    r = 1.0 / (1.0 + e)
    pos = f >= 0.0
    sig = jnp.where(pos, r, e * r)
    nsig = jnp.where(pos, e * r, r)
    logf = jnp.log(lb + (1.0 - lb) * sig)
    kin = (1.0 - lb) * nsig

    hi = logf.astype(BF16)
    lo = (logf - hi.astype(F32)).astype(BF16)
    expo = _dot(sum_ref[...], jnp.concatenate([hi, lo], axis=0))
    dec = jnp.exp(expo)

    masks = mask_ref[...]
    og = og_ref[...]
    n_lv = len(HGRN_LEVELS)
    for h in range(N_HEADS):
        cs = slice(h * hd, (h + 1) * hd)
        q = q_ref[:, cs].astype(F32)
        k = kin[:, cs]
        v = i_ref[:, cs]
        d0 = dec[0:c, cs]
        d1 = dec[c:2 * c, cs]
        st = st_scr[h]
        o = _dot_nt((q * d0).astype(BF16), st.astype(BF16))
        a = _dot_nt(q.astype(BF16), k.astype(BF16)) * masks[n_lv * c:(n_lv + 1) * c]
        for lv in range(n_lv):
            dl = dec[(2 + lv) * c:(3 + lv) * c, cs]
            a = a + _dot_nt((q * dl).astype(BF16), (k * dl).astype(BF16)) * masks[lv * c:(lv + 1) * c]
        o = o + _dot(a.astype(BF16), v)
        st_scr[h] = st * d0[c - 1:c, :] + _dot_tn(v, (k * d1).astype(BF16))
        ms = jnp.mean(o * o, axis=-1, keepdims=True)
        o = o * lax.rsqrt(ms + EPS) * og * g_ref[:, cs].astype(F32)
        o_ref[:, cs] = o.astype(BF16)

    @pl.when(ci == n_chunks - 1)
    def _():
        for h in range(N_HEADS):
            sout_ref[0, h] = st_scr[h].T


def _hgrn(q, f, i, g, s0, lb, og, batch, seq):
    n, w = q.shape
    nc = seq // CHUNK
    sum_cat, mask_mat = _hgrn_consts()
    row = lambda b, c: (b * nc + c, 0)
    const = lambda b, c: (0, 0)
    return pl.pallas_call(
        functools.partial(_hgrn_kernel, n_chunks=nc),
        out_shape=[jax.ShapeDtypeStruct((n, w), BF16),
                   jax.ShapeDtypeStruct(s0.shape, F32)],
        grid=(batch, nc),
        in_specs=[pl.BlockSpec((CHUNK, w), row),
                  pl.BlockSpec((CHUNK, w), row),
                  pl.BlockSpec((CHUNK, w), row),
                  pl.BlockSpec((CHUNK, w), row),
                  pl.BlockSpec((1,) + s0.shape[1:], lambda b, c: (b, 0, 0, 0)),
                  pl.BlockSpec((1, w), const),
                  pl.BlockSpec((1, HEAD_DIM), const),
                  pl.BlockSpec(sum_cat.shape, const),
                  pl.BlockSpec(mask_mat.shape, const)],
        out_specs=[pl.BlockSpec((CHUNK, w), row),
                   pl.BlockSpec((1,) + s0.shape[1:], lambda b, c: (b, 0, 0, 0))],
        scratch_shapes=[pltpu.VMEM((N_HEADS, HEAD_DIM, HEAD_DIM), F32)],
        compiler_params=_cparams(("arbitrary", "arbitrary")),
        name="hgrn2",
    )(q, f, i, g, s0, lb, og, sum_cat, mask_mat)


def _stack_maps(qt):
    lane = lax.broadcasted_iota(jnp.int32, qt.shape, 1)
    zero = jnp.zeros_like(qt)
    return jnp.concatenate([jnp.where(lane < D_QK, qt, zero), jnp.where(lane >= D_QK, qt, zero)], axis=0)


def _lane_tile(x, width):
    if width < LANES:
        return x[:, :width]
    return jnp.concatenate([x] * (width // LANES), axis=1)


def _softmax_update(s, v, m_ref, l_ref, acc_ref, rows):
    m_prev = m_ref[rows, :]
    m_next = jnp.maximum(m_prev, jnp.max(s, axis=1, keepdims=True))
    alpha = jnp.exp2(m_prev - m_next)
    p = jnp.exp2(s - _lane_tile(m_next, s.shape[1]))
    l_ref[rows, :] = alpha * l_ref[rows, :] + jnp.sum(p, axis=1, keepdims=True)
    acc_ref[rows, :] = alpha * acc_ref[rows, :] + _dot(p.astype(BF16), v)
    m_ref[rows, :] = m_next


def _diag_bias(slope, r0, rows, tq):
    r = r0 + lax.broadcasted_iota(jnp.int32, (rows, tq), 0)
    c = lax.broadcasted_iota(jnp.int32, (rows, tq), 1)
    allowed = (r // CHUNK) >= (c // CHUNK)
    return allowed, slope * (r - jnp.abs(r - c)).astype(F32)


def _lambda(lq1_ref, lk1_ref, lq2_ref, lk2_ref, lam_init):
    return (jnp.exp(jnp.sum(lq1_ref[...] * lk1_ref[...], axis=1, keepdims=True))
            - jnp.exp(jnp.sum(lq2_ref[...] * lk2_ref[...], axis=1, keepdims=True)) + lam_init)


def _attn_finish(acc_ref, l_ref, row0, tq, lam, sg, lam_init):
    a1 = acc_ref[row0:row0 + tq, :] / l_ref[row0:row0 + tq, :]
    a2 = acc_ref[row0 + tq:row0 + 2 * tq, :] / l_ref[row0 + tq:row0 + 2 * tq, :]
    o = a1 - lam * a2
    ms = jnp.mean(o * o, axis=-1, keepdims=True)
    return (o * lax.rsqrt(ms + EPS) * sg * (1.0 - lam_init)).astype(BF16)


def _attn_prompt_kernel(q_ref, k_ref, v_ref, slope_ref, lq1_ref, lk1_ref, lq2_ref, lk2_ref, sg_ref,
                        o_ref, qs, s_a, s_b, p_a, p_b, vx_a, vx_b, al_a, al_b, m_scr, l_scr, acc_scr,
                        *, tq, lam_init):
    qi = pl.program_id(2)
    slope = slope_ref[0, :, 0:1]
    rb = ATTN_ROWS
    qs[...] = _stack_maps(q_ref[...])
    bufs = ((s_a, p_a, vx_a, al_a), (s_b, p_b, vx_b, al_b))
    for _, _, vx, _ in bufs:
        vx[:, HEAD_DIM:] = jnp.ones((tq, LANES), BF16)
    m_scr[...] = jnp.full(m_scr.shape, NEG_BIG, F32)
    l_scr[...] = jnp.zeros(l_scr.shape, F32)
    acc_scr[...] = jnp.zeros(acc_scr.shape, F32)

    def scores(kj):
        return _dot_nt(qs[...], k_ref[pl.ds(kj * tq, tq), :])

    def update(buf, kj, diag):
        s_ref, p_scr, vx, al_scr = bufs[buf]
        if not diag:
            col = lax.broadcasted_iota(jnp.int32, (1, tq), 1)
            bias = slope * (col + (kj - qi) * tq).astype(F32)
        def biased(r0):
            rows = slice(r0, r0 + rb)
            if diag:
                allowed, bias_d = _diag_bias(slope, r0 % tq, rb, tq)
                return jnp.where(allowed, s_ref[rows, :] + bias_d, NEG_BIG)
            return s_ref[rows, :] + bias

        for r0 in range(0, 2 * tq, rb):
            rows = slice(r0, r0 + rb)
            m_prev = m_scr[rows, :]
            m_next = jnp.maximum(m_prev, jnp.max(biased(r0), axis=1, keepdims=True))
            al_scr[rows, :] = jnp.exp2(m_prev - m_next)
            m_scr[rows, :] = m_next
        for r0 in range(0, 2 * tq, rb):
            rows = slice(r0, r0 + rb)
            p_scr[rows, :] = jnp.exp2(biased(r0) - _lane_tile(m_scr[rows, :], tq)).astype(BF16)
        vx[:, 0:HEAD_DIM] = v_ref[pl.ds(kj * tq, tq), :]
        pv = _dot(p_scr[...], vx[...])
        alpha = al_scr[...]
        acc_scr[...] = alpha * acc_scr[...] + pv[:, 0:HEAD_DIM]
        l_scr[...] = alpha * l_scr[...] + pv[:, HEAD_DIM:]

    s_a[...] = scores(0)

    def pair_step(pj, carry):
        kj = 2 * pj
        s_b[...] = scores(kj + 1)
        update(0, kj, False)
        s_a[...] = scores(kj + 2)
        update(1, kj + 1, False)
        return carry

    lax.fori_loop(0, lax.shift_right_logical(qi, 1), pair_step, 0)
    odd = (qi & 1) == 1

    @pl.when(jnp.logical_not(odd))
    def _():
        update(0, qi, True)

    @pl.when(odd)
    def _():
        s_b[...] = scores(qi)
        update(0, qi - 1, False)
        update(1, qi, True)

    lam = _lambda(lq1_ref, lk1_ref, lq2_ref, lk2_ref, lam_init)
    o_ref[...] = _attn_finish(acc_scr, l_scr, 0, tq, lam, sg_ref[...], lam_init)


def _attn_cached_kernel(q_ref, kn_ref, vn_ref, pk_ref, pv_ref, slope_ref,
                        lq1_ref, lk1_ref, lq2_ref, lk2_ref, sg_ref,
                        o_ref, qs, m_scr, l_scr, acc_scr, *, tq, tk, p_len, lam_init):
    hd = HEAD_DIM
    for h in range(N_HEADS):
        qs[2 * tq * h:2 * tq * (h + 1), :] = _stack_maps(q_ref[:, h * hd:(h + 1) * hd])
    m_scr[...] = jnp.full(m_scr.shape, NEG_BIG, F32)
    l_scr[...] = jnp.zeros(l_scr.shape, F32)
    acc_scr[...] = jnp.zeros(acc_scr.shape, F32)
    col = lax.broadcasted_iota(jnp.int32, (1, tk), 1)

    def past_step(kj, carry):
        pos = (col + (kj * tk - p_len)).astype(F32)
        for h in range(N_HEADS):
            rows = slice(2 * tq * h, 2 * tq * (h + 1))
            k = pk_ref[0, pl.ds(kj * (tk * N_HEADS) + h, tk, stride=N_HEADS), :].astype(BF16)
            v = pv_ref[0, pl.ds(kj * (tk * N_HEADS) + h, tk, stride=N_HEADS), :].astype(BF16)
            s = _dot_nt(qs[rows, :], k) + slope_ref[h, :, 0:1] * pos
            _softmax_update(s, v, m_scr, l_scr, acc_scr, rows)
        return carry

    lax.fori_loop(0, p_len // tk, past_step, 0)

    lam = _lambda(lq1_ref, lk1_ref, lq2_ref, lk2_ref, lam_init)
    for h in range(N_HEADS):
        kd = kn_ref[:, h * hd:(h + 1) * hd]
        vd = vn_ref[:, h * hd:(h + 1) * hd]
        allowed, bias = _diag_bias(slope_ref[h, :, 0:1], 0, tq, tq)
        for half in range(2):
            rows = slice(2 * tq * h + half * tq, 2 * tq * h + (half + 1) * tq)
            s = jnp.where(allowed, _dot_nt(qs[rows, :], kd) + bias, NEG_BIG)
            _softmax_update(s, vd, m_scr, l_scr, acc_scr, rows)
        o_ref[:, h * hd:(h + 1) * hd] = _attn_finish(acc_scr, l_scr, 2 * tq * h, tq, lam, sg_ref[...],
                                                     lam_init)


def _alibi_slopes():
    s = (2.0 ** (-8.0 * np.arange(1, N_HEADS + 1) / N_HEADS)) * LOG2E
    return jnp.asarray(np.broadcast_to(s[:, None, None], (N_HEADS, 1, LANES)), F32)


def _attention_prompt(qa, kb, vb, batch, seq, lam_params, subln_g, lam_init):
    n, w = qa.shape
    tq = min(ATTN_TILE, seq)
    nq = seq // tq
    assert seq % tq == 0 and tq % CHUNK == 0
    small = lambda b, h, qi: (0, 0)
    q_spec = pl.BlockSpec((tq, HEAD_DIM), lambda b, h, qi: (b * nq + qi, h))
    kv_spec = pl.BlockSpec((seq, HEAD_DIM), lambda b, h, qi: (b, h))
    return pl.pallas_call(
        functools.partial(_attn_prompt_kernel, tq=tq, lam_init=lam_init),
        out_shape=jax.ShapeDtypeStruct((n, w), BF16),
        grid=(batch, N_HEADS, nq),
        in_specs=[q_spec, kv_spec, kv_spec,
                  pl.BlockSpec((1, 1, LANES), lambda b, h, qi: (h, 0, 0)),
                  pl.BlockSpec((1, D_QK), small), pl.BlockSpec((1, D_QK), small),
                  pl.BlockSpec((1, D_QK), small), pl.BlockSpec((1, D_QK), small),
                  pl.BlockSpec((1, HEAD_DIM), small)],
        out_specs=q_spec,
        scratch_shapes=[pltpu.VMEM((2 * tq, HEAD_DIM), BF16),
                        pltpu.VMEM((2 * tq, tq), F32), pltpu.VMEM((2 * tq, tq), F32),
                        pltpu.VMEM((2 * tq, tq), BF16), pltpu.VMEM((2 * tq, tq), BF16),
                        pltpu.VMEM((tq, HEAD_DIM + LANES), BF16), pltpu.VMEM((tq, HEAD_DIM + LANES), BF16),
                        pltpu.VMEM((2 * tq, LANES), F32), pltpu.VMEM((2 * tq, LANES), F32),
                        pltpu.VMEM((2 * tq, LANES), F32), pltpu.VMEM((2 * tq, LANES), F32),
                        pltpu.VMEM((2 * tq, HEAD_DIM), F32)],
        compiler_params=_cparams(("arbitrary", "arbitrary", "arbitrary")),
        name="diff_attn",
    )(qa, kb, vb, _alibi_slopes(), *lam_params, subln_g)


def _attention_cached(qa, kb, vb, batch, seq, past_k, past_v, lam_params, subln_g, lam_init):
    n, w = qa.shape
    p_len = past_k.shape[1]
    tk = min(SAMPLE_KEY_TILE, p_len)
    assert p_len % tk == 0 and p_len % CHUNK == 0 and seq == CHUNK
    rows = 2 * seq * N_HEADS
    small = lambda b: (0, 0)
    tok_spec = pl.BlockSpec((seq, w), lambda b: (b, 0))
    past_spec = pl.BlockSpec((1, p_len * N_HEADS, HEAD_DIM), lambda b: (b, 0, 0))
    return pl.pallas_call(
        functools.partial(_attn_cached_kernel, tq=seq, tk=tk, p_len=p_len, lam_init=lam_init),
        out_shape=jax.ShapeDtypeStruct((n, w), BF16),
        grid=(batch,),
        in_specs=[tok_spec, tok_spec, tok_spec, past_spec, past_spec,
                  pl.BlockSpec((N_HEADS, 1, LANES), lambda b: (0, 0, 0)),
                  pl.BlockSpec((1, D_QK), small), pl.BlockSpec((1, D_QK), small),
                  pl.BlockSpec((1, D_QK), small), pl.BlockSpec((1, D_QK), small),
                  pl.BlockSpec((1, HEAD_DIM), small)],
        out_specs=tok_spec,
        scratch_shapes=[pltpu.VMEM((rows, HEAD_DIM), BF16), pltpu.VMEM((rows, LANES), F32),
                        pltpu.VMEM((rows, LANES), F32), pltpu.VMEM((rows, HEAD_DIM), F32)],
        compiler_params=_cparams(("arbitrary",)),
        name="diff_attn_cached",
    )(qa, kb, vb, past_k.reshape(batch, p_len * N_HEADS, HEAD_DIM),
      past_v.reshape(batch, p_len * N_HEADS, HEAD_DIM), _alibi_slopes(), *lam_params, subln_g)


def _split_bf16(x):
    hi = x.astype(BF16)
    lo = (x - hi.astype(F32)).astype(BF16)
    return hi, lo


def _out_kernel(orp_ref, oap_ref, xp_ref, modp_ref, ors_ref, oas_ref, xs_ref, mods_ref,
                g2_ref, w_ref, rw_ref, rb_ref, tri_ref,
                x1_ref, h2_ref, route_ref, cnt_ref, cnt_scr, *, n_tiles_p):
    i = pl.program_id(0)

    @pl.when(i == 0)
    def _():
        cnt_scr[...] = jnp.zeros_like(cnt_scr)

    args = (g2_ref, w_ref, rw_ref, rb_ref, tri_ref, x1_ref, h2_ref, route_ref, cnt_scr)

    @pl.when(i < n_tiles_p)
    def _():
        _out_tile(orp_ref, oap_ref, xp_ref, modp_ref, *args)

    @pl.when(i >= n_tiles_p)
    def _():
        _out_tile(ors_ref, oas_ref, xs_ref, mods_ref, *args)

    cnt_ref[...] = jnp.broadcast_to(cnt_scr[...], cnt_ref.shape)


def _out_tile(or_ref, oa_ref, x_ref, mod_ref, g2_ref, w_ref, rw_ref, rb_ref, tri_ref,
              x1_ref, h2_ref, route_ref, cnt_scr):
    bb, t, d = x_ref.shape
    tm = bb * t
    half = or_ref.shape[1]
    mix = _dot(or_ref[...], w_ref[0:half, :]) + _dot(oa_ref[...], w_ref[half:, :])
    m = mod_ref[...]
    x1 = x_ref[...] + m[:, 2:3, :] * mix.reshape(bb, t, d)
    ms = jnp.mean(x1 * x1, axis=-1, keepdims=True)
    h2 = x1 * lax.rsqrt(ms + EPS) * g2_ref[...] * (1.0 + m[:, 4:5, :]) + m[:, 3:4, :]
    x1_ref[...] = x1.reshape(tm, d)
    h2 = h2.reshape(tm, d)
    h2_ref[...] = h2

    hh, hl = _split_bf16(h2)
    wh, wl = _split_bf16(rw_ref[...])
    logits = _dot(hh, wh) + _dot(hl, wh) + _dot(hh, wl) + rb_ref[...]
    lane = lax.broadcasted_iota(jnp.int32, logits.shape, 1)
    neg = jnp.float32(-jnp.inf)
    big = jnp.int32(1 << 20)

    is_g = (lane >= N_EXPERTS) & (lane < N_EXPERTS + N_GROUPS)
    gl = jnp.where(is_g, logits, neg)
    gmax = jnp.max(gl, axis=1, keepdims=True)
    gidx = jnp.min(jnp.where(gl == gmax, lane - N_EXPERTS, big), axis=1, keepdims=True)
    g_prob = 1.0 / jnp.sum(jnp.exp(gl - gmax), axis=1, keepdims=True)

    in_grp = (lane < N_EXPERTS) & ((lane // EXPERTS_PER_GROUP) == gidx)
    el = jnp.where(in_grp, logits, neg)
    m1 = jnp.max(el, axis=1, keepdims=True)
    i1 = jnp.min(jnp.where(el == m1, lane, big), axis=1, keepdims=True)
    el2 = jnp.where(lane == i1, neg, el)
    m2 = jnp.max(el2, axis=1, keepdims=True)
    i2 = jnp.min(jnp.where(el2 == m2, lane, big), axis=1, keepdims=True)
    e21 = jnp.exp(m2 - m1)
    w1 = g_prob / (1.0 + e21)
    w2 = g_prob * e21 / (1.0 + e21)
    tri = tri_ref[...]
    oh1 = jnp.where(lane == i1, 1.0, 0.0)
    oh2 = jnp.where(lane == i2, 1.0, 0.0)
    cnt = cnt_scr[...]
    c1 = jnp.sum(oh1, axis=0, keepdims=True)
    c2 = jnp.sum(oh2, axis=0, keepdims=True)
    r1 = jnp.sum(oh1 * (_dot(tri, oh1.astype(BF16)) + cnt), axis=1, keepdims=True)
    r2 = jnp.sum(oh2 * (_dot(tri, oh2.astype(BF16)) + cnt + c1), axis=1, keepdims=True)
    cnt_scr[...] = cnt + c1 + c2

    vals = (i1.astype(F32), i2.astype(F32), w1, w2, r1, r2)
    route = jnp.zeros(logits.shape, F32)
    for k, v in enumerate(vals):
        route = jnp.where(lane == k, v, route)
    route_ref[...] = route


def _out_proj(o_rp, o_ap, xp, mod_p, o_rs, o_as, xs, mod_s, g2, w_out_b, rw, rb):
    d = xp.shape[2]
    half = o_rp.shape[1]
    n_p = xp.shape[0] * xp.shape[1]
    n_s = xs.shape[0] * xs.shape[1]
    n = n_p + n_s
    bbp, tp = _row_tiling(xp.shape[0], xp.shape[1], ROW_TILE)
    bbs, ts = _row_tiling(xs.shape[0], xs.shape[1], ROW_TILE)
    tm = bbp * tp
    assert bbs * ts == tm and n_p % tm == 0 and n_s % tm == 0
    ntp = n_p // tm
    nts = n_s // tm
    modp_tiles = (xp.shape[1] // tp) if bbp == 1 else 1
    mods_tiles = (xs.shape[1] // ts) if bbs == 1 else 1
    pi = lambda i: jnp.minimum(i, ntp - 1)
    si = lambda i: jnp.maximum(i - ntp, 0)
    tri = jnp.asarray(np.tril(np.ones((tm, tm), np.float32), -1), BF16)
    row = lambda i: (i, 0)
    const = lambda i: (0, 0)
    return pl.pallas_call(
        functools.partial(_out_kernel, n_tiles_p=ntp),
        out_shape=[jax.ShapeDtypeStruct((n, d), F32), jax.ShapeDtypeStruct((n, d), F32),
                   jax.ShapeDtypeStruct((n, LANES), F32), jax.ShapeDtypeStruct((8, LANES), F32)],
        grid=(ntp + nts,),
        in_specs=[pl.BlockSpec((tm, half), lambda i: (pi(i), 0)),
                  pl.BlockSpec((tm, half), lambda i: (pi(i), 0)),
                  pl.BlockSpec((bbp, tp, d), lambda i: (pi(i), 0, 0)),
                  pl.BlockSpec((bbp, 6, d), lambda i: (pi(i) // modp_tiles, 0, 0)),
                  pl.BlockSpec((tm, half), lambda i: (si(i), 0)),
                  pl.BlockSpec((tm, half), lambda i: (si(i), 0)),
                  pl.BlockSpec((bbs, ts, d), lambda i: (si(i), 0, 0)),
                  pl.BlockSpec((bbs, 6, d), lambda i: (si(i) // mods_tiles, 0, 0)),
                  pl.BlockSpec((1, d), const),
                  pl.BlockSpec(w_out_b.shape, const),
                  pl.BlockSpec(rw.shape, const),
                  pl.BlockSpec((1, LANES), const),
                  pl.BlockSpec((tm, tm), const)],
        out_specs=[pl.BlockSpec((tm, d), row), pl.BlockSpec((tm, d), row),
                   pl.BlockSpec((tm, LANES), row), pl.BlockSpec((8, LANES), const)],
        scratch_shapes=[pltpu.VMEM((1, LANES), F32)],
        compiler_params=_cparams(("arbitrary",)),
        name="out_proj_router",
    )(o_rp, o_ap, xp.reshape(n_p // tp, tp, d), mod_p,
      o_rs, o_as, xs.reshape(n_s // ts, ts, d), mod_s, g2, w_out_b, rw, rb, tri)


ROW_PACK_SHIFT = 16
MOE_CHUNKS = 4


def _moe_kernel(pack_ref, be_ref, nu_ref,
                h2_hbm, wg_ref, wu_ref, wd_ref, out_hbm,
                xbuf, obuf, xb, wgb, wub, wdb, gsem, ssem, *, n_blk):
    i = pl.program_id(0)
    n_used = nu_ref[0]
    bm = xbuf.shape[1]
    de = wgb.shape[1]
    cw = de // MOE_CHUNKS
    rpb = bm // MOE_CHUNKS
    slot = lax.rem(i, 2)
    other = 1 - slot
    dst_mask = (1 << ROW_PACK_SHIFT) - 1

    def gather_copy(tok, r, s):
        return pltpu.make_async_copy(h2_hbm.at[pl.ds(tok, 1)], xbuf.at[s, pl.ds(r, 1)], gsem.at[s])

    def scatter_copy(dst, r, s):
        return pltpu.make_async_copy(obuf.at[s, pl.ds(r, 1)], out_hbm.at[pl.ds(dst, 1)], ssem.at[s])

    def start_gather(blk, s, rows):
        for r in rows:
            gather_copy(lax.shift_right_logical(pack_ref[blk * bm + r], ROW_PACK_SHIFT), r, s).start()

    def start_scatter(blk, s, rows):
        for r in rows:
            scatter_copy(pack_ref[blk * bm + r] & dst_mask, r, s).start()

    def wait_gather(s):
        for r in range(bm):
            gather_copy(0, r, s).wait()

    def wait_scatter(s):
        for r in range(bm):
            scatter_copy(0, r, s).wait()

    @pl.when(i < n_used)
    def _():
        @pl.when(i == 0)
        def _():
            start_gather(0, 0, range(bm))
            obuf[...] = jnp.zeros_like(obuf)
            spare0 = out_hbm.shape[0] - 2 * bm
            fills = [pltpu.make_async_copy(obuf.at[s], out_hbm.at[pl.ds(spare0 + s * bm, bm)],
                                           ssem.at[s]) for s in range(2)]
            for c in fills:
                c.start()
            for c in fills:
                c.wait()

        new_expert = jnp.logical_or(i == 0, be_ref[i] != be_ref[jnp.maximum(i - 1, 0)])

        @pl.when(new_expert)
        def _():
            wgb[...] = wg_ref[0].astype(BF16)
            wub[...] = wu_ref[0].astype(BF16)
            wdb[...] = wd_ref[0].astype(BF16)

        wait_gather(slot)
        xb[...] = xbuf[slot].astype(BF16)
        nxt = jnp.minimum(i + 1, n_blk - 1)
        prv = jnp.where(i == 0, n_blk, i - 1)
        hid = []
        for c in range(MOE_CHUNKS):
            cols = slice(c * cw, (c + 1) * cw)
            hid.append((_silu(_dot(xb[...], wgb[:, cols])) * _dot(xb[...], wub[:, cols])).astype(BF16))
            rows = range(c * rpb, (c + 1) * rpb)
            start_gather(nxt, other, rows)
            start_scatter(prv, other, rows)
        y = _dot(jnp.concatenate(hid, axis=1), wdb[...])

        @pl.when(i >= 1)
        def _():
            wait_scatter(slot)

        obuf[slot] = y

        @pl.when(i == n_used - 1)
        def _():
            wait_gather(other)
            start_scatter(i, slot, range(bm))
            wait_scatter(other)
            wait_scatter(slot)


def _moe(h2, route, counts, wg, wu, wd):
    n, d = h2.shape
    assert TOP_K == 2
    s_rows = n * TOP_K
    bm = MOE_ROWS
    n_blk = -(-s_rows // bm) + N_EXPERTS
    n_pad = n_blk * bm
    n_out = s_rows + 2 * bm
    assert n < (1 << (31 - ROW_PACK_SHIFT)) and n_out <= (1 << ROW_PACK_SHIFT)

    counts = counts[0, :N_EXPERTS].astype(jnp.int32)
    padded = (counts + bm - 1) // bm * bm
    pad_end = jnp.cumsum(padded)
    pad_start = pad_end - padded
    expert = route[:, 0:TOP_K].astype(jnp.int32)
    rank = route[:, 2 * TOP_K:3 * TOP_K].astype(jnp.int32)
    onehot = expert[:, :, None] == jnp.arange(N_EXPERTS, dtype=jnp.int32)[None, None, :]
    dest = jnp.sum(jnp.where(onehot, pad_start[None, None, :], 0), axis=2) + rank
    tok = jnp.arange(n, dtype=jnp.int32)
    valid = (tok[:, None] << ROW_PACK_SHIFT) | (jnp.arange(TOP_K, dtype=jnp.int32)[None, :] * n
                                                + tok[:, None])
    pos = jnp.arange(n_pad + bm, dtype=jnp.int32)
    blk_par = jnp.where(pos < n_pad, (pos // bm) % 2, 1)
    spare = s_rows + blk_par * bm + pos % bm
    row_pack = spare.at[dest.reshape(-1)].set(valid.reshape(-1))
    blk_start = jnp.arange(n_blk, dtype=jnp.int32) * bm
    blk_e = jnp.minimum(jnp.sum((pad_end[None, :] <= blk_start[:, None]).astype(jnp.int32), axis=1),
                        N_EXPERTS - 1)
    n_used = (pad_end[-1] // bm).astype(jnp.int32).reshape(1)

    de = wg.shape[2]
    assert de % MOE_CHUNKS == 0 and bm % MOE_CHUNKS == 0
    wmap = lambda i, pack, be, nu: (be[i], 0, 0)
    grid_spec = pltpu.PrefetchScalarGridSpec(
        num_scalar_prefetch=3,
        grid=(n_blk,),
        in_specs=[pl.BlockSpec(memory_space=pl.ANY),
                  pl.BlockSpec((1, d, de), wmap),
                  pl.BlockSpec((1, d, de), wmap),
                  pl.BlockSpec((1, de, d), wmap)],
        out_specs=pl.BlockSpec(memory_space=pl.ANY),
        scratch_shapes=[pltpu.VMEM((2, bm, d), F32), pltpu.VMEM((2, bm, d), F32),
                        pltpu.VMEM((bm, d), BF16),
                        pltpu.VMEM((d, de), BF16), pltpu.VMEM((d, de), BF16),
                        pltpu.VMEM((de, d), BF16),
                        pltpu.SemaphoreType.DMA((2,)), pltpu.SemaphoreType.DMA((2,))])
    return pl.pallas_call(
        functools.partial(_moe_kernel, n_blk=n_blk),
        out_shape=jax.ShapeDtypeStruct((n_out, d), F32),
        grid_spec=grid_spec,
        compiler_params=_cparams(("arbitrary",)),
        name="moe_experts",
    )(row_pack, blk_e, n_used, h2, wg, wu, wd)


def _final_kernel(x1_ref, r0_ref, r1_ref, route_ref, mod_ref, y_ref):
    w = route_ref[...]
    moe = w[:, :, TOP_K:TOP_K + 1] * r0_ref[...] + w[:, :, TOP_K + 1:TOP_K + 2] * r1_ref[...]
    y_ref[...] = x1_ref[...] + mod_ref[...][:, 5:6, :] * moe


def _final(x1, rows, route, mod, batch, seq, row_off):
    n_all, d = x1.shape
    n = batch * seq
    bb, t = _row_tiling(batch, seq, ROW_TILE)
    tm = bb * t
    tiles_per_mod = (seq // t) if bb == 1 else 1
    assert row_off % tm == 0 and n_all % tm == 0 and rows.shape[0] % t == 0
    off = row_off // tm
    slot1 = n_all // tm
    rows_v = rows.reshape(-1, t, d)
    return pl.pallas_call(
        _final_kernel,
        out_shape=jax.ShapeDtypeStruct((n // t, t, d), F32),
        grid=(n // tm,),
        in_specs=[pl.BlockSpec((bb, t, d), lambda i: (i + off, 0, 0)),
                  pl.BlockSpec((bb, t, d), lambda i: (i + off, 0, 0)),
                  pl.BlockSpec((bb, t, d), lambda i: (i + off + slot1, 0, 0)),
                  pl.BlockSpec((bb, t, LANES), lambda i: (i + off, 0, 0)),
                  pl.BlockSpec((bb, 6, d), lambda i: (i // tiles_per_mod, 0, 0))],
        out_specs=pl.BlockSpec((bb, t, d), lambda i: (i, 0, 0)),
        compiler_params=_cparams(("arbitrary",)),
        name="final_residual",
    )(x1.reshape(-1, t, d), rows_v, rows_v, route.reshape(-1, t, LANES), mod).reshape(batch, seq, d)


def _mixer_group(x, mod, past_k, past_v, s0, lyr):
    batch, seq, d = x.shape
    q, f, i, g, qa, kf, kb, vf, vb = _in_proj(x, mod, lyr["g1"], lyr["w_in"], lyr["qg"], lyr["kg"],
                                               lyr["bd"])
    o_r, s_new = _hgrn(q, f, i, g, s0, lyr["lb"], lyr["og"], batch, seq)
    if past_k is None:
        o_a = _attention_prompt(qa, kb, vb, batch, seq, lyr["lam_params"], lyr["sg"], lyr["lam_init"])
    else:
        o_a = _attention_cached(qa, kb, vb, batch, seq, past_k, past_v, lyr["lam_params"], lyr["sg"],
                                lyr["lam_init"])
    k_new = kf.reshape(batch, seq, N_HEADS, HEAD_DIM)
    v_new = vf.reshape(batch, seq, N_HEADS, HEAD_DIM)
    return o_r, o_a, k_new, v_new, s_new


def kernel(x_prompt, x_sample, cache_k_attn, cache_v_attn, state_hgrn, c_prompt, c_sample, w_ada, b_ada, norm1_g, norm2_g, w_in, w_out, hgrn_lb_param, hgrn_onorm_g, attn_qnorm_g, attn_knorm_g, lambda_q1, lambda_k1, lambda_q2, lambda_k2, attn_subln_g, router_group_w, router_group_b, router_expert_w, router_expert_b, expert_w_gate, expert_w_up, expert_w_down):
    depth = w_ada.shape[0]
    bp, tp, d = x_prompt.shape
    bs, ts, _ = x_sample.shape
    d_h = N_HEADS * HEAD_DIM
    lb_all = jnp.cumsum(jax.nn.softmax(hgrn_lb_param.astype(F32), axis=0), axis=0)
    bd = jnp.asarray(np.kron(np.eye(LANES // D_QK), np.ones((D_QK, D_QK))), BF16)

    xp, xs = x_prompt, x_sample
    kp, vp, sp, ksm, vsm, ssm = [], [], [], [], [], []
    for l in range(depth):
        wl = w_in[l]
        qk = wl[:, 4 * d_h:4 * d_h + 4 * N_HEADS * D_QK].reshape(d, 2, 2, N_HEADS, D_QK)
        qk = qk.transpose(0, 1, 3, 2, 4).reshape(d, 4 * N_HEADS * D_QK)
        w_in_b = jnp.concatenate([wl[:, :4 * d_h], qk, wl[:, 4 * d_h + 4 * N_HEADS * D_QK:]],
                                 axis=1).astype(BF16)
        rw = jnp.zeros((d, LANES), F32)
        rw = rw.at[:, :N_EXPERTS].set(router_expert_w[l])
        rw = rw.at[:, N_EXPERTS:N_EXPERTS + N_GROUPS].set(router_group_w[l])
        rb = jnp.zeros((1, LANES), F32)
        rb = rb.at[0, :N_EXPERTS].set(router_expert_b[l])
        rb = rb.at[0, N_EXPERTS:N_EXPERTS + N_GROUPS].set(router_group_b[l])
        lyr = dict(
            g1=norm1_g[l].reshape(1, d), g2=norm2_g[l].reshape(1, d),
            w_in=w_in_b, w_out=w_out[l].astype(BF16),
            qg=jnp.tile(attn_qnorm_g[l], 2).reshape(1, LANES),
            kg=jnp.tile(attn_knorm_g[l], 2).reshape(1, LANES),
            bd=bd, lb=lb_all[l].reshape(1, d_h), og=hgrn_onorm_g[l].reshape(1, HEAD_DIM),
            sg=attn_subln_g[l].reshape(1, HEAD_DIM),
            lam_params=[p[l].reshape(1, D_QK) for p in (lambda_q1, lambda_k1, lambda_q2, lambda_k2)],
            lam_init=0.8 - 0.6 * math.exp(-0.3 * l),
            rw=rw, rb=rb)

        c_all = jnp.concatenate([c_prompt, c_sample], axis=0)
        rows = c_all.shape[0]
        rows_pad = -(-rows // 8) * 8
        mod = _ada(jnp.pad(c_all, ((0, rows_pad - rows), (0, 0))), w_ada[l], b_ada[l])
        mod_p = mod[:bp].reshape(bp, 6, d)
        mod_s = mod[bp:bp + bs].reshape(bs, 6, d)

        s0_p = jnp.zeros((bp, N_HEADS, HEAD_DIM, HEAD_DIM), F32)
        o_rp, o_ap, k_n, v_n, s_n = _mixer_group(xp, mod_p, None, None, s0_p, lyr)
        kp.append(k_n)
        vp.append(v_n)
        sp.append(s_n)
        o_rs, o_as, k_n, v_n, s_n = _mixer_group(xs, mod_s, cache_k_attn[l], cache_v_attn[l],
                                                 state_hgrn[l], lyr)
        ksm.append(k_n)
        vsm.append(v_n)
        ssm.append(s_n)

        x1, h2, route, counts = _out_proj(o_rp, o_ap, xp, mod_p, o_rs, o_as, xs, mod_s,
                                          lyr["g2"], lyr["w_out"], lyr["rw"], lyr["rb"])
        rows = _moe(h2, route, counts, expert_w_gate[l], expert_w_up[l], expert_w_down[l])
        xp = _final(x1, rows, route, mod_p, bp, tp, 0)
        xs = _final(x1, rows, route, mod_s, bs, ts, bp * tp)
    return (xp, xs, jnp.stack(kp), jnp.stack(vp), jnp.stack(sp),
            jnp.stack(ksm), jnp.stack(vsm), jnp.stack(ssm))
```

```python
import functools
import math

import numpy as np
import jax
import jax.numpy as jnp
from jax import lax
from jax.experimental import pallas as pl
from jax.experimental.pallas import tpu as pltpu

F32 = jnp.float32
BF16 = jnp.bfloat16

N_HEADS = 8
HEAD_DIM = 128
D_QK = 64
CHUNK = 64
N_GROUPS = 4
EXPERTS_PER_GROUP = 8
N_EXPERTS = N_GROUPS * EXPERTS_PER_GROUP
TOP_K = 2
EPS = 1e-6
NEG_BIG = -1e30
LOG2E = math.log2(math.e)

LANES = 128
ROW_TILE = 512
ATTN_TILE = 512
ATTN_ROWS = 64
SAMPLE_KEY_TILE = 512
MOE_ROWS = 256
VMEM_LIMIT = 56 * 1024 * 1024

HGRN_LEVELS = (32, 16, 8, 4, 2, 1)


def _cparams(sem, vmem=VMEM_LIMIT):
    return pltpu.CompilerParams(dimension_semantics=sem, vmem_limit_bytes=vmem)


def _dot(a, b):
    return jnp.dot(a, b, preferred_element_type=F32)


def _dot_nt(a, b):
    return lax.dot_general(a, b, (((1,), (1,)), ((), ())), preferred_element_type=F32)


def _dot_tn(a, b):
    return lax.dot_general(a, b, (((0,), (0,)), ((), ())), preferred_element_type=F32)


def _silu(x):
    return x / (1.0 + jnp.exp(-x))


def _row_tiling(batch, seq, target):
    if seq >= target:
        assert seq % target == 0
        return 1, target
    bb = max(1, min(batch, target // seq))
    while batch % bb:
        bb -= 1
    return bb, seq


def _ada_kernel(c_ref, w_ref, b_ref, o_ref):
    s = _silu(c_ref[...]).astype(BF16)
    o_ref[...] = _dot(s, w_ref[...].astype(BF16)) + b_ref[...]


def _ada(c_all, w_ada, b_ada):
    rows, d = c_all.shape
    n_out = w_ada.shape[1]
    tn = 1024
    return pl.pallas_call(
        _ada_kernel,
        out_shape=jax.ShapeDtypeStruct((rows, n_out), F32),
        grid=(n_out // tn,),
        in_specs=[pl.BlockSpec((rows, d), lambda j: (0, 0)),
                  pl.BlockSpec((d, tn), lambda j: (0, j)),
                  pl.BlockSpec((1, tn), lambda j: (0, j))],
        out_specs=pl.BlockSpec((rows, tn), lambda j: (0, j)),
        compiler_params=_cparams(("arbitrary",)),
        name="ada_mod",
    )(c_all, w_ada, b_ada.reshape(1, n_out))


def _group_norm64(acc, bd, gain):
    sq = (acc * acc).astype(BF16)
    parts = []
    for t in range(acc.shape[1] // LANES):
        ss = _dot(sq[:, t * LANES:(t + 1) * LANES], bd)
        a = acc[:, t * LANES:(t + 1) * LANES]
        parts.append(a * lax.rsqrt(ss * (1.0 / D_QK) + EPS) * gain)
    return jnp.concatenate(parts, axis=1)


def _in_kernel(x_ref, mod_ref, g1_ref, w_ref, qg_ref, kg_ref, bd_ref,
               q_ref, f_ref, i_ref, g_ref, qa_ref, kf_ref, kb_ref, vf_ref, vb_ref,
               h_scr):
    j = pl.program_id(1)
    tm, d = h_scr.shape

    @pl.when(j == 0)
    def _():
        x = x_ref[...]
        ms = jnp.mean(x * x, axis=-1, keepdims=True)
        xn = x * lax.rsqrt(ms + EPS) * g1_ref[...]
        m = mod_ref[...]
        h = xn * (1.0 + m[:, 1:2, :]) + m[:, 0:1, :]
        h_scr[...] = h.reshape(tm, d).astype(BF16)

    acc = _dot(h_scr[...], w_ref[...])

    @pl.when(j == 0)
    def _():
        q_ref[...] = _silu(acc).astype(BF16)

    @pl.when(j == 1)
    def _():
        f_ref[...] = acc

    @pl.when(j == 2)
    def _():
        i_ref[...] = acc.astype(BF16)

    @pl.when(j == 3)
    def _():
        g_ref[...] = _silu(acc).astype(BF16)

    @pl.when(j == 4)
    def _():
        qa_ref[...] = (_group_norm64(acc, bd_ref[...], qg_ref[...]) * (LOG2E * D_QK ** -0.5)).astype(BF16)

    @pl.when(j == 5)
    def _():
        kn = _group_norm64(acc, bd_ref[...], kg_ref[...])
        kf_ref[...] = kn
        kb_ref[...] = kn.astype(BF16)

    @pl.when(j == 6)
    def _():
        vf_ref[...] = acc
        vb_ref[...] = acc.astype(BF16)


def _in_proj(x, mod, g1, w_in_b, qg, kg, bd):
    batch, seq, d = x.shape
    n = batch * seq
    bb, t = _row_tiling(batch, seq, ROW_TILE)
    tm = bb * t
    tiles_per_mod = (seq // t) if bb == 1 else 1
    sec = 1024
    n_sec = w_in_b.shape[1] // sec
    xv = x.reshape(n // t, t, d)
    row = lambda i, j: (i, 0)
    out_dt = [BF16, F32, BF16, BF16, BF16, F32, BF16, F32, BF16]
    return pl.pallas_call(
        _in_kernel,
        out_shape=[jax.ShapeDtypeStruct((n, sec), dt) for dt in out_dt],
        grid=(n // tm, n_sec),
        in_specs=[pl.BlockSpec((bb, t, d), lambda i, j: (i, 0, 0)),
                  pl.BlockSpec((bb, 6, d), lambda i, j: (i // tiles_per_mod, 0, 0)),
                  pl.BlockSpec((1, d), lambda i, j: (0, 0)),
                  pl.BlockSpec((d, sec), lambda i, j: (0, j)),
                  pl.BlockSpec((1, LANES), lambda i, j: (0, 0)),
                  pl.BlockSpec((1, LANES), lambda i, j: (0, 0)),
                  pl.BlockSpec((LANES, LANES), lambda i, j: (0, 0))],
        out_specs=[pl.BlockSpec((tm, sec), row) for _ in out_dt],
        scratch_shapes=[pltpu.VMEM((tm, d), BF16)],
        compiler_params=_cparams(("arbitrary", "arbitrary")),
        name="in_proj",
    )(xv, mod, g1, w_in_b, qg, kg, bd)


def _hgrn_consts():
    c = CHUNK
    t = np.arange(c)[:, None]
    s = np.arange(c)[None, :]
    sums = [(s <= t), (s > t)]
    masks = []
    for m in HGRN_LEVELS:
        start = (t // (2 * m)) * (2 * m)
        ref = start + m - 1
        lower = (t % (2 * m)) >= m
        sums.append(np.where(lower, (s > ref) & (s <= t), (s > t) & (s <= ref)))
        s_start = (s // (2 * m)) * (2 * m)
        masks.append((s_start == start) & lower & ((s % (2 * m)) < m))
    masks.append(s == t)
    sum_mat = np.concatenate(sums, axis=0).astype(np.float32)
    sum_cat = np.concatenate([sum_mat, sum_mat], axis=1)
    mask_mat = np.concatenate(masks, axis=0).astype(np.float32)
    return jnp.asarray(sum_cat, BF16), jnp.asarray(mask_mat, F32)


def _hgrn_kernel(q_ref, f_ref, i_ref, g_ref, s0_ref, lb_ref, og_ref, sum_ref, mask_ref,
                 o_ref, sout_ref, st_scr, *, n_chunks):
    ci = pl.program_id(1)
    c = CHUNK
    hd = HEAD_DIM

    @pl.when(ci == 0)
    def _():
        for h in range(N_HEADS):
            st_scr[h] = s0_ref[0, h].T

    f = f_ref[...]
    lb = lb_ref[...]
    e = jnp.exp(-jnp.abs(f))
    r = 1.0 / (1.0 + e)
    pos = f >= 0.0
    sig = jnp.where(pos, r, e * r)
    nsig = jnp.where(pos, e * r, r)
    logf = jnp.log(lb + (1.0 - lb) * sig)
    kin = (1.0 - lb) * nsig

    hi = logf.astype(BF16)
    lo = (logf - hi.astype(F32)).astype(BF16)
    expo = _dot(sum_ref[...], jnp.concatenate([hi, lo], axis=0))
    dec = jnp.exp(expo)

    masks = mask_ref[...]
    og = og_ref[...]
    n_lv = len(HGRN_LEVELS)
    for h in range(N_HEADS):
        cs = slice(h * hd, (h + 1) * hd)
        q = q_ref[:, cs].astype(F32)
        k = kin[:, cs]
        v = i_ref[:, cs]
        d0 = dec[0:c, cs]
        d1 = dec[c:2 * c, cs]
        st = st_scr[h]
        o = _dot_nt((q * d0).astype(BF16), st.astype(BF16))
        a = _dot_nt(q.astype(BF16), k.astype(BF16)) * masks[n_lv * c:(n_lv + 1) * c]
        for lv in range(n_lv):
            dl = dec[(2 + lv) * c:(3 + lv) * c, cs]
            a = a + _dot_nt((q * dl).astype(BF16), (k * dl).astype(BF16)) * masks[lv * c:(lv + 1) * c]
        o = o + _dot(a.astype(BF16), v)
        st_scr[h] = st * d0[c - 1:c, :] + _dot_tn(v, (k * d1).astype(BF16))
        ms = jnp.mean(o * o, axis=-1, keepdims=True)
        o = o * lax.rsqrt(ms + EPS) * og * g_ref[:, cs].astype(F32)
        o_ref[:, cs] = o.astype(BF16)

    @pl.when(ci == n_chunks - 1)
    def _():
        for h in range(N_HEADS):
            sout_ref[0, h] = st_scr[h].T


def _hgrn(q, f, i, g, s0, lb, og, batch, seq):
    n, w = q.shape
    nc = seq // CHUNK
    sum_cat, mask_mat = _hgrn_consts()
    row = lambda b, c: (b * nc + c, 0)
    const = lambda b, c: (0, 0)
    return pl.pallas_call(
        functools.partial(_hgrn_kernel, n_chunks=nc),
        out_shape=[jax.ShapeDtypeStruct((n, w), BF16),
                   jax.ShapeDtypeStruct(s0.shape, F32)],
        grid=(batch, nc),
        in_specs=[pl.BlockSpec((CHUNK, w), row),
                  pl.BlockSpec((CHUNK, w), row),
                  pl.BlockSpec((CHUNK, w), row),
                  pl.BlockSpec((CHUNK, w), row),
                  pl.BlockSpec((1,) + s0.shape[1:], lambda b, c: (b, 0, 0, 0)),
                  pl.BlockSpec((1, w), const),
                  pl.BlockSpec((1, HEAD_DIM), const),
                  pl.BlockSpec(sum_cat.shape, const),
                  pl.BlockSpec(mask_mat.shape, const)],
        out_specs=[pl.BlockSpec((CHUNK, w), row),
                   pl.BlockSpec((1,) + s0.shape[1:], lambda b, c: (b, 0, 0, 0))],
        scratch_shapes=[pltpu.VMEM((N_HEADS, HEAD_DIM, HEAD_DIM), F32)],
        compiler_params=_cparams(("arbitrary", "arbitrary")),
        name="hgrn2",
    )(q, f, i, g, s0, lb, og, sum_cat, mask_mat)


def _stack_maps(qt):
    lane = lax.broadcasted_iota(jnp.int32, qt.shape, 1)
    zero = jnp.zeros_like(qt)
    return jnp.concatenate([jnp.where(lane < D_QK, qt, zero), jnp.where(lane >= D_QK, qt, zero)], axis=0)


def _lane_tile(x, width):
    if width < LANES:
        return x[:, :width]
    return jnp.concatenate([x] * (width // LANES), axis=1)


def _softmax_update(s, v, m_ref, l_ref, acc_ref, rows):
    m_prev = m_ref[rows, :]
    m_next = jnp.maximum(m_prev, jnp.max(s, axis=1, keepdims=True))
    alpha = jnp.exp2(m_prev - m_next)
    p = jnp.exp2(s - _lane_tile(m_next, s.shape[1]))
    l_ref[rows, :] = alpha * l_ref[rows, :] + jnp.sum(p, axis=1, keepdims=True)
    acc_ref[rows, :] = alpha * acc_ref[rows, :] + _dot(p.astype(BF16), v)
    m_ref[rows, :] = m_next


def _diag_bias(slope, r0, rows, tq):
    r = r0 + lax.broadcasted_iota(jnp.int32, (rows, tq), 0)
    c = lax.broadcasted_iota(jnp.int32, (rows, tq), 1)
    allowed = (r // CHUNK) >= (c // CHUNK)
    return allowed, slope * (r - jnp.abs(r - c)).astype(F32)


def _lambda(lq1_ref, lk1_ref, lq2_ref, lk2_ref, lam_init):
    return (jnp.exp(jnp.sum(lq1_ref[...] * lk1_ref[...], axis=1, keepdims=True))
            - jnp.exp(jnp.sum(lq2_ref[...] * lk2_ref[...], axis=1, keepdims=True)) + lam_init)


def _attn_finish(acc_ref, l_ref, row0, tq, lam, sg, lam_init):
    a1 = acc_ref[row0:row0 + tq, :] / l_ref[row0:row0 + tq, :]
    a2 = acc_ref[row0 + tq:row0 + 2 * tq, :] / l_ref[row0 + tq:row0 + 2 * tq, :]
    o = a1 - lam * a2
    ms = jnp.mean(o * o, axis=-1, keepdims=True)
    return (o * lax.rsqrt(ms + EPS) * sg * (1.0 - lam_init)).astype(BF16)


def _attn_prompt_kernel(q_ref, k_ref, v_ref, slope_ref, lq1_ref, lk1_ref, lq2_ref, lk2_ref, sg_ref,
                        o_ref, qs, s_a, s_b, p_a, p_b, vx_a, vx_b, al_a, al_b, m_scr, l_scr, acc_scr,
                        *, tq, lam_init):
    qi = pl.program_id(2)
    slope = slope_ref[0, :, 0:1]
    rb = ATTN_ROWS
    qs[...] = _stack_maps(q_ref[...])
    bufs = ((s_a, p_a, vx_a, al_a), (s_b, p_b, vx_b, al_b))
    for _, _, vx, _ in bufs:
        vx[:, HEAD_DIM:] = jnp.ones((tq, LANES), BF16)
    m_scr[...] = jnp.full(m_scr.shape, NEG_BIG, F32)
    l_scr[...] = jnp.zeros(l_scr.shape, F32)
    acc_scr[...] = jnp.zeros(acc_scr.shape, F32)

    def scores(kj):
        return _dot_nt(qs[...], k_ref[pl.ds(kj * tq, tq), :])

    def update(buf, kj, diag):
        s_ref, p_scr, vx, al_scr = bufs[buf]
        if not diag:
            col = lax.broadcasted_iota(jnp.int32, (1, tq), 1)
            bias = slope * (col + (kj - qi) * tq).astype(F32)
        def biased(r0):
            rows = slice(r0, r0 + rb)
            if diag:
                allowed, bias_d = _diag_bias(slope, r0 % tq, rb, tq)
                return jnp.where(allowed, s_ref[rows, :] + bias_d, NEG_BIG)
            return s_ref[rows, :] + bias

        for r0 in range(0, 2 * tq, rb):
            rows = slice(r0, r0 + rb)
            m_prev = m_scr[rows, :]
            m_next = jnp.maximum(m_prev, jnp.max(biased(r0), axis=1, keepdims=True))
            al_scr[rows, :] = jnp.exp2(m_prev - m_next)
            m_scr[rows, :] = m_next
        for r0 in range(0, 2 * tq, rb):
            rows = slice(r0, r0 + rb)
            p_scr[rows, :] = jnp.exp2(biased(r0) - _lane_tile(m_scr[rows, :], tq)).astype(BF16)
        vx[:, 0:HEAD_DIM] = v_ref[pl.ds(kj * tq, tq), :]
        pv = _dot(p_scr[...], vx[...])
        alpha = al_scr[...]
        acc_scr[...] = alpha * acc_scr[...] + pv[:, 0:HEAD_DIM]
        l_scr[...] = alpha * l_scr[...] + pv[:, HEAD_DIM:]

    s_a[...] = scores(0)

    def pair_step(pj, carry):
        kj = 2 * pj
        s_b[...] = scores(kj + 1)
        update(0, kj, False)
        s_a[...] = scores(kj + 2)
        update(1, kj + 1, False)
        return carry

    lax.fori_loop(0, lax.shift_right_logical(qi, 1), pair_step, 0)
    odd = (qi & 1) == 1

    @pl.when(jnp.logical_not(odd))
    def _():
        update(0, qi, True)

    @pl.when(odd)
    def _():
        s_b[...] = scores(qi)
        update(0, qi - 1, False)
        update(1, qi, True)

    lam = _lambda(lq1_ref, lk1_ref, lq2_ref, lk2_ref, lam_init)
    o_ref[...] = _attn_finish(acc_scr, l_scr, 0, tq, lam, sg_ref[...], lam_init)


def _attn_cached_kernel(q_ref, kn_ref, vn_ref, pk_ref, pv_ref, slope_ref,
                        lq1_ref, lk1_ref, lq2_ref, lk2_ref, sg_ref,
                        o_ref, qs, m_scr, l_scr, acc_scr, *, tq, tk, p_len, lam_init):
    hd = HEAD_DIM
    for h in range(N_HEADS):
        qs[2 * tq * h:2 * tq * (h + 1), :] = _stack_maps(q_ref[:, h * hd:(h + 1) * hd])
    m_scr[...] = jnp.full(m_scr.shape, NEG_BIG, F32)
    l_scr[...] = jnp.zeros(l_scr.shape, F32)
    acc_scr[...] = jnp.zeros(acc_scr.shape, F32)
    col = lax.broadcasted_iota(jnp.int32, (1, tk), 1)

    def past_step(kj, carry):
        pos = (col + (kj * tk - p_len)).astype(F32)
        for h in range(N_HEADS):
            rows = slice(2 * tq * h, 2 * tq * (h + 1))
            k = pk_ref[0, pl.ds(kj * (tk * N_HEADS) + h, tk, stride=N_HEADS), :].astype(BF16)
            v = pv_ref[0, pl.ds(kj * (tk * N_HEADS) + h, tk, stride=N_HEADS), :].astype(BF16)
            s = _dot_nt(qs[rows, :], k) + slope_ref[h, :, 0:1] * pos
            _softmax_update(s, v, m_scr, l_scr, acc_scr, rows)
        return carry

    lax.fori_loop(0, p_len // tk, past_step, 0)

    lam = _lambda(lq1_ref, lk1_ref, lq2_ref, lk2_ref, lam_init)
    for h in range(N_HEADS):
        kd = kn_ref[:, h * hd:(h + 1) * hd]
        vd = vn_ref[:, h * hd:(h + 1) * hd]
        allowed, bias = _diag_bias(slope_ref[h, :, 0:1], 0, tq, tq)
        for half in range(2):
            rows = slice(2 * tq * h + half * tq, 2 * tq * h + (half + 1) * tq)
            s = jnp.where(allowed, _dot_nt(qs[rows, :], kd) + bias, NEG_BIG)
            _softmax_update(s, vd, m_scr, l_scr, acc_scr, rows)
        o_ref[:, h * hd:(h + 1) * hd] = _attn_finish(acc_scr, l_scr, 2 * tq * h, tq, lam, sg_ref[...],
                                                     lam_init)


def _alibi_slopes():
    s = (2.0 ** (-8.0 * np.arange(1, N_HEADS + 1) / N_HEADS)) * LOG2E
    return jnp.asarray(np.broadcast_to(s[:, None, None], (N_HEADS, 1, LANES)), F32)


def _attention_prompt(qa, kb, vb, batch, seq, lam_params, subln_g, lam_init):
    n, w = qa.shape
    tq = min(ATTN_TILE, seq)
    nq = seq // tq
    assert seq % tq == 0 and tq % CHUNK == 0
    small = lambda b, h, qi: (0, 0)
    q_spec = pl.BlockSpec((tq, HEAD_DIM), lambda b, h, qi: (b * nq + qi, h))
    kv_spec = pl.BlockSpec((seq, HEAD_DIM), lambda b, h, qi: (b, h))
    return pl.pallas_call(
        functools.partial(_attn_prompt_kernel, tq=tq, lam_init=lam_init),
        out_shape=jax.ShapeDtypeStruct((n, w), BF16),
        grid=(batch, N_HEADS, nq),
        in_specs=[q_spec, kv_spec, kv_spec,
                  pl.BlockSpec((1, 1, LANES), lambda b, h, qi: (h, 0, 0)),
                  pl.BlockSpec((1, D_QK), small), pl.BlockSpec((1, D_QK), small),
                  pl.BlockSpec((1, D_QK), small), pl.BlockSpec((1, D_QK), small),
                  pl.BlockSpec((1, HEAD_DIM), small)],
        out_specs=q_spec,
        scratch_shapes=[pltpu.VMEM((2 * tq, HEAD_DIM), BF16),
                        pltpu.VMEM((2 * tq, tq), F32), pltpu.VMEM((2 * tq, tq), F32),
                        pltpu.VMEM((2 * tq, tq), BF16), pltpu.VMEM((2 * tq, tq), BF16),
                        pltpu.VMEM((tq, HEAD_DIM + LANES), BF16), pltpu.VMEM((tq, HEAD_DIM + LANES), BF16),
                        pltpu.VMEM((2 * tq, LANES), F32), pltpu.VMEM((2 * tq, LANES), F32),
                        pltpu.VMEM((2 * tq, LANES), F32), pltpu.VMEM((2 * tq, LANES), F32),
                        pltpu.VMEM((2 * tq, HEAD_DIM), F32)],
        compiler_params=_cparams(("arbitrary", "arbitrary", "arbitrary")),
        name="diff_attn",
    )(qa, kb, vb, _alibi_slopes(), *lam_params, subln_g)


def _attention_cached(qa, kb, vb, batch, seq, past_k, past_v, lam_params, subln_g, lam_init):
    n, w = qa.shape
    p_len = past_k.shape[1]
    tk = min(SAMPLE_KEY_TILE, p_len)
    assert p_len % tk == 0 and p_len % CHUNK == 0 and seq == CHUNK
    rows = 2 * seq * N_HEADS
    small = lambda b: (0, 0)
    tok_spec = pl.BlockSpec((seq, w), lambda b: (b, 0))
    past_spec = pl.BlockSpec((1, p_len * N_HEADS, HEAD_DIM), lambda b: (b, 0, 0))
    return pl.pallas_call(
        functools.partial(_attn_cached_kernel, tq=seq, tk=tk, p_len=p_len, lam_init=lam_init),
        out_shape=jax.ShapeDtypeStruct((n, w), BF16),
        grid=(batch,),
        in_specs=[tok_spec, tok_spec, tok_spec, past_spec, past_spec,
                  pl.BlockSpec((N_HEADS, 1, LANES), lambda b: (0, 0, 0)),
                  pl.BlockSpec((1, D_QK), small), pl.BlockSpec((1, D_QK), small),
                  pl.BlockSpec((1, D_QK), small), pl.BlockSpec((1, D_QK), small),
                  pl.BlockSpec((1, HEAD_DIM), small)],
        out_specs=tok_spec,
        scratch_shapes=[pltpu.VMEM((rows, HEAD_DIM), BF16), pltpu.VMEM((rows, LANES), F32),
                        pltpu.VMEM((rows, LANES), F32), pltpu.VMEM((rows, HEAD_DIM), F32)],
        compiler_params=_cparams(("arbitrary",)),
        name="diff_attn_cached",
    )(qa, kb, vb, past_k.reshape(batch, p_len * N_HEADS, HEAD_DIM),
      past_v.reshape(batch, p_len * N_HEADS, HEAD_DIM), _alibi_slopes(), *lam_params, subln_g)


def _split_bf16(x):
    hi = x.astype(BF16)
    lo = (x - hi.astype(F32)).astype(BF16)
    return hi, lo


def _out_kernel(orp_ref, oap_ref, xp_ref, modp_ref, ors_ref, oas_ref, xs_ref, mods_ref,
                g2_ref, w_ref, rw_ref, rb_ref, tri_ref,
                x1_ref, h2_ref, route_ref, cnt_ref, cnt_scr, *, n_tiles_p):
    i = pl.program_id(0)

    @pl.when(i == 0)
    def _():
        cnt_scr[...] = jnp.zeros_like(cnt_scr)

    args = (g2_ref, w_ref, rw_ref, rb_ref, tri_ref, x1_ref, h2_ref, route_ref, cnt_scr)

    @pl.when(i < n_tiles_p)
    def _():
        _out_tile(orp_ref, oap_ref, xp_ref, modp_ref, *args)

    @pl.when(i >= n_tiles_p)
    def _():
        _out_tile(ors_ref, oas_ref, xs_ref, mods_ref, *args)

    cnt_ref[...] = jnp.broadcast_to(cnt_scr[...], cnt_ref.shape)


def _out_tile(or_ref, oa_ref, x_ref, mod_ref, g2_ref, w_ref, rw_ref, rb_ref, tri_ref,
              x1_ref, h2_ref, route_ref, cnt_scr):
    bb, t, d = x_ref.shape
    tm = bb * t
    half = or_ref.shape[1]
    mix = _dot(or_ref[...], w_ref[0:half, :]) + _dot(oa_ref[...], w_ref[half:, :])
    m = mod_ref[...]
    x1 = x_ref[...] + m[:, 2:3, :] * mix.reshape(bb, t, d)
    ms = jnp.mean(x1 * x1, axis=-1, keepdims=True)
    h2 = x1 * lax.rsqrt(ms + EPS) * g2_ref[...] * (1.0 + m[:, 4:5, :]) + m[:, 3:4, :]
    x1_ref[...] = x1.reshape(tm, d)
    h2 = h2.reshape(tm, d)
    slabs = d // LANES
    for j in range(slabs):
        h2_ref[pl.ds(j, tm, stride=slabs), :] = h2[:, j * LANES:(j + 1) * LANES]

    hh, hl = _split_bf16(h2)
    wh, wl = _split_bf16(rw_ref[...])
    logits = _dot(hh, wh) + _dot(hl, wh) + _dot(hh, wl) + rb_ref[...]
    lane = lax.broadcasted_iota(jnp.int32, logits.shape, 1)
    neg = jnp.float32(-jnp.inf)
    big = jnp.int32(1 << 20)

    is_g = (lane >= N_EXPERTS) & (lane < N_EXPERTS + N_GROUPS)
    gl = jnp.where(is_g, logits, neg)
    gmax = jnp.max(gl, axis=1, keepdims=True)
    gidx = jnp.min(jnp.where(gl == gmax, lane - N_EXPERTS, big), axis=1, keepdims=True)
    g_prob = 1.0 / jnp.sum(jnp.exp(gl - gmax), axis=1, keepdims=True)

    in_grp = (lane < N_EXPERTS) & ((lane // EXPERTS_PER_GROUP) == gidx)
    el = jnp.where(in_grp, logits, neg)
    m1 = jnp.max(el, axis=1, keepdims=True)
    i1 = jnp.min(jnp.where(el == m1, lane, big), axis=1, keepdims=True)
    el2 = jnp.where(lane == i1, neg, el)
    m2 = jnp.max(el2, axis=1, keepdims=True)
    i2 = jnp.min(jnp.where(el2 == m2, lane, big), axis=1, keepdims=True)
    e21 = jnp.exp(m2 - m1)
    w1 = g_prob / (1.0 + e21)
    w2 = g_prob * e21 / (1.0 + e21)
    tri = tri_ref[...]
    oh1 = jnp.where(lane == i1, 1.0, 0.0)
    oh2 = jnp.where(lane == i2, 1.0, 0.0)
    cnt = cnt_scr[...]
    c1 = jnp.sum(oh1, axis=0, keepdims=True)
    c2 = jnp.sum(oh2, axis=0, keepdims=True)
    r1 = jnp.sum(oh1 * (_dot(tri, oh1.astype(BF16)) + cnt), axis=1, keepdims=True)
    r2 = jnp.sum(oh2 * (_dot(tri, oh2.astype(BF16)) + cnt + c1), axis=1, keepdims=True)
    cnt_scr[...] = cnt + c1 + c2

    vals = (i1.astype(F32), i2.astype(F32), w1, w2, r1, r2)
    route = jnp.zeros(logits.shape, F32)
    for k, v in enumerate(vals):
        route = jnp.where(lane == k, v, route)
    route_ref[...] = route


def _out_proj(o_rp, o_ap, xp, mod_p, o_rs, o_as, xs, mod_s, g2, w_out_b, rw, rb):
    d = xp.shape[2]
    half = o_rp.shape[1]
    n_p = xp.shape[0] * xp.shape[1]
    n_s = xs.shape[0] * xs.shape[1]
    n = n_p + n_s
    bbp, tp = _row_tiling(xp.shape[0], xp.shape[1], ROW_TILE)
    bbs, ts = _row_tiling(xs.shape[0], xs.shape[1], ROW_TILE)
    tm = bbp * tp
    assert bbs * ts == tm and n_p % tm == 0 and n_s % tm == 0
    ntp = n_p // tm
    nts = n_s // tm
    modp_tiles = (xp.shape[1] // tp) if bbp == 1 else 1
    mods_tiles = (xs.shape[1] // ts) if bbs == 1 else 1
    pi = lambda i: jnp.minimum(i, ntp - 1)
    si = lambda i: jnp.maximum(i - ntp, 0)
    tri = jnp.asarray(np.tril(np.ones((tm, tm), np.float32), -1), BF16)
    row = lambda i: (i, 0)
    const = lambda i: (0, 0)
    return pl.pallas_call(
        functools.partial(_out_kernel, n_tiles_p=ntp),
        out_shape=[jax.ShapeDtypeStruct((n, d), F32), jax.ShapeDtypeStruct((n * (d // LANES), LANES), F32),
                   jax.ShapeDtypeStruct((n, LANES), F32), jax.ShapeDtypeStruct((8, LANES), F32)],
        grid=(ntp + nts,),
        in_specs=[pl.BlockSpec((tm, half), lambda i: (pi(i), 0)),
                  pl.BlockSpec((tm, half), lambda i: (pi(i), 0)),
                  pl.BlockSpec((bbp, tp, d), lambda i: (pi(i), 0, 0)),
                  pl.BlockSpec((bbp, 6, d), lambda i: (pi(i) // modp_tiles, 0, 0)),
                  pl.BlockSpec((tm, half), lambda i: (si(i), 0)),
                  pl.BlockSpec((tm, half), lambda i: (si(i), 0)),
                  pl.BlockSpec((bbs, ts, d), lambda i: (si(i), 0, 0)),
                  pl.BlockSpec((bbs, 6, d), lambda i: (si(i) // mods_tiles, 0, 0)),
                  pl.BlockSpec((1, d), const),
                  pl.BlockSpec(w_out_b.shape, const),
                  pl.BlockSpec(rw.shape, const),
                  pl.BlockSpec((1, LANES), const),
                  pl.BlockSpec((tm, tm), const)],
        out_specs=[pl.BlockSpec((tm, d), row), pl.BlockSpec((tm * (d // LANES), LANES), row),
                   pl.BlockSpec((tm, LANES), row), pl.BlockSpec((8, LANES), const)],
        scratch_shapes=[pltpu.VMEM((1, LANES), F32)],
        compiler_params=_cparams(("arbitrary",)),
        name="out_proj_router",
    )(o_rp, o_ap, xp.reshape(n_p // tp, tp, d), mod_p,
      o_rs, o_as, xs.reshape(n_s // ts, ts, d), mod_s, g2, w_out_b, rw, rb, tri)


ROW_PACK_SHIFT = 16
MXU_COLS = 256
ROW_PITCH = 24


def _moe_kernel(pack_ref, be_ref, nu_ref,
                h2_hbm, wg_ref, wu_ref, wd_ref, out_hbm,
                xbuf, obuf, xb, hb, wgb, wub, wdb, gsem, ssem, *, n_blk):
    i = pl.program_id(0)
    n_used = nu_ref[0]
    bm, d = xb.shape
    slabs = d // LANES
    de = wgb.shape[1]
    n_up = de // MXU_COLS
    n_down = d // MXU_COLS
    slot = lax.rem(i, 2)
    other = 1 - slot
    dst_mask = (1 << ROW_PACK_SHIFT) - 1

    def gather_copy(tok, r, s):
        src = pl.multiple_of(tok * slabs, slabs)
        return pltpu.make_async_copy(h2_hbm.at[pl.ds(src, slabs)],
                                     xbuf.at[s, pl.ds(r * ROW_PITCH, slabs)], gsem.at[s])

    def scatter_copy(dst, r, s):
        dst = pl.multiple_of(dst * slabs, slabs)
        return pltpu.make_async_copy(obuf.at[s, pl.ds(r * ROW_PITCH, slabs)],
                                     out_hbm.at[pl.ds(dst, slabs)], ssem.at[s])

    def start_gather(blk, s, rows):
        for r in rows:
            tok = lax.shift_right_logical(pack_ref[blk * bm + r], ROW_PACK_SHIFT)
            gather_copy(tok, r, s).start(priority=r % 2)

    def start_scatter(blk, s, rows):
        for r in rows:
            scatter_copy(pack_ref[blk * bm + r] & dst_mask, r, s).start(priority=r % 2)

    def wait_gather(s):
        for r in range(bm):
            gather_copy(0, r, s).wait()

    def wait_scatter(s):
        for r in range(bm):
            scatter_copy(0, r, s).wait()

    @pl.when(i < n_used)
    def _():
        @pl.when(i == 0)
        def _():
            start_gather(0, 0, range(bm))
            obuf[...] = jnp.zeros_like(obuf)
            spare0 = out_hbm.shape[0] - 2 * bm * slabs
            fills = [pltpu.make_async_copy(obuf.at[s, pl.ds(0, bm * slabs)],
                                           out_hbm.at[pl.ds(spare0 + s * bm * slabs, bm * slabs)],
                                           ssem.at[s]) for s in range(2)]
            for c in fills:
                c.start()
            for c in fills:
                c.wait()

        new_expert = jnp.logical_or(i == 0, be_ref[i] != be_ref[jnp.maximum(i - 1, 0)])

        @pl.when(new_expert)
        def _():
            wgb[...] = wg_ref[0].astype(BF16)
            wub[...] = wu_ref[0].astype(BF16)
            wdb[...] = wd_ref[0].astype(BF16)

        wait_gather(slot)
        for j in range(slabs):
            xb[:, j * LANES:(j + 1) * LANES] = xbuf[slot, pl.ds(j, bm, stride=ROW_PITCH), :].astype(BF16)
        nxt = jnp.minimum(i + 1, n_blk - 1)
        prv = jnp.where(i == 0, n_blk, i - 1)

        def copy_batch(k, n_batches, base):
            per = (bm // 2) // n_batches
            rows = range(base + k * per, base + (k + 1) * per)
            start_gather(nxt, other, rows)
            start_scatter(prv, other, rows)

        for c in range(n_up):
            cols = slice(c * MXU_COLS, (c + 1) * MXU_COLS)
            hb[:, cols] = (_silu(_dot(xb[...], wgb[:, cols])) * _dot(xb[...], wub[:, cols])).astype(BF16)
            copy_batch(c, n_up, 0)

        @pl.when(i >= 1)
        def _():
            wait_scatter(slot)

        for c in range(n_down):
            y = _dot(hb[...], wdb[:, c * MXU_COLS:(c + 1) * MXU_COLS])
            for jj in range(MXU_COLS // LANES):
                j = c * (MXU_COLS // LANES) + jj
                obuf[slot, pl.ds(j, bm, stride=ROW_PITCH), :] = y[:, jj * LANES:(jj + 1) * LANES]
            copy_batch(c, n_down, bm // 2)

        @pl.when(i == n_used - 1)
        def _():
            wait_gather(other)
            start_scatter(i, slot, range(bm))
            wait_scatter(other)
            wait_scatter(slot)


def _moe(h2, route, counts, wg, wu, wd):
    n = route.shape[0]
    d = wg.shape[1]
    slabs = d // LANES
    assert h2.shape == (n * slabs, LANES)
    assert TOP_K == 2
    s_rows = n * TOP_K
    bm = MOE_ROWS
    n_blk = -(-s_rows // bm) + N_EXPERTS
    n_pad = n_blk * bm
    n_out = s_rows + 2 * bm
    assert n < (1 << (31 - ROW_PACK_SHIFT)) and n_out <= (1 << ROW_PACK_SHIFT)

    counts = counts[0, :N_EXPERTS].astype(jnp.int32)
    padded = (counts + bm - 1) // bm * bm
    pad_end = jnp.cumsum(padded)
    pad_start = pad_end - padded
    expert = route[:, 0:TOP_K].astype(jnp.int32)
    rank = route[:, 2 * TOP_K:3 * TOP_K].astype(jnp.int32)
    onehot = expert[:, :, None] == jnp.arange(N_EXPERTS, dtype=jnp.int32)[None, None, :]
    dest = jnp.sum(jnp.where(onehot, pad_start[None, None, :], 0), axis=2) + rank
    tok = jnp.arange(n, dtype=jnp.int32)
    valid = (tok[:, None] << ROW_PACK_SHIFT) | (jnp.arange(TOP_K, dtype=jnp.int32)[None, :] * n
                                                + tok[:, None])
    pos = jnp.arange(n_pad + bm, dtype=jnp.int32)
    blk_par = jnp.where(pos < n_pad, (pos // bm) % 2, 1)
    spare = s_rows + blk_par * bm + pos % bm
    row_pack = spare.at[dest.reshape(-1)].set(valid.reshape(-1))
    blk_start = jnp.arange(n_blk, dtype=jnp.int32) * bm
    blk_e = jnp.minimum(jnp.sum((pad_end[None, :] <= blk_start[:, None]).astype(jnp.int32), axis=1),
                        N_EXPERTS - 1)
    n_used = (pad_end[-1] // bm).astype(jnp.int32).reshape(1)

    de = wg.shape[2]
    assert de % MXU_COLS == 0 and d % MXU_COLS == 0 and slabs <= ROW_PITCH
    assert (bm // 2) % (de // MXU_COLS) == 0 and (bm // 2) % (d // MXU_COLS) == 0
    wmap = lambda i, pack, be, nu: (be[i], 0, 0)
    grid_spec = pltpu.PrefetchScalarGridSpec(
        num_scalar_prefetch=3,
        grid=(n_blk,),
        in_specs=[pl.BlockSpec(memory_space=pl.ANY),
                  pl.BlockSpec((1, d, de), wmap),
                  pl.BlockSpec((1, d, de), wmap),
                  pl.BlockSpec((1, de, d), wmap)],
        out_specs=pl.BlockSpec(memory_space=pl.ANY),
        scratch_shapes=[pltpu.VMEM((2, bm * ROW_PITCH, LANES), F32),
                        pltpu.VMEM((2, bm * ROW_PITCH, LANES), F32),
                        pltpu.VMEM((bm, d), BF16), pltpu.VMEM((bm, de), BF16),
                        pltpu.VMEM((d, de), BF16), pltpu.VMEM((d, de), BF16),
                        pltpu.VMEM((de, d), BF16),
                        pltpu.SemaphoreType.DMA((2,)), pltpu.SemaphoreType.DMA((2,))])
    return pl.pallas_call(
        functools.partial(_moe_kernel, n_blk=n_blk),
        out_shape=jax.ShapeDtypeStruct((n_out * slabs, LANES), F32),
        grid_spec=grid_spec,
        compiler_params=_cparams(("arbitrary",)),
        name="moe_experts",
    )(row_pack, blk_e, n_used, h2, wg, wu, wd)


def _final_kernel(x1_ref, r0_ref, r1_ref, route_ref, mod_ref, y_ref):
    bb, t, d = x1_ref.shape
    tm = bb * t
    slabs = d // LANES
    w = route_ref[...]
    w0 = w[:, :, TOP_K:TOP_K + 1]
    w1 = w[:, :, TOP_K + 1:TOP_K + 2]
    g_f = mod_ref[...][:, 5:6, :]
    for j in range(slabs):
        cols = slice(j * LANES, (j + 1) * LANES)
        r0 = r0_ref[pl.ds(j, tm, stride=slabs), :].reshape(bb, t, LANES)
        r1 = r1_ref[pl.ds(j, tm, stride=slabs), :].reshape(bb, t, LANES)
        y_ref[:, :, cols] = x1_ref[:, :, cols] + g_f[:, :, cols] * (w0 * r0 + w1 * r1)


def _final(x1, rows, route, mod, batch, seq, row_off):
    n_all, d = x1.shape
    slabs = d // LANES
    n = batch * seq
    bb, t = _row_tiling(batch, seq, ROW_TILE)
    tm = bb * t
    tiles_per_mod = (seq // t) if bb == 1 else 1
    assert row_off % tm == 0 and n_all % tm == 0 and rows.shape[0] % (tm * slabs) == 0
    off = row_off // tm
    slot1 = n_all // tm
    return pl.pallas_call(
        _final_kernel,
        out_shape=jax.ShapeDtypeStruct((n // t, t, d), F32),
        grid=(n // tm,),
        in_specs=[pl.BlockSpec((bb, t, d), lambda i: (i + off, 0, 0)),
                  pl.BlockSpec((tm * slabs, LANES), lambda i: (i + off, 0)),
                  pl.BlockSpec((tm * slabs, LANES), lambda i: (i + off + slot1, 0)),
                  pl.BlockSpec((bb, t, LANES), lambda i: (i + off, 0, 0)),
                  pl.BlockSpec((bb, 6, d), lambda i: (i // tiles_per_mod, 0, 0))],
        out_specs=pl.BlockSpec((bb, t, d), lambda i: (i, 0, 0)),
        compiler_params=_cparams(("arbitrary",)),
        name="final_residual",
    )(x1.reshape(-1, t, d), rows, rows, route.reshape(-1, t, LANES), mod).reshape(batch, seq, d)


def _mixer_group(x, mod, past_k, past_v, s0, lyr):
    batch, seq, d = x.shape
    q, f, i, g, qa, kf, kb, vf, vb = _in_proj(x, mod, lyr["g1"], lyr["w_in"], lyr["qg"], lyr["kg"],
                                               lyr["bd"])
    o_r, s_new = _hgrn(q, f, i, g, s0, lyr["lb"], lyr["og"], batch, seq)
    if past_k is None:
        o_a = _attention_prompt(qa, kb, vb, batch, seq, lyr["lam_params"], lyr["sg"], lyr["lam_init"])
    else:
        o_a = _attention_cached(qa, kb, vb, batch, seq, past_k, past_v, lyr["lam_params"], lyr["sg"],
                                lyr["lam_init"])
    k_new = kf.reshape(batch, seq, N_HEADS, HEAD_DIM)
    v_new = vf.reshape(batch, seq, N_HEADS, HEAD_DIM)
    return o_r, o_a, k_new, v_new, s_new


def kernel(x_prompt, x_sample, cache_k_attn, cache_v_attn, state_hgrn, c_prompt, c_sample, w_ada, b_ada, norm1_g, norm2_g, w_in, w_out, hgrn_lb_param, hgrn_onorm_g, attn_qnorm_g, attn_knorm_g, lambda_q1, lambda_k1, lambda_q2, lambda_k2, attn_subln_g, router_group_w, router_group_b, router_expert_w, router_expert_b, expert_w_gate, expert_w_up, expert_w_down):
    depth = w_ada.shape[0]
    bp, tp, d = x_prompt.shape
    bs, ts, _ = x_sample.shape
    d_h = N_HEADS * HEAD_DIM
    lb_all = jnp.cumsum(jax.nn.softmax(hgrn_lb_param.astype(F32), axis=0), axis=0)
    bd = jnp.asarray(np.kron(np.eye(LANES // D_QK), np.ones((D_QK, D_QK))), BF16)

    xp, xs = x_prompt, x_sample
    kp, vp, sp, ksm, vsm, ssm = [], [], [], [], [], []
    for l in range(depth):
        wl = w_in[l]
        qk = wl[:, 4 * d_h:4 * d_h + 4 * N_HEADS * D_QK].reshape(d, 2, 2, N_HEADS, D_QK)
        qk = qk.transpose(0, 1, 3, 2, 4).reshape(d, 4 * N_HEADS * D_QK)
        w_in_b = jnp.concatenate([wl[:, :4 * d_h], qk, wl[:, 4 * d_h + 4 * N_HEADS * D_QK:]],
                                 axis=1).astype(BF16)
        rw = jnp.zeros((d, LANES), F32)
        rw = rw.at[:, :N_EXPERTS].set(router_expert_w[l])
        rw = rw.at[:, N_EXPERTS:N_EXPERTS + N_GROUPS].set(router_group_w[l])
        rb = jnp.zeros((1, LANES), F32)
        rb = rb.at[0, :N_EXPERTS].set(router_expert_b[l])
        rb = rb.at[0, N_EXPERTS:N_EXPERTS + N_GROUPS].set(router_group_b[l])
        lyr = dict(
            g1=norm1_g[l].reshape(1, d), g2=norm2_g[l].reshape(1, d),
            w_in=w_in_b, w_out=w_out[l].astype(BF16),
            qg=jnp.tile(attn_qnorm_g[l], 2).reshape(1, LANES),
            kg=jnp.tile(attn_knorm_g[l], 2).reshape(1, LANES),
            bd=bd, lb=lb_all[l].reshape(1, d_h), og=hgrn_onorm_g[l].reshape(1, HEAD_DIM),
            sg=attn_subln_g[l].reshape(1, HEAD_DIM),
            lam_params=[p[l].reshape(1, D_QK) for p in (lambda_q1, lambda_k1, lambda_q2, lambda_k2)],
            lam_init=0.8 - 0.6 * math.exp(-0.3 * l),
            rw=rw, rb=rb)

        c_all = jnp.concatenate([c_prompt, c_sample], axis=0)
        rows = c_all.shape[0]
        rows_pad = -(-rows // 8) * 8
        mod = _ada(jnp.pad(c_all, ((0, rows_pad - rows), (0, 0))), w_ada[l], b_ada[l])
        mod_p = mod[:bp].reshape(bp, 6, d)
        mod_s = mod[bp:bp + bs].reshape(bs, 6, d)

        s0_p = jnp.zeros((bp, N_HEADS, HEAD_DIM, HEAD_DIM), F32)
        o_rp, o_ap, k_n, v_n, s_n = _mixer_group(xp, mod_p, None, None, s0_p, lyr)
        kp.append(k_n)
        vp.append(v_n)
        sp.append(s_n)
        o_rs, o_as, k_n, v_n, s_n = _mixer_group(xs, mod_s, cache_k_attn[l], cache_v_attn[l],
                                                 state_hgrn[l], lyr)
        ksm.append(k_n)
        vsm.append(v_n)
        ssm.append(s_n)

        x1, h2, route, counts = _out_proj(o_rp, o_ap, xp, mod_p, o_rs, o_as, xs, mod_s,
                                          lyr["g2"], lyr["w_out"], lyr["rw"], lyr["rb"])
        rows = _moe(h2, route, counts, expert_w_gate[l], expert_w_up[l], expert_w_down[l])
        xp = _final(x1, rows, route, mod_p, bp, tp, 0)
        xs = _final(x1, rows, route, mod_s, bs, ts, bp * tp)
    return (xp, xs, jnp.stack(kp), jnp.stack(vp), jnp.stack(sp),
            jnp.stack(ksm), jnp.stack(vsm), jnp.stack(ssm))
```

```python
import functools
import math

import numpy as np
import jax
import jax.numpy as jnp
from jax import lax
from jax.experimental import pallas as pl
from jax.experimental.pallas import tpu as pltpu

F32 = jnp.float32
BF16 = jnp.bfloat16

N_HEADS = 8
HEAD_DIM = 128
D_QK = 64
CHUNK = 64
N_GROUPS = 4
EXPERTS_PER_GROUP = 8
N_EXPERTS = N_GROUPS * EXPERTS_PER_GROUP
TOP_K = 2
EPS = 1e-6
NEG_BIG = -1e30
LOG2E = math.log2(math.e)

LANES = 128
ROW_TILE = 512
ATTN_TILE = 512
ATTN_ROWS = 64
SAMPLE_KEY_TILE = 512
MOE_ROWS = 256
VMEM_LIMIT = 56 * 1024 * 1024

HGRN_LEVELS = (32, 16, 8, 4, 2, 1)


def _cparams(sem, vmem=VMEM_LIMIT):
    return pltpu.CompilerParams(dimension_semantics=sem, vmem_limit_bytes=vmem)


def _dot(a, b):
    return jnp.dot(a, b, preferred_element_type=F32)


def _dot_nt(a, b):
    return lax.dot_general(a, b, (((1,), (1,)), ((), ())), preferred_element_type=F32)


def _dot_tn(a, b):
    return lax.dot_general(a, b, (((0,), (0,)), ((), ())), preferred_element_type=F32)


def _silu(x):
    return x / (1.0 + jnp.exp(-x))


def _row_tiling(batch, seq, target):
    if seq >= target:
        assert seq % target == 0
        return 1, target
    bb = max(1, min(batch, target // seq))
    while batch % bb:
        bb -= 1
    return bb, seq


def _ada_kernel(c_ref, w_ref, b_ref, o_ref):
    s = _silu(c_ref[...]).astype(BF16)
    o_ref[...] = _dot(s, w_ref[...].astype(BF16)) + b_ref[...]


def _ada(c_all, w_ada, b_ada):
    rows, d = c_all.shape
    n_out = w_ada.shape[1]
    tn = 1024
    return pl.pallas_call(
        _ada_kernel,
        out_shape=jax.ShapeDtypeStruct((rows, n_out), F32),
        grid=(n_out // tn,),
        in_specs=[pl.BlockSpec((rows, d), lambda j: (0, 0)),
                  pl.BlockSpec((d, tn), lambda j: (0, j)),
                  pl.BlockSpec((1, tn), lambda j: (0, j))],
        out_specs=pl.BlockSpec((rows, tn), lambda j: (0, j)),
        compiler_params=_cparams(("arbitrary",)),
        name="ada_mod",
    )(c_all, w_ada, b_ada.reshape(1, n_out))


def _group_norm64(acc, bd, gain):
    sq = (acc * acc).astype(BF16)
    parts = []
    for t in range(acc.shape[1] // LANES):
        ss = _dot(sq[:, t * LANES:(t + 1) * LANES], bd)
        a = acc[:, t * LANES:(t + 1) * LANES]
        parts.append(a * lax.rsqrt(ss * (1.0 / D_QK) + EPS) * gain)
    return jnp.concatenate(parts, axis=1)


def _in_kernel(x_ref, mod_ref, g1_ref, w_ref, qg_ref, kg_ref, bd_ref,
               q_ref, f_ref, i_ref, g_ref, qa_ref, kf_ref, kb_ref, vf_ref, vb_ref,
               h_scr):
    j = pl.program_id(1)
    tm, d = h_scr.shape

    @pl.when(j == 0)
    def _():
        x = x_ref[...]
        ms = jnp.mean(x * x, axis=-1, keepdims=True)
        xn = x * lax.rsqrt(ms + EPS) * g1_ref[...]
        m = mod_ref[...]
        h = xn * (1.0 + m[:, 1:2, :]) + m[:, 0:1, :]
        h_scr[...] = h.reshape(tm, d).astype(BF16)

    acc = _dot(h_scr[...], w_ref[...])

    @pl.when(j == 0)
    def _():
        q_ref[...] = _silu(acc).astype(BF16)

    @pl.when(j == 1)
    def _():
        f_ref[...] = acc

    @pl.when(j == 2)
    def _():
        i_ref[...] = acc.astype(BF16)

    @pl.when(j == 3)
    def _():
        g_ref[...] = _silu(acc).astype(BF16)

    @pl.when(j == 4)
    def _():
        qa_ref[...] = (_group_norm64(acc, bd_ref[...], qg_ref[...]) * (LOG2E * D_QK ** -0.5)).astype(BF16)

    @pl.when(j == 5)
    def _():
        kn = _group_norm64(acc, bd_ref[...], kg_ref[...])
        kf_ref[...] = kn
        kb_ref[...] = kn.astype(BF16)

    @pl.when(j == 6)
    def _():
        vf_ref[...] = acc
        vb_ref[...] = acc.astype(BF16)


def _in_proj(x, mod, g1, w_in_b, qg, kg, bd):
    batch, seq, d = x.shape
    n = batch * seq
    bb, t = _row_tiling(batch, seq, ROW_TILE)
    tm = bb * t
    tiles_per_mod = (seq // t) if bb == 1 else 1
    sec = 1024
    n_sec = w_in_b.shape[1] // sec
    xv = x.reshape(n // t, t, d)
    row = lambda i, j: (i, 0)
    out_dt = [BF16, F32, BF16, BF16, BF16, F32, BF16, F32, BF16]
    return pl.pallas_call(
        _in_kernel,
        out_shape=[jax.ShapeDtypeStruct((n, sec), dt) for dt in out_dt],
        grid=(n // tm, n_sec),
        in_specs=[pl.BlockSpec((bb, t, d), lambda i, j: (i, 0, 0)),
                  pl.BlockSpec((bb, 6, d), lambda i, j: (i // tiles_per_mod, 0, 0)),
                  pl.BlockSpec((1, d), lambda i, j: (0, 0)),
                  pl.BlockSpec((d, sec), lambda i, j: (0, j)),
                  pl.BlockSpec((1, LANES), lambda i, j: (0, 0)),
                  pl.BlockSpec((1, LANES), lambda i, j: (0, 0)),
                  pl.BlockSpec((LANES, LANES), lambda i, j: (0, 0))],
        out_specs=[pl.BlockSpec((tm, sec), row) for _ in out_dt],
        scratch_shapes=[pltpu.VMEM((tm, d), BF16)],
        compiler_params=_cparams(("arbitrary", "arbitrary")),
        name="in_proj",
    )(xv, mod, g1, w_in_b, qg, kg, bd)


def _hgrn_consts():
    c = CHUNK
    t = np.arange(c)[:, None]
    s = np.arange(c)[None, :]
    sums = [(s <= t), (s > t)]
    masks = []
    for m in HGRN_LEVELS:
        start = (t // (2 * m)) * (2 * m)
        ref = start + m - 1
        lower = (t % (2 * m)) >= m
        sums.append(np.where(lower, (s > ref) & (s <= t), (s > t) & (s <= ref)))
        s_start = (s // (2 * m)) * (2 * m)
        masks.append((s_start == start) & lower & ((s % (2 * m)) < m))
    masks.append(s == t)
    sum_mat = np.concatenate(sums, axis=0).astype(np.float32)
    sum_cat = np.concatenate([sum_mat, sum_mat], axis=1)
    mask_mat = np.concatenate(masks, axis=0).astype(np.float32)
    return jnp.asarray(sum_cat, BF16), jnp.asarray(mask_mat, F32)


def _hgrn_kernel(q_ref, f_ref, i_ref, g_ref, s0_ref, lb_ref, og_ref, sum_ref, mask_ref,
                 o_ref, sout_ref, st_scr, *, n_chunks):
    ci = pl.program_id(1)
    c = CHUNK
    hd = HEAD_DIM

    @pl.when(ci == 0)
    def _():
        for h in range(N_HEADS):
            st_scr[h] = s0_ref[0, h].T

    f = f_ref[...]
    lb = lb_ref[...]
    e = jnp.exp(-jnp.abs(f))
    r = 1.0 / (1.0 + e)
    pos = f >= 0.0
    sig = jnp.where(pos, r, e * r)
    nsig = jnp.where(pos, e * r, r)
    logf = jnp.log(lb + (1.0 - lb) * sig)
    kin = (1.0 - lb) * nsig

    hi = logf.astype(BF16)
    lo = (logf - hi.astype(F32)).astype(BF16)
    expo = _dot(sum_ref[...], jnp.concatenate([hi, lo], axis=0))
    dec = jnp.exp(expo)

    masks = mask_ref[...]
    og = og_ref[...]
    n_lv = len(HGRN_LEVELS)
    for h in range(N_HEADS):
        cs = slice(h * hd, (h + 1) * hd)
        q = q_ref[:, cs].astype(F32)
        k = kin[:, cs]
        v = i_ref[:, cs]
        d0 = dec[0:c, cs]
        d1 = dec[c:2 * c, cs]
        st = st_scr[h]
        o = _dot_nt((q * d0).astype(BF16), st.astype(BF16))
        a = _dot_nt(q.astype(BF16), k.astype(BF16)) * masks[n_lv * c:(n_lv + 1) * c]
        for lv in range(n_lv):
            dl = dec[(2 + lv) * c:(3 + lv) * c, cs]
            a = a + _dot_nt((q * dl).astype(BF16), (k * dl).astype(BF16)) * masks[lv * c:(lv + 1) * c]
        o = o + _dot(a.astype(BF16), v)
        st_scr[h] = st * d0[c - 1:c, :] + _dot_tn(v, (k * d1).astype(BF16))
        ms = jnp.mean(o * o, axis=-1, keepdims=True)
        o = o * lax.rsqrt(ms + EPS) * og * g_ref[:, cs].astype(F32)
        o_ref[:, cs] = o.astype(BF16)

    @pl.when(ci == n_chunks - 1)
    def _():
        for h in range(N_HEADS):
            sout_ref[0, h] = st_scr[h].T


def _hgrn(q, f, i, g, s0, lb, og, batch, seq):
    n, w = q.shape
    nc = seq // CHUNK
    sum_cat, mask_mat = _hgrn_consts()
    row = lambda b, c: (b * nc + c, 0)
    const = lambda b, c: (0, 0)
    return pl.pallas_call(
        functools.partial(_hgrn_kernel, n_chunks=nc),
        out_shape=[jax.ShapeDtypeStruct((n, w), BF16),
                   jax.ShapeDtypeStruct(s0.shape, F32)],
        grid=(batch, nc),
        in_specs=[pl.BlockSpec((CHUNK, w), row),
                  pl.BlockSpec((CHUNK, w), row),
                  pl.BlockSpec((CHUNK, w), row),
                  pl.BlockSpec((CHUNK, w), row),
                  pl.BlockSpec((1,) + s0.shape[1:], lambda b, c: (b, 0, 0, 0)),
                  pl.BlockSpec((1, w), const),
                  pl.BlockSpec((1, HEAD_DIM), const),
                  pl.BlockSpec(sum_cat.shape, const),
                  pl.BlockSpec(mask_mat.shape, const)],
        out_specs=[pl.BlockSpec((CHUNK, w), row),
                   pl.BlockSpec((1,) + s0.shape[1:], lambda b, c: (b, 0, 0, 0))],
        scratch_shapes=[pltpu.VMEM((N_HEADS, HEAD_DIM, HEAD_DIM), F32)],
        compiler_params=_cparams(("arbitrary", "arbitrary")),
        name="hgrn2",
    )(q, f, i, g, s0, lb, og, sum_cat, mask_mat)


def _stack_maps(qt):
    lane = lax.broadcasted_iota(jnp.int32, qt.shape, 1)
    zero = jnp.zeros_like(qt)
    return jnp.concatenate([jnp.where(lane < D_QK, qt, zero), jnp.where(lane >= D_QK, qt, zero)], axis=0)


def _lane_tile(x, width):
    if width < LANES:
        return x[:, :width]
    return jnp.concatenate([x] * (width // LANES), axis=1)


def _softmax_update(s, v, m_ref, l_ref, acc_ref, rows):
    m_prev = m_ref[rows, :]
    m_next = jnp.maximum(m_prev, jnp.max(s, axis=1, keepdims=True))
    alpha = jnp.exp2(m_prev - m_next)
    p = jnp.exp2(s - _lane_tile(m_next, s.shape[1]))
    l_ref[rows, :] = alpha * l_ref[rows, :] + jnp.sum(p, axis=1, keepdims=True)
    acc_ref[rows, :] = alpha * acc_ref[rows, :] + _dot(p.astype(BF16), v)
    m_ref[rows, :] = m_next


def _diag_bias(slope, r0, rows, tq):
    r = r0 + lax.broadcasted_iota(jnp.int32, (rows, tq), 0)
    c = lax.broadcasted_iota(jnp.int32, (rows, tq), 1)
    allowed = (r // CHUNK) >= (c // CHUNK)
    return allowed, slope * (r - jnp.abs(r - c)).astype(F32)


def _lambda(lq1_ref, lk1_ref, lq2_ref, lk2_ref, lam_init):
    return (jnp.exp(jnp.sum(lq1_ref[...] * lk1_ref[...], axis=1, keepdims=True))
            - jnp.exp(jnp.sum(lq2_ref[...] * lk2_ref[...], axis=1, keepdims=True)) + lam_init)


def _attn_finish(acc_ref, l_ref, row0, tq, lam, sg, lam_init):
    a1 = acc_ref[row0:row0 + tq, :] / l_ref[row0:row0 + tq, :]
    a2 = acc_ref[row0 + tq:row0 + 2 * tq, :] / l_ref[row0 + tq:row0 + 2 * tq, :]
    o = a1 - lam * a2
    ms = jnp.mean(o * o, axis=-1, keepdims=True)
    return (o * lax.rsqrt(ms + EPS) * sg * (1.0 - lam_init)).astype(BF16)


def _attn_prompt_kernel(q_ref, k_ref, v_ref, slope_ref, lq1_ref, lk1_ref, lq2_ref, lk2_ref, sg_ref,
                        o_ref, qs, s_a, s_b, p_a, p_b, vx_a, vx_b, al_a, al_b, m_scr, l_scr, acc_scr,
                        *, tq, lam_init):
    qi = pl.program_id(2)
    slope = slope_ref[0, :, 0:1]
    rb = ATTN_ROWS
    qs[...] = _stack_maps(q_ref[...])
    bufs = ((s_a, p_a, vx_a, al_a), (s_b, p_b, vx_b, al_b))
    for _, _, vx, _ in bufs:
        vx[:, HEAD_DIM:] = jnp.ones((tq, LANES), BF16)
    m_scr[...] = jnp.full(m_scr.shape, NEG_BIG, F32)
    l_scr[...] = jnp.zeros(l_scr.shape, F32)
    acc_scr[...] = jnp.zeros(acc_scr.shape, F32)

    def scores(kj):
        return _dot_nt(qs[...], k_ref[pl.ds(kj * tq, tq), :])

    def update(buf, kj, diag):
        s_ref, p_scr, vx, al_scr = bufs[buf]
        if not diag:
            col = lax.broadcasted_iota(jnp.int32, (1, tq), 1)
            bias = slope * (col + (kj - qi) * tq).astype(F32)
        def biased(r0):
            rows = slice(r0, r0 + rb)
            if diag:
                allowed, bias_d = _diag_bias(slope, r0 % tq, rb, tq)
                return jnp.where(allowed, s_ref[rows, :] + bias_d, NEG_BIG)
            return s_ref[rows, :] + bias

        for r0 in range(0, 2 * tq, rb):
            rows = slice(r0, r0 + rb)
            m_prev = m_scr[rows, :]
            m_next = jnp.maximum(m_prev, jnp.max(biased(r0), axis=1, keepdims=True))
            al_scr[rows, :] = jnp.exp2(m_prev - m_next)
            m_scr[rows, :] = m_next
        for r0 in range(0, 2 * tq, rb):
            rows = slice(r0, r0 + rb)
            p_scr[rows, :] = jnp.exp2(biased(r0) - _lane_tile(m_scr[rows, :], tq)).astype(BF16)
        vx[:, 0:HEAD_DIM] = v_ref[pl.ds(kj * tq, tq), :]
        pv = _dot(p_scr[...], vx[...])
        alpha = al_scr[...]
        acc_scr[...] = alpha * acc_scr[...] + pv[:, 0:HEAD_DIM]
        l_scr[...] = alpha * l_scr[...] + pv[:, HEAD_DIM:]

    s_a[...] = scores(0)

    def pair_step(pj, carry):
        kj = 2 * pj
        s_b[...] = scores(kj + 1)
        update(0, kj, False)
        s_a[...] = scores(kj + 2)
        update(1, kj + 1, False)
        return carry

    lax.fori_loop(0, lax.shift_right_logical(qi, 1), pair_step, 0)
    odd = (qi & 1) == 1

    @pl.when(jnp.logical_not(odd))
    def _():
        update(0, qi, True)

    @pl.when(odd)
    def _():
        s_b[...] = scores(qi)
        update(0, qi - 1, False)
        update(1, qi, True)

    lam = _lambda(lq1_ref, lk1_ref, lq2_ref, lk2_ref, lam_init)
    o_ref[...] = _attn_finish(acc_scr, l_scr, 0, tq, lam, sg_ref[...], lam_init)


def _attn_cached_kernel(q_ref, kn_ref, vn_ref, pk_ref, pv_ref, slope_ref,
                        lq1_ref, lk1_ref, lq2_ref, lk2_ref, sg_ref,
                        o_ref, qs, m_scr, l_scr, acc_scr, *, tq, tk, p_len, lam_init):
    hd = HEAD_DIM
    for h in range(N_HEADS):
        qs[2 * tq * h:2 * tq * (h + 1), :] = _stack_maps(q_ref[:, h * hd:(h + 1) * hd])
    m_scr[...] = jnp.full(m_scr.shape, NEG_BIG, F32)
    l_scr[...] = jnp.zeros(l_scr.shape, F32)
    acc_scr[...] = jnp.zeros(acc_scr.shape, F32)
    col = lax.broadcasted_iota(jnp.int32, (1, tk), 1)

    def past_step(kj, carry):
        pos = (col + (kj * tk - p_len)).astype(F32)
        for h in range(N_HEADS):
            rows = slice(2 * tq * h, 2 * tq * (h + 1))
            k = pk_ref[0, pl.ds(kj * (tk * N_HEADS) + h, tk, stride=N_HEADS), :].astype(BF16)
            v = pv_ref[0, pl.ds(kj * (tk * N_HEADS) + h, tk, stride=N_HEADS), :].astype(BF16)
            s = _dot_nt(qs[rows, :], k) + slope_ref[h, :, 0:1] * pos
            _softmax_update(s, v, m_scr, l_scr, acc_scr, rows)
        return carry

    lax.fori_loop(0, p_len // tk, past_step, 0)

    lam = _lambda(lq1_ref, lk1_ref, lq2_ref, lk2_ref, lam_init)
    for h in range(N_HEADS):
        kd = kn_ref[:, h * hd:(h + 1) * hd]
        vd = vn_ref[:, h * hd:(h + 1) * hd]
        allowed, bias = _diag_bias(slope_ref[h, :, 0:1], 0, tq, tq)
        for half in range(2):
            rows = slice(2 * tq * h + half * tq, 2 * tq * h + (half + 1) * tq)
            s = jnp.where(allowed, _dot_nt(qs[rows, :], kd) + bias, NEG_BIG)
            _softmax_update(s, vd, m_scr, l_scr, acc_scr, rows)
        o_ref[:, h * hd:(h + 1) * hd] = _attn_finish(acc_scr, l_scr, 2 * tq * h, tq, lam, sg_ref[...],
                                                     lam_init)


def _alibi_slopes():
    s = (2.0 ** (-8.0 * np.arange(1, N_HEADS + 1) / N_HEADS)) * LOG2E
    return jnp.asarray(np.broadcast_to(s[:, None, None], (N_HEADS, 1, LANES)), F32)


def _attention_prompt(qa, kb, vb, batch, seq, lam_params, subln_g, lam_init):
    n, w = qa.shape
    tq = min(ATTN_TILE, seq)
    nq = seq // tq
    assert seq % tq == 0 and tq % CHUNK == 0
    small = lambda b, h, qi: (0, 0)
    q_spec = pl.BlockSpec((tq, HEAD_DIM), lambda b, h, qi: (b * nq + qi, h))
    kv_spec = pl.BlockSpec((seq, HEAD_DIM), lambda b, h, qi: (b, h))
    return pl.pallas_call(
        functools.partial(_attn_prompt_kernel, tq=tq, lam_init=lam_init),
        out_shape=jax.ShapeDtypeStruct((n, w), BF16),
        grid=(batch, N_HEADS, nq),
        in_specs=[q_spec, kv_spec, kv_spec,
                  pl.BlockSpec((1, 1, LANES), lambda b, h, qi: (h, 0, 0)),
                  pl.BlockSpec((1, D_QK), small), pl.BlockSpec((1, D_QK), small),
                  pl.BlockSpec((1, D_QK), small), pl.BlockSpec((1, D_QK), small),
                  pl.BlockSpec((1, HEAD_DIM), small)],
        out_specs=q_spec,
        scratch_shapes=[pltpu.VMEM((2 * tq, HEAD_DIM), BF16),
                        pltpu.VMEM((2 * tq, tq), F32), pltpu.VMEM((2 * tq, tq), F32),
                        pltpu.VMEM((2 * tq, tq), BF16), pltpu.VMEM((2 * tq, tq), BF16),
                        pltpu.VMEM((tq, HEAD_DIM + LANES), BF16), pltpu.VMEM((tq, HEAD_DIM + LANES), BF16),
                        pltpu.VMEM((2 * tq, LANES), F32), pltpu.VMEM((2 * tq, LANES), F32),
                        pltpu.VMEM((2 * tq, LANES), F32), pltpu.VMEM((2 * tq, LANES), F32),
                        pltpu.VMEM((2 * tq, HEAD_DIM), F32)],
        compiler_params=_cparams(("arbitrary", "arbitrary", "arbitrary")),
        name="diff_attn",
    )(qa, kb, vb, _alibi_slopes(), *lam_params, subln_g)


def _attention_cached(qa, kb, vb, batch, seq, past_k, past_v, lam_params, subln_g, lam_init):
    n, w = qa.shape
    p_len = past_k.shape[1]
    tk = min(SAMPLE_KEY_TILE, p_len)
    assert p_len % tk == 0 and p_len % CHUNK == 0 and seq == CHUNK
    rows = 2 * seq * N_HEADS
    small = lambda b: (0, 0)
    tok_spec = pl.BlockSpec((seq, w), lambda b: (b, 0))
    past_spec = pl.BlockSpec((1, p_len * N_HEADS, HEAD_DIM), lambda b: (b, 0, 0))
    return pl.pallas_call(
        functools.partial(_attn_cached_kernel, tq=seq, tk=tk, p_len=p_len, lam_init=lam_init),
        out_shape=jax.ShapeDtypeStruct((n, w), BF16),
        grid=(batch,),
        in_specs=[tok_spec, tok_spec, tok_spec, past_spec, past_spec,
                  pl.BlockSpec((N_HEADS, 1, LANES), lambda b: (0, 0, 0)),
                  pl.BlockSpec((1, D_QK), small), pl.BlockSpec((1, D_QK), small),
                  pl.BlockSpec((1, D_QK), small), pl.BlockSpec((1, D_QK), small),
                  pl.BlockSpec((1, HEAD_DIM), small)],
        out_specs=tok_spec,
        scratch_shapes=[pltpu.VMEM((rows, HEAD_DIM), BF16), pltpu.VMEM((rows, LANES), F32),
                        pltpu.VMEM((rows, LANES), F32), pltpu.VMEM((rows, HEAD_DIM), F32)],
        compiler_params=_cparams(("arbitrary",)),
        name="diff_attn_cached",
    )(qa, kb, vb, past_k.reshape(batch, p_len * N_HEADS, HEAD_DIM),
      past_v.reshape(batch, p_len * N_HEADS, HEAD_DIM), _alibi_slopes(), *lam_params, subln_g)


def _split_bf16(x):
    hi = x.astype(BF16)
    lo = (x - hi.astype(F32)).astype(BF16)
    return hi, lo


def _out_kernel(orp_ref, oap_ref, xp_ref, modp_ref, ors_ref, oas_ref, xs_ref, mods_ref,
                g2_ref, w_ref, rw_ref, rb_ref, tri_ref,
                x1_ref, h2_ref, route_ref, cnt_ref, cnt_scr, *, n_tiles_p):
    i = pl.program_id(0)

    @pl.when(i == 0)
    def _():
        cnt_scr[...] = jnp.zeros_like(cnt_scr)

    args = (g2_ref, w_ref, rw_ref, rb_ref, tri_ref, x1_ref, h2_ref, route_ref, cnt_scr)

    @pl.when(i < n_tiles_p)
    def _():
        _out_tile(orp_ref, oap_ref, xp_ref, modp_ref, *args)

    @pl.when(i >= n_tiles_p)
    def _():
        _out_tile(ors_ref, oas_ref, xs_ref, mods_ref, *args)

    cnt_ref[...] = jnp.broadcast_to(cnt_scr[...], cnt_ref.shape)


def _out_tile(or_ref, oa_ref, x_ref, mod_ref, g2_ref, w_ref, rw_ref, rb_ref, tri_ref,
              x1_ref, h2_ref, route_ref, cnt_scr):
    bb, t, d = x_ref.shape
    tm = bb * t
    half = or_ref.shape[1]
    mix = _dot(or_ref[...], w_ref[0:half, :]) + _dot(oa_ref[...], w_ref[half:, :])
    m = mod_ref[...]
    x1 = x_ref[...] + m[:, 2:3, :] * mix.reshape(bb, t, d)
    ms = jnp.mean(x1 * x1, axis=-1, keepdims=True)
    h2 = x1 * lax.rsqrt(ms + EPS) * g2_ref[...] * (1.0 + m[:, 4:5, :]) + m[:, 3:4, :]
    x1_ref[...] = x1.reshape(tm, d)
    h2 = h2.reshape(tm, d)
    slabs = d // LANES
    for j in range(slabs):
        h2_ref[pl.ds(j, tm, stride=slabs), :] = h2[:, j * LANES:(j + 1) * LANES]

    hh, hl = _split_bf16(h2)
    wh, wl = _split_bf16(rw_ref[...])
    logits = _dot(hh, wh) + _dot(hl, wh) + _dot(hh, wl) + rb_ref[...]
    lane = lax.broadcasted_iota(jnp.int32, logits.shape, 1)
    neg = jnp.float32(-jnp.inf)
    big = jnp.int32(1 << 20)

    is_g = (lane >= N_EXPERTS) & (lane < N_EXPERTS + N_GROUPS)
    gl = jnp.where(is_g, logits, neg)
    gmax = jnp.max(gl, axis=1, keepdims=True)
    gidx = jnp.min(jnp.where(gl == gmax, lane - N_EXPERTS, big), axis=1, keepdims=True)
    g_prob = 1.0 / jnp.sum(jnp.exp(gl - gmax), axis=1, keepdims=True)

    in_grp = (lane < N_EXPERTS) & ((lane // EXPERTS_PER_GROUP) == gidx)
    el = jnp.where(in_grp, logits, neg)
    m1 = jnp.max(el, axis=1, keepdims=True)
    i1 = jnp.min(jnp.where(el == m1, lane, big), axis=1, keepdims=True)
    el2 = jnp.where(lane == i1, neg, el)
    m2 = jnp.max(el2, axis=1, keepdims=True)
    i2 = jnp.min(jnp.where(el2 == m2, lane, big), axis=1, keepdims=True)
    e21 = jnp.exp(m2 - m1)
    w1 = g_prob / (1.0 + e21)
    w2 = g_prob * e21 / (1.0 + e21)
    tri = tri_ref[...]
    oh1 = jnp.where(lane == i1, 1.0, 0.0)
    oh2 = jnp.where(lane == i2, 1.0, 0.0)
    cnt = cnt_scr[...]
    c1 = jnp.sum(oh1, axis=0, keepdims=True)
    c2 = jnp.sum(oh2, axis=0, keepdims=True)
    r1 = jnp.sum(oh1 * (_dot(tri, oh1.astype(BF16)) + cnt), axis=1, keepdims=True)
    r2 = jnp.sum(oh2 * (_dot(tri, oh2.astype(BF16)) + cnt + c1), axis=1, keepdims=True)
    cnt_scr[...] = cnt + c1 + c2

    vals = (i1.astype(F32), i2.astype(F32), w1, w2, r1, r2)
    route = jnp.zeros(logits.shape, F32)
    for k, v in enumerate(vals):
        route = jnp.where(lane == k, v, route)
    route_ref[...] = route


def _out_proj(o_rp, o_ap, xp, mod_p, o_rs, o_as, xs, mod_s, g2, w_out_b, rw, rb):
    d = xp.shape[2]
    half = o_rp.shape[1]
    n_p = xp.shape[0] * xp.shape[1]
    n_s = xs.shape[0] * xs.shape[1]
    n = n_p + n_s
    bbp, tp = _row_tiling(xp.shape[0], xp.shape[1], ROW_TILE)
    bbs, ts = _row_tiling(xs.shape[0], xs.shape[1], ROW_TILE)
    tm = bbp * tp
    assert bbs * ts == tm and n_p % tm == 0 and n_s % tm == 0
    ntp = n_p // tm
    nts = n_s // tm
    modp_tiles = (xp.shape[1] // tp) if bbp == 1 else 1
    mods_tiles = (xs.shape[1] // ts) if bbs == 1 else 1
    pi = lambda i: jnp.minimum(i, ntp - 1)
    si = lambda i: jnp.maximum(i - ntp, 0)
    tri = jnp.asarray(np.tril(np.ones((tm, tm), np.float32), -1), BF16)
    row = lambda i: (i, 0)
    const = lambda i: (0, 0)
    return pl.pallas_call(
        functools.partial(_out_kernel, n_tiles_p=ntp),
        out_shape=[jax.ShapeDtypeStruct((n, d), F32), jax.ShapeDtypeStruct((n * (d // LANES), LANES), F32),
                   jax.ShapeDtypeStruct((n, LANES), F32), jax.ShapeDtypeStruct((8, LANES), F32)],
        grid=(ntp + nts,),
        in_specs=[pl.BlockSpec((tm, half), lambda i: (pi(i), 0)),
                  pl.BlockSpec((tm, half), lambda i: (pi(i), 0)),
                  pl.BlockSpec((bbp, tp, d), lambda i: (pi(i), 0, 0)),
                  pl.BlockSpec((bbp, 6, d), lambda i: (pi(i) // modp_tiles, 0, 0)),
                  pl.BlockSpec((tm, half), lambda i: (si(i), 0)),
                  pl.BlockSpec((tm, half), lambda i: (si(i), 0)),
                  pl.BlockSpec((bbs, ts, d), lambda i: (si(i), 0, 0)),
                  pl.BlockSpec((bbs, 6, d), lambda i: (si(i) // mods_tiles, 0, 0)),
                  pl.BlockSpec((1, d), const),
                  pl.BlockSpec(w_out_b.shape, const),
                  pl.BlockSpec(rw.shape, const),
                  pl.BlockSpec((1, LANES), const),
                  pl.BlockSpec((tm, tm), const)],
        out_specs=[pl.BlockSpec((tm, d), row), pl.BlockSpec((tm * (d // LANES), LANES), row),
                   pl.BlockSpec((tm, LANES), row), pl.BlockSpec((8, LANES), const)],
        scratch_shapes=[pltpu.VMEM((1, LANES), F32)],
        compiler_params=_cparams(("arbitrary",)),
        name="out_proj_router",
    )(o_rp, o_ap, xp.reshape(n_p // tp, tp, d), mod_p,
      o_rs, o_as, xs.reshape(n_s // ts, ts, d), mod_s, g2, w_out_b, rw, rb, tri)


ROW_PACK_SHIFT = 16
MXU_COLS = 256
ROW_PITCH = 24
MOE_GATHER_BUFS = 3


def _moe_kernel(pack_ref, be_ref, nu_ref,
                h2_hbm, wg_ref, wu_ref, wd_ref, out_hbm,
                xbuf, obuf, xb, hb, wgb, wub, wdb, gsem, ssem, *, n_blk):
    i = pl.program_id(0)
    n_used = nu_ref[0]
    bm, d = xb.shape
    slabs = d // LANES
    de = wgb.shape[1]
    n_up = de // MXU_COLS
    n_down = d // MXU_COLS
    n_gbuf = xbuf.shape[0]
    gslot = lax.rem(i, n_gbuf)
    gnext = lax.rem(i + n_gbuf - 1, n_gbuf)
    slot = lax.rem(i, 2)
    other = 1 - slot
    dst_mask = (1 << ROW_PACK_SHIFT) - 1

    def gather_copy(tok, r, s):
        src = pl.multiple_of(tok * slabs, slabs)
        return pltpu.make_async_copy(h2_hbm.at[pl.ds(src, slabs)],
                                     xbuf.at[s, pl.ds(r * ROW_PITCH, slabs)], gsem.at[s])

    def scatter_copy(dst, r, s):
        dst = pl.multiple_of(dst * slabs, slabs)
        return pltpu.make_async_copy(obuf.at[s, pl.ds(r * ROW_PITCH, slabs)],
                                     out_hbm.at[pl.ds(dst, slabs)], ssem.at[s])

    def start_gather(blk, s, rows):
        for r in rows:
            tok = lax.shift_right_logical(pack_ref[blk * bm + r], ROW_PACK_SHIFT)
            gather_copy(tok, r, s).start(priority=r % 2)

    def start_scatter(blk, s, rows):
        for r in rows:
            scatter_copy(pack_ref[blk * bm + r] & dst_mask, r, s).start(priority=r % 2)

    def wait_gather(s):
        for r in range(bm):
            gather_copy(0, r, s).wait()

    def wait_scatter(s):
        for r in range(bm):
            scatter_copy(0, r, s).wait()

    @pl.when(i < n_used)
    def _():
        @pl.when(i == 0)
        def _():
            for b in range(n_gbuf - 1):
                start_gather(min(b, n_blk - 1), b, range(bm))
            obuf[...] = jnp.zeros_like(obuf)
            spare0 = out_hbm.shape[0] - 2 * bm * slabs
            fills = [pltpu.make_async_copy(obuf.at[s, pl.ds(0, bm * slabs)],
                                           out_hbm.at[pl.ds(spare0 + s * bm * slabs, bm * slabs)],
                                           ssem.at[s]) for s in range(2)]
            for c in fills:
                c.start()
            for c in fills:
                c.wait()

        new_expert = jnp.logical_or(i == 0, be_ref[i] != be_ref[jnp.maximum(i - 1, 0)])

        @pl.when(new_expert)
        def _():
            wgb[...] = wg_ref[0].astype(BF16)
            wub[...] = wu_ref[0].astype(BF16)
            wdb[...] = wd_ref[0].astype(BF16)

        wait_gather(gslot)
        for j in range(slabs):
            xb[:, j * LANES:(j + 1) * LANES] = xbuf[gslot, pl.ds(j, bm, stride=ROW_PITCH), :].astype(BF16)
        nxt = jnp.minimum(i + n_gbuf - 1, n_blk - 1)
        prv = jnp.where(i == 0, n_blk, i - 1)

        def copy_batch(k, n_batches, base):
            per = (bm // 2) // n_batches
            rows = range(base + k * per, base + (k + 1) * per)
            start_gather(nxt, gnext, rows)
            start_scatter(prv, other, rows)

        for c in range(n_up):
            cols = slice(c * MXU_COLS, (c + 1) * MXU_COLS)
            hb[:, cols] = (_silu(_dot(xb[...], wgb[:, cols])) * _dot(xb[...], wub[:, cols])).astype(BF16)
            copy_batch(c, n_up, 0)

        @pl.when(i >= 1)
        def _():
            wait_scatter(slot)

        for c in range(n_down):
            y = _dot(hb[...], wdb[:, c * MXU_COLS:(c + 1) * MXU_COLS])
            for jj in range(MXU_COLS // LANES):
                j = c * (MXU_COLS // LANES) + jj
                obuf[slot, pl.ds(j, bm, stride=ROW_PITCH), :] = y[:, jj * LANES:(jj + 1) * LANES]
            copy_batch(c, n_down, bm // 2)

        @pl.when(i == n_used - 1)
        def _():
            for b in range(1, n_gbuf):
                wait_gather(lax.rem(i + b, n_gbuf))
            start_scatter(i, slot, range(bm))
            wait_scatter(other)
            wait_scatter(slot)


def _moe(h2, route, counts, wg, wu, wd):
    n = route.shape[0]
    d = wg.shape[1]
    slabs = d // LANES
    assert h2.shape == (n * slabs, LANES)
    assert TOP_K == 2
    s_rows = n * TOP_K
    bm = MOE_ROWS
    n_blk = -(-s_rows // bm) + N_EXPERTS
    n_pad = n_blk * bm
    n_out = s_rows + 2 * bm
    assert n < (1 << (31 - ROW_PACK_SHIFT)) and n_out <= (1 << ROW_PACK_SHIFT)

    counts = counts[0, :N_EXPERTS].astype(jnp.int32)
    padded = (counts + bm - 1) // bm * bm
    pad_end = jnp.cumsum(padded)
    pad_start = pad_end - padded
    expert = route[:, 0:TOP_K].astype(jnp.int32)
    rank = route[:, 2 * TOP_K:3 * TOP_K].astype(jnp.int32)
    onehot = expert[:, :, None] == jnp.arange(N_EXPERTS, dtype=jnp.int32)[None, None, :]
    dest = jnp.sum(jnp.where(onehot, pad_start[None, None, :], 0), axis=2) + rank
    tok = jnp.arange(n, dtype=jnp.int32)
    valid = (tok[:, None] << ROW_PACK_SHIFT) | (jnp.arange(TOP_K, dtype=jnp.int32)[None, :] * n
                                                + tok[:, None])
    pos = jnp.arange(n_pad + bm, dtype=jnp.int32)
    blk_par = jnp.where(pos < n_pad, (pos // bm) % 2, 1)
    spare = s_rows + blk_par * bm + pos % bm
    row_pack = spare.at[dest.reshape(-1)].set(valid.reshape(-1))
    blk_start = jnp.arange(n_blk, dtype=jnp.int32) * bm
    blk_e = jnp.minimum(jnp.sum((pad_end[None, :] <= blk_start[:, None]).astype(jnp.int32), axis=1),
                        N_EXPERTS - 1)
    n_used = (pad_end[-1] // bm).astype(jnp.int32).reshape(1)

    de = wg.shape[2]
    assert de % MXU_COLS == 0 and d % MXU_COLS == 0 and slabs <= ROW_PITCH
    assert (bm // 2) % (de // MXU_COLS) == 0 and (bm // 2) % (d // MXU_COLS) == 0
    wmap = lambda i, pack, be, nu: (be[i], 0, 0)
    grid_spec = pltpu.PrefetchScalarGridSpec(
        num_scalar_prefetch=3,
        grid=(n_blk,),
        in_specs=[pl.BlockSpec(memory_space=pl.ANY),
                  pl.BlockSpec((1, d, de), wmap),
                  pl.BlockSpec((1, d, de), wmap),
                  pl.BlockSpec((1, de, d), wmap)],
        out_specs=pl.BlockSpec(memory_space=pl.ANY),
        scratch_shapes=[pltpu.VMEM((MOE_GATHER_BUFS, bm * ROW_PITCH, LANES), F32),
                        pltpu.VMEM((2, bm * ROW_PITCH, LANES), F32),
                        pltpu.VMEM((bm, d), BF16), pltpu.VMEM((bm, de), BF16),
                        pltpu.VMEM((d, de), BF16), pltpu.VMEM((d, de), BF16),
                        pltpu.VMEM((de, d), BF16),
                        pltpu.SemaphoreType.DMA((MOE_GATHER_BUFS,)), pltpu.SemaphoreType.DMA((2,))])
    return pl.pallas_call(
        functools.partial(_moe_kernel, n_blk=n_blk),
        out_shape=jax.ShapeDtypeStruct((n_out * slabs, LANES), F32),
        grid_spec=grid_spec,
        compiler_params=_cparams(("arbitrary",)),
        name="moe_experts",
    )(row_pack, blk_e, n_used, h2, wg, wu, wd)


def _final_kernel(x1_ref, r0_ref, r1_ref, route_ref, mod_ref, y_ref):
    bb, t, d = x1_ref.shape
    tm = bb * t
    slabs = d // LANES
    w = route_ref[...]
    w0 = w[:, :, TOP_K:TOP_K + 1]
    w1 = w[:, :, TOP_K + 1:TOP_K + 2]
    g_f = mod_ref[...][:, 5:6, :]
    for j in range(slabs):
        cols = slice(j * LANES, (j + 1) * LANES)
        r0 = r0_ref[pl.ds(j, tm, stride=slabs), :].reshape(bb, t, LANES)
        r1 = r1_ref[pl.ds(j, tm, stride=slabs), :].reshape(bb, t, LANES)
        y_ref[:, :, cols] = x1_ref[:, :, cols] + g_f[:, :, cols] * (w0 * r0 + w1 * r1)


def _final(x1, rows, route, mod, batch, seq, row_off):
    n_all, d = x1.shape
    slabs = d // LANES
    n = batch * seq
    bb, t = _row_tiling(batch, seq, ROW_TILE)
    tm = bb * t
    tiles_per_mod = (seq // t) if bb == 1 else 1
    assert row_off % tm == 0 and n_all % tm == 0 and rows.shape[0] % (tm * slabs) == 0
    off = row_off // tm
    slot1 = n_all // tm
    return pl.pallas_call(
        _final_kernel,
        out_shape=jax.ShapeDtypeStruct((n // t, t, d), F32),
        grid=(n // tm,),
        in_specs=[pl.BlockSpec((bb, t, d), lambda i: (i + off, 0, 0)),
                  pl.BlockSpec((tm * slabs, LANES), lambda i: (i + off, 0)),
                  pl.BlockSpec((tm * slabs, LANES), lambda i: (i + off + slot1, 0)),
                  pl.BlockSpec((bb, t, LANES), lambda i: (i + off, 0, 0)),
                  pl.BlockSpec((bb, 6, d), lambda i: (i // tiles_per_mod, 0, 0))],
        out_specs=pl.BlockSpec((bb, t, d), lambda i: (i, 0, 0)),
        compiler_params=_cparams(("arbitrary",)),
        name="final_residual",
    )(x1.reshape(-1, t, d), rows, rows, route.reshape(-1, t, LANES), mod).reshape(batch, seq, d)


def _mixer_group(x, mod, past_k, past_v, s0, lyr):
    batch, seq, d = x.shape
    q, f, i, g, qa, kf, kb, vf, vb = _in_proj(x, mod, lyr["g1"], lyr["w_in"], lyr["qg"], lyr["kg"],
                                               lyr["bd"])
    o_r, s_new = _hgrn(q, f, i, g, s0, lyr["lb"], lyr["og"], batch, seq)
    if past_k is None:
        o_a = _attention_prompt(qa, kb, vb, batch, seq, lyr["lam_params"], lyr["sg"], lyr["lam_init"])
    else:
        o_a = _attention_cached(qa, kb, vb, batch, seq, past_k, past_v, lyr["lam_params"], lyr["sg"],
                                lyr["lam_init"])
    k_new = kf.reshape(batch, seq, N_HEADS, HEAD_DIM)
    v_new = vf.reshape(batch, seq, N_HEADS, HEAD_DIM)
    return o_r, o_a, k_new, v_new, s_new


def kernel(x_prompt, x_sample, cache_k_attn, cache_v_attn, state_hgrn, c_prompt, c_sample, w_ada, b_ada, norm1_g, norm2_g, w_in, w_out, hgrn_lb_param, hgrn_onorm_g, attn_qnorm_g, attn_knorm_g, lambda_q1, lambda_k1, lambda_q2, lambda_k2, attn_subln_g, router_group_w, router_group_b, router_expert_w, router_expert_b, expert_w_gate, expert_w_up, expert_w_down):
    depth = w_ada.shape[0]
    bp, tp, d = x_prompt.shape
    bs, ts, _ = x_sample.shape
    d_h = N_HEADS * HEAD_DIM
    lb_all = jnp.cumsum(jax.nn.softmax(hgrn_lb_param.astype(F32), axis=0), axis=0)
    bd = jnp.asarray(np.kron(np.eye(LANES // D_QK), np.ones((D_QK, D_QK))), BF16)

    xp, xs = x_prompt, x_sample
    kp, vp, sp, ksm, vsm, ssm = [], [], [], [], [], []
    for l in range(depth):
        wl = w_in[l]
        qk = wl[:, 4 * d_h:4 * d_h + 4 * N_HEADS * D_QK].reshape(d, 2, 2, N_HEADS, D_QK)
        qk = qk.transpose(0, 1, 3, 2, 4).reshape(d, 4 * N_HEADS * D_QK)
        w_in_b = jnp.concatenate([wl[:, :4 * d_h], qk, wl[:, 4 * d_h + 4 * N_HEADS * D_QK:]],
                                 axis=1).astype(BF16)
        rw = jnp.zeros((d, LANES), F32)
        rw = rw.at[:, :N_EXPERTS].set(router_expert_w[l])
        rw = rw.at[:, N_EXPERTS:N_EXPERTS + N_GROUPS].set(router_group_w[l])
        rb = jnp.zeros((1, LANES), F32)
        rb = rb.at[0, :N_EXPERTS].set(router_expert_b[l])
        rb = rb.at[0, N_EXPERTS:N_EXPERTS + N_GROUPS].set(router_group_b[l])
        lyr = dict(
            g1=norm1_g[l].reshape(1, d), g2=norm2_g[l].reshape(1, d),
            w_in=w_in_b, w_out=w_out[l].astype(BF16),
            qg=jnp.tile(attn_qnorm_g[l], 2).reshape(1, LANES),
            kg=jnp.tile(attn_knorm_g[l], 2).reshape(1, LANES),
            bd=bd, lb=lb_all[l].reshape(1, d_h), og=hgrn_onorm_g[l].reshape(1, HEAD_DIM),
            sg=attn_subln_g[l].reshape(1, HEAD_DIM),
            lam_params=[p[l].reshape(1, D_QK) for p in (lambda_q1, lambda_k1, lambda_q2, lambda_k2)],
            lam_init=0.8 - 0.6 * math.exp(-0.3 * l),
            rw=rw, rb=rb)

        c_all = jnp.concatenate([c_prompt, c_sample], axis=0)
        rows = c_all.shape[0]
        rows_pad = -(-rows // 8) * 8
        mod = _ada(jnp.pad(c_all, ((0, rows_pad - rows), (0, 0))), w_ada[l], b_ada[l])
        mod_p = mod[:bp].reshape(bp, 6, d)
        mod_s = mod[bp:bp + bs].reshape(bs, 6, d)

        s0_p = jnp.zeros((bp, N_HEADS, HEAD_DIM, HEAD_DIM), F32)
        o_rp, o_ap, k_n, v_n, s_n = _mixer_group(xp, mod_p, None, None, s0_p, lyr)
        kp.append(k_n)
        vp.append(v_n)
        sp.append(s_n)
        o_rs, o_as, k_n, v_n, s_n = _mixer_group(xs, mod_s, cache_k_attn[l], cache_v_attn[l],
                                                 state_hgrn[l], lyr)
        ksm.append(k_n)
        vsm.append(v_n)
        ssm.append(s_n)

        x1, h2, route, counts = _out_proj(o_rp, o_ap, xp, mod_p, o_rs, o_as, xs, mod_s,
                                          lyr["g2"], lyr["w_out"], lyr["rw"], lyr["rb"])
        rows = _moe(h2, route, counts, expert_w_gate[l], expert_w_up[l], expert_w_down[l])
        xp = _final(x1, rows, route, mod_p, bp, tp, 0)
        xs = _final(x1, rows, route, mod_s, bs, ts, bp * tp)
    return (xp, xs, jnp.stack(kp), jnp.stack(vp), jnp.stack(sp),
            jnp.stack(ksm), jnp.stack(vsm), jnp.stack(ssm))
```

```python
import functools
import math

import numpy as np
import jax
import jax.numpy as jnp
from jax import lax
from jax.experimental import pallas as pl
from jax.experimental.pallas import tpu as pltpu

F32 = jnp.float32
BF16 = jnp.bfloat16

N_HEADS = 8
HEAD_DIM = 128
D_QK = 64
CHUNK = 64
N_GROUPS = 4
EXPERTS_PER_GROUP = 8
N_EXPERTS = N_GROUPS * EXPERTS_PER_GROUP
TOP_K = 2
EPS = 1e-6
NEG_BIG = -1e30
LOG2E = math.log2(math.e)

LANES = 128
MXU_COLS = 256
ROW_TILE = 512
ATTN_TILE = 512
ATTN_ROWS = 64
SAMPLE_KEY_TILE = 512
MOE_ROWS = 256
VMEM_LIMIT = 56 * 1024 * 1024

HGRN_LEVELS = (32, 16, 8, 4, 2, 1)
HGRN_CHUNKS_PER_STEP = 2


def _cparams(sem, vmem=VMEM_LIMIT):
    return pltpu.CompilerParams(dimension_semantics=sem, vmem_limit_bytes=vmem)


def _dot(a, b):
    return jnp.dot(a, b, preferred_element_type=F32)


def _dot_nt(a, b):
    return lax.dot_general(a, b, (((1,), (1,)), ((), ())), preferred_element_type=F32)


def _dot_tn(a, b):
    return lax.dot_general(a, b, (((0,), (0,)), ((), ())), preferred_element_type=F32)


def _silu(x):
    return x / (1.0 + jnp.exp(-x))


def _row_tiling(batch, seq, target):
    if seq >= target:
        assert seq % target == 0
        return 1, target
    bb = max(1, min(batch, target // seq))
    while batch % bb:
        bb -= 1
    return bb, seq


def _ada_kernel(c_ref, w_ref, b_ref, o_ref):
    s = _silu(c_ref[...]).astype(BF16)
    o_ref[...] = _dot(s, w_ref[...].astype(BF16)) + b_ref[...]


def _ada(c_all, w_ada, b_ada):
    rows, d = c_all.shape
    n_out = w_ada.shape[1]
    tn = 1024
    return pl.pallas_call(
        _ada_kernel,
        out_shape=jax.ShapeDtypeStruct((rows, n_out), F32),
        grid=(n_out // tn,),
        in_specs=[pl.BlockSpec((rows, d), lambda j: (0, 0)),
                  pl.BlockSpec((d, tn), lambda j: (0, j)),
                  pl.BlockSpec((1, tn), lambda j: (0, j))],
        out_specs=pl.BlockSpec((rows, tn), lambda j: (0, j)),
        compiler_params=_cparams(("arbitrary",)),
        name="ada_mod",
    )(c_all, w_ada, b_ada.reshape(1, n_out))


def _group_norm64(acc, bd, gain):
    sq = (acc * acc).astype(BF16)
    parts = []
    for t in range(acc.shape[1] // LANES):
        ss = _dot(sq[:, t * LANES:(t + 1) * LANES], bd)
        a = acc[:, t * LANES:(t + 1) * LANES]
        parts.append(a * lax.rsqrt(ss * (1.0 / D_QK) + EPS) * gain)
    return jnp.concatenate(parts, axis=1)


def _in_kernel(x_ref, mod_ref, g1_ref, w_ref, qg_ref, kg_ref, bd_ref,
               q_ref, f_ref, i_ref, g_ref, qa_ref, kf_ref, kb_ref, vf_ref, vb_ref,
               h_scr):
    j = pl.program_id(1)
    tm, d = h_scr.shape

    @pl.when(j == 0)
    def _():
        x = x_ref[...]
        ms = jnp.mean(x * x, axis=-1, keepdims=True)
        xn = x * lax.rsqrt(ms + EPS) * g1_ref[...]
        m = mod_ref[...]
        h = xn * (1.0 + m[:, 1:2, :]) + m[:, 0:1, :]
        h_scr[...] = h.reshape(tm, d).astype(BF16)

    def section(k, epilogue, chunked=True):
        @pl.when(j == k)
        def _():
            width = w_ref.shape[1]
            if not chunked:
                epilogue(_dot(h_scr[...], w_ref[...]), slice(0, width))
                return
            for c in range(width // MXU_COLS):
                cols = slice(c * MXU_COLS, (c + 1) * MXU_COLS)
                epilogue(_dot(h_scr[...], w_ref[:, cols]), cols)

    def put(ref, fn):
        def epilogue(acc, cols):
            ref[:, cols] = fn(acc)
        return epilogue

    def put_k(acc, cols):
        kn = _group_norm64(acc, bd_ref[...], kg_ref[...])
        kf_ref[:, cols] = kn
        kb_ref[:, cols] = kn.astype(BF16)

    def put_v(acc, cols):
        vf_ref[:, cols] = acc
        vb_ref[:, cols] = acc.astype(BF16)

    section(0, put(q_ref, lambda a: _silu(a).astype(BF16)))
    section(1, put(f_ref, lambda a: a))
    section(2, put(i_ref, lambda a: a.astype(BF16)))
    section(3, put(g_ref, lambda a: _silu(a).astype(BF16)))
    section(4, put(qa_ref, lambda a: (_group_norm64(a, bd_ref[...], qg_ref[...])
                                      * (LOG2E * D_QK ** -0.5)).astype(BF16)), chunked=False)
    section(5, put_k, chunked=False)
    section(6, put_v)


def _in_proj(x, mod, g1, w_in_b, qg, kg, bd):
    batch, seq, d = x.shape
    n = batch * seq
    bb, t = _row_tiling(batch, seq, ROW_TILE)
    tm = bb * t
    tiles_per_mod = (seq // t) if bb == 1 else 1
    sec = 1024
    n_sec = w_in_b.shape[1] // sec
    xv = x.reshape(n // t, t, d)
    row = lambda i, j: (i, 0)
    out_dt = [BF16, F32, BF16, BF16, BF16, F32, BF16, F32, BF16]
    return pl.pallas_call(
        _in_kernel,
        out_shape=[jax.ShapeDtypeStruct((n, sec), dt) for dt in out_dt],
        grid=(n // tm, n_sec),
        in_specs=[pl.BlockSpec((bb, t, d), lambda i, j: (i, 0, 0)),
                  pl.BlockSpec((bb, 6, d), lambda i, j: (i // tiles_per_mod, 0, 0)),
                  pl.BlockSpec((1, d), lambda i, j: (0, 0)),
                  pl.BlockSpec((d, sec), lambda i, j: (0, j)),
                  pl.BlockSpec((1, LANES), lambda i, j: (0, 0)),
                  pl.BlockSpec((1, LANES), lambda i, j: (0, 0)),
                  pl.BlockSpec((LANES, LANES), lambda i, j: (0, 0))],
        out_specs=[pl.BlockSpec((tm, sec), row) for _ in out_dt],
        scratch_shapes=[pltpu.VMEM((tm, d), BF16)],
        compiler_params=_cparams(("arbitrary", "arbitrary")),
        name="in_proj",
    )(xv, mod, g1, w_in_b, qg, kg, bd)


def _hgrn_consts():
    c = CHUNK
    t = np.arange(c)[:, None]
    s = np.arange(c)[None, :]
    sums = [(s <= t), (s > t)]
    masks = []
    for m in HGRN_LEVELS:
        start = (t // (2 * m)) * (2 * m)
        ref = start + m - 1
        lower = (t % (2 * m)) >= m
        sums.append(np.where(lower, (s > ref) & (s <= t), (s > t) & (s <= ref)))
        s_start = (s // (2 * m)) * (2 * m)
        masks.append((s_start == start) & lower & ((s % (2 * m)) < m))
    masks.append(s == t)
    sum_mat = np.concatenate(sums, axis=0).astype(np.float32)
    sum_cat = np.concatenate([sum_mat, sum_mat], axis=1)
    mask_mat = np.concatenate(masks, axis=0).astype(np.float32)
    return jnp.asarray(sum_cat, BF16), jnp.asarray(mask_mat, F32)


def _hgrn_kernel(q_ref, f_ref, i_ref, g_ref, s0_ref, lb_ref, og_ref, sum_ref, mask_ref,
                 o_ref, sout_ref, st_scr, *, n_steps):
    ci = pl.program_id(1)

    @pl.when(ci == 0)
    def _():
        for h in range(N_HEADS):
            st_scr[h] = s0_ref[0, h].T

    for cc in range(f_ref.shape[0] // CHUNK):
        _hgrn_chunk(slice(cc * CHUNK, (cc + 1) * CHUNK), q_ref, f_ref, i_ref, g_ref, lb_ref, og_ref,
                    sum_ref, mask_ref, o_ref, st_scr)

    @pl.when(ci == n_steps - 1)
    def _():
        for h in range(N_HEADS):
            sout_ref[0, h] = st_scr[h].T


def _hgrn_chunk(rows, q_ref, f_ref, i_ref, g_ref, lb_ref, og_ref, sum_ref, mask_ref, o_ref, st_scr):
    c = CHUNK
    hd = HEAD_DIM
    f = f_ref[rows, :]
    lb = lb_ref[...]
    e = jnp.exp(-jnp.abs(f))
    r = 1.0 / (1.0 + e)
    pos = f >= 0.0
    sig = jnp.where(pos, r, e * r)
    nsig = jnp.where(pos, e * r, r)
    logf = jnp.log(lb + (1.0 - lb) * sig)
    kin = (1.0 - lb) * nsig

    hi = logf.astype(BF16)
    lo = (logf - hi.astype(F32)).astype(BF16)
    expo = _dot(sum_ref[...], jnp.concatenate([hi, lo], axis=0))
    dec = jnp.exp(expo)

    masks = mask_ref[...]
    og = og_ref[...]
    n_lv = len(HGRN_LEVELS)
    for h in range(N_HEADS):
        cs = slice(h * hd, (h + 1) * hd)
        q = q_ref[rows, cs].astype(F32)
        k = kin[:, cs]
        v = i_ref[rows, cs]
        d0 = dec[0:c, cs]
        d1 = dec[c:2 * c, cs]
        st = st_scr[h]
        o = _dot_nt((q * d0).astype(BF16), st.astype(BF16))
        a = _dot_nt(q.astype(BF16), k.astype(BF16)) * masks[n_lv * c:(n_lv + 1) * c]
        for lv in range(n_lv):
            dl = dec[(2 + lv) * c:(3 + lv) * c, cs]
            a = a + _dot_nt((q * dl).astype(BF16), (k * dl).astype(BF16)) * masks[lv * c:(lv + 1) * c]
        o = o + _dot(a.astype(BF16), v)
        st_scr[h] = st * d0[c - 1:c, :] + _dot_tn(v, (k * d1).astype(BF16))
        ms = jnp.mean(o * o, axis=-1, keepdims=True)
        o = o * lax.rsqrt(ms + EPS) * og * g_ref[rows, cs].astype(F32)
        o_ref[rows, cs] = o.astype(BF16)


def _hgrn(q, f, i, g, s0, lb, og, batch, seq):
    n, w = q.shape
    cps = HGRN_CHUNKS_PER_STEP if (seq // CHUNK) % HGRN_CHUNKS_PER_STEP == 0 else 1
    rows = cps * CHUNK
    nc = seq // rows
    sum_cat, mask_mat = _hgrn_consts()
    row = lambda b, c: (b * nc + c, 0)
    const = lambda b, c: (0, 0)
    return pl.pallas_call(
        functools.partial(_hgrn_kernel, n_steps=nc),
        out_shape=[jax.ShapeDtypeStruct((n, w), BF16),
                   jax.ShapeDtypeStruct(s0.shape, F32)],
        grid=(batch, nc),
        in_specs=[pl.BlockSpec((rows, w), row),
                  pl.BlockSpec((rows, w), row),
                  pl.BlockSpec((rows, w), row),
                  pl.BlockSpec((rows, w), row),
                  pl.BlockSpec((1,) + s0.shape[1:], lambda b, c: (b, 0, 0, 0)),
                  pl.BlockSpec((1, w), const),
                  pl.BlockSpec((1, HEAD_DIM), const),
                  pl.BlockSpec(sum_cat.shape, const),
                  pl.BlockSpec(mask_mat.shape, const)],
        out_specs=[pl.BlockSpec((rows, w), row),
                   pl.BlockSpec((1,) + s0.shape[1:], lambda b, c: (b, 0, 0, 0))],
        scratch_shapes=[pltpu.VMEM((N_HEADS, HEAD_DIM, HEAD_DIM), F32)],
        compiler_params=_cparams(("arbitrary", "arbitrary")),
        name="hgrn2",
    )(q, f, i, g, s0, lb, og, sum_cat, mask_mat)


def _stack_maps(qt):
    lane = lax.broadcasted_iota(jnp.int32, qt.shape, 1)
    zero = jnp.zeros_like(qt)
    return jnp.concatenate([jnp.where(lane < D_QK, qt, zero), jnp.where(lane >= D_QK, qt, zero)], axis=0)


def _lane_tile(x, width):
    if width < LANES:
        return x[:, :width]
    return jnp.concatenate([x] * (width // LANES), axis=1)


def _softmax_update(s, v, m_ref, l_ref, acc_ref, rows):
    m_prev = m_ref[rows, :]
    m_next = jnp.maximum(m_prev, jnp.max(s, axis=1, keepdims=True))
    alpha = jnp.exp2(m_prev - m_next)
    p = jnp.exp2(s - _lane_tile(m_next, s.shape[1]))
    l_ref[rows, :] = alpha * l_ref[rows, :] + jnp.sum(p, axis=1, keepdims=True)
    acc_ref[rows, :] = alpha * acc_ref[rows, :] + _dot(p.astype(BF16), v)
    m_ref[rows, :] = m_next


def _diag_bias(slope, r0, rows, tq):
    r = r0 + lax.broadcasted_iota(jnp.int32, (rows, tq), 0)
    c = lax.broadcasted_iota(jnp.int32, (rows, tq), 1)
    allowed = (r // CHUNK) >= (c // CHUNK)
    return allowed, slope * (r - jnp.abs(r - c)).astype(F32)


def _lambda(lq1_ref, lk1_ref, lq2_ref, lk2_ref, lam_init):
    return (jnp.exp(jnp.sum(lq1_ref[...] * lk1_ref[...], axis=1, keepdims=True))
            - jnp.exp(jnp.sum(lq2_ref[...] * lk2_ref[...], axis=1, keepdims=True)) + lam_init)


def _attn_finish(acc_ref, l_ref, row0, tq, lam, sg, lam_init):
    a1 = acc_ref[row0:row0 + tq, :] / l_ref[row0:row0 + tq, :]
    a2 = acc_ref[row0 + tq:row0 + 2 * tq, :] / l_ref[row0 + tq:row0 + 2 * tq, :]
    o = a1 - lam * a2
    ms = jnp.mean(o * o, axis=-1, keepdims=True)
    return (o * lax.rsqrt(ms + EPS) * sg * (1.0 - lam_init)).astype(BF16)


def _attn_prompt_kernel(q_ref, k_ref, v_ref, slope_ref, lq1_ref, lk1_ref, lq2_ref, lk2_ref, sg_ref,
                        o_ref, qs, s_a, s_b, p_a, p_b, vx_a, vx_b, al_a, al_b, m_scr, l_scr, acc_scr,
                        *, tq, lam_init):
    qi = pl.program_id(2)
    slope = slope_ref[0, :, 0:1]
    rb = ATTN_ROWS
    qs[...] = _stack_maps(q_ref[...])
    bufs = ((s_a, p_a, vx_a, al_a), (s_b, p_b, vx_b, al_b))
    for _, _, vx, _ in bufs:
        vx[:, HEAD_DIM:] = jnp.ones((tq, LANES), BF16)
    m_scr[...] = jnp.full(m_scr.shape, NEG_BIG, F32)
    l_scr[...] = jnp.zeros(l_scr.shape, F32)
    acc_scr[...] = jnp.zeros(acc_scr.shape, F32)

    def scores(kj):
        return _dot_nt(qs[...], k_ref[pl.ds(kj * tq, tq), :])

    def update(buf, kj, diag):
        s_ref, p_scr, vx, al_scr = bufs[buf]
        if not diag:
            col = lax.broadcasted_iota(jnp.int32, (1, tq), 1)
            bias = slope * (col + (kj - qi) * tq).astype(F32)
        def biased(r0):
            rows = slice(r0, r0 + rb)
            if diag:
                allowed, bias_d = _diag_bias(slope, r0 % tq, rb, tq)
                return jnp.where(allowed, s_ref[rows, :] + bias_d, NEG_BIG)
            return s_ref[rows, :] + bias

        for r0 in range(0, 2 * tq, rb):
            rows = slice(r0, r0 + rb)
            m_prev = m_scr[rows, :]
            m_next = jnp.maximum(m_prev, jnp.max(biased(r0), axis=1, keepdims=True))
            al_scr[rows, :] = jnp.exp2(m_prev - m_next)
            m_scr[rows, :] = m_next
        for r0 in range(0, 2 * tq, rb):
            rows = slice(r0, r0 + rb)
            p_scr[rows, :] = jnp.exp2(biased(r0) - _lane_tile(m_scr[rows, :], tq)).astype(BF16)
        vx[:, 0:HEAD_DIM] = v_ref[pl.ds(kj * tq, tq), :]
        pv = _dot(p_scr[...], vx[...])
        alpha = al_scr[...]
        acc_scr[...] = alpha * acc_scr[...] + pv[:, 0:HEAD_DIM]
        l_scr[...] = alpha * l_scr[...] + pv[:, HEAD_DIM:]

    s_a[...] = scores(0)

    def pair_step(pj, carry):
        kj = 2 * pj
        s_b[...] = scores(kj + 1)
        update(0, kj, False)
        s_a[...] = scores(kj + 2)
        update(1, kj + 1, False)
        return carry

    lax.fori_loop(0, lax.shift_right_logical(qi, 1), pair_step, 0)
    odd = (qi & 1) == 1

    @pl.when(jnp.logical_not(odd))
    def _():
        update(0, qi, True)

    @pl.when(odd)
    def _():
        s_b[...] = scores(qi)
        update(0, qi - 1, False)
        update(1, qi, True)

    lam = _lambda(lq1_ref, lk1_ref, lq2_ref, lk2_ref, lam_init)
    o_ref[...] = _attn_finish(acc_scr, l_scr, 0, tq, lam, sg_ref[...], lam_init)


def _attn_cached_kernel(q_ref, kn_ref, vn_ref, pk_ref, pv_ref, slope_ref,
                        lq1_ref, lk1_ref, lq2_ref, lk2_ref, sg_ref,
                        o_ref, qs, m_scr, l_scr, acc_scr, *, tq, tk, p_len, lam_init):
    hd = HEAD_DIM
    for h in range(N_HEADS):
        qs[2 * tq * h:2 * tq * (h + 1), :] = _stack_maps(q_ref[:, h * hd:(h + 1) * hd])
    m_scr[...] = jnp.full(m_scr.shape, NEG_BIG, F32)
    l_scr[...] = jnp.zeros(l_scr.shape, F32)
    acc_scr[...] = jnp.zeros(acc_scr.shape, F32)
    col = lax.broadcasted_iota(jnp.int32, (1, tk), 1)

    def past_step(kj, carry):
        pos = (col + (kj * tk - p_len)).astype(F32)
        for h in range(N_HEADS):
            rows = slice(2 * tq * h, 2 * tq * (h + 1))
            k = pk_ref[0, pl.ds(kj * (tk * N_HEADS) + h, tk, stride=N_HEADS), :].astype(BF16)
            v = pv_ref[0, pl.ds(kj * (tk * N_HEADS) + h, tk, stride=N_HEADS), :].astype(BF16)
            s = _dot_nt(qs[rows, :], k) + slope_ref[h, :, 0:1] * pos
            _softmax_update(s, v, m_scr, l_scr, acc_scr, rows)
        return carry

    lax.fori_loop(0, p_len // tk, past_step, 0)

    lam = _lambda(lq1_ref, lk1_ref, lq2_ref, lk2_ref, lam_init)
    for h in range(N_HEADS):
        kd = kn_ref[:, h * hd:(h + 1) * hd]
        vd = vn_ref[:, h * hd:(h + 1) * hd]
        allowed, bias = _diag_bias(slope_ref[h, :, 0:1], 0, tq, tq)
        for half in range(2):
            rows = slice(2 * tq * h + half * tq, 2 * tq * h + (half + 1) * tq)
            s = jnp.where(allowed, _dot_nt(qs[rows, :], kd) + bias, NEG_BIG)
            _softmax_update(s, vd, m_scr, l_scr, acc_scr, rows)
        o_ref[:, h * hd:(h + 1) * hd] = _attn_finish(acc_scr, l_scr, 2 * tq * h, tq, lam, sg_ref[...],
                                                     lam_init)


def _alibi_slopes():
    s = (2.0 ** (-8.0 * np.arange(1, N_HEADS + 1) / N_HEADS)) * LOG2E
    return jnp.asarray(np.broadcast_to(s[:, None, None], (N_HEADS, 1, LANES)), F32)


def _attention_prompt(qa, kb, vb, batch, seq, lam_params, subln_g, lam_init):
    n, w = qa.shape
    tq = min(ATTN_TILE, seq)
    nq = seq // tq
    assert seq % tq == 0 and tq % CHUNK == 0
    small = lambda b, h, qi: (0, 0)
    q_spec = pl.BlockSpec((tq, HEAD_DIM), lambda b, h, qi: (b * nq + qi, h))
    kv_spec = pl.BlockSpec((seq, HEAD_DIM), lambda b, h, qi: (b, h))
    return pl.pallas_call(
        functools.partial(_attn_prompt_kernel, tq=tq, lam_init=lam_init),
        out_shape=jax.ShapeDtypeStruct((n, w), BF16),
        grid=(batch, N_HEADS, nq),
        in_specs=[q_spec, kv_spec, kv_spec,
                  pl.BlockSpec((1, 1, LANES), lambda b, h, qi: (h, 0, 0)),
                  pl.BlockSpec((1, D_QK), small), pl.BlockSpec((1, D_QK), small),
                  pl.BlockSpec((1, D_QK), small), pl.BlockSpec((1, D_QK), small),
                  pl.BlockSpec((1, HEAD_DIM), small)],
        out_specs=q_spec,
        scratch_shapes=[pltpu.VMEM((2 * tq, HEAD_DIM), BF16),
                        pltpu.VMEM((2 * tq, tq), F32), pltpu.VMEM((2 * tq, tq), F32),
                        pltpu.VMEM((2 * tq, tq), BF16), pltpu.VMEM((2 * tq, tq), BF16),
                        pltpu.VMEM((tq, HEAD_DIM + LANES), BF16), pltpu.VMEM((tq, HEAD_DIM + LANES), BF16),
                        pltpu.VMEM((2 * tq, LANES), F32), pltpu.VMEM((2 * tq, LANES), F32),
                        pltpu.VMEM((2 * tq, LANES), F32), pltpu.VMEM((2 * tq, LANES), F32),
                        pltpu.VMEM((2 * tq, HEAD_DIM), F32)],
        compiler_params=_cparams(("arbitrary", "arbitrary", "arbitrary")),
        name="diff_attn",
    )(qa, kb, vb, _alibi_slopes(), *lam_params, subln_g)


def _attention_cached(qa, kb, vb, batch, seq, past_k, past_v, lam_params, subln_g, lam_init):
    n, w = qa.shape
    p_len = past_k.shape[1]
    tk = min(SAMPLE_KEY_TILE, p_len)
    assert p_len % tk == 0 and p_len % CHUNK == 0 and seq == CHUNK
    rows = 2 * seq * N_HEADS
    small = lambda b: (0, 0)
    tok_spec = pl.BlockSpec((seq, w), lambda b: (b, 0))
    past_spec = pl.BlockSpec((1, p_len * N_HEADS, HEAD_DIM), lambda b: (b, 0, 0))
    return pl.pallas_call(
        functools.partial(_attn_cached_kernel, tq=seq, tk=tk, p_len=p_len, lam_init=lam_init),
        out_shape=jax.ShapeDtypeStruct((n, w), BF16),
        grid=(batch,),
        in_specs=[tok_spec, tok_spec, tok_spec, past_spec, past_spec,
                  pl.BlockSpec((N_HEADS, 1, LANES), lambda b: (0, 0, 0)),
                  pl.BlockSpec((1, D_QK), small), pl.BlockSpec((1, D_QK), small),
                  pl.BlockSpec((1, D_QK), small), pl.BlockSpec((1, D_QK), small),
                  pl.BlockSpec((1, HEAD_DIM), small)],
        out_specs=tok_spec,
        scratch_shapes=[pltpu.VMEM((rows, HEAD_DIM), BF16), pltpu.VMEM((rows, LANES), F32),
                        pltpu.VMEM((rows, LANES), F32), pltpu.VMEM((rows, HEAD_DIM), F32)],
        compiler_params=_cparams(("arbitrary",)),
        name="diff_attn_cached",
    )(qa, kb, vb, past_k.reshape(batch, p_len * N_HEADS, HEAD_DIM),
      past_v.reshape(batch, p_len * N_HEADS, HEAD_DIM), _alibi_slopes(), *lam_params, subln_g)


def _split_bf16(x):
    hi = x.astype(BF16)
    lo = (x - hi.astype(F32)).astype(BF16)
    return hi, lo


def _out_kernel(orp_ref, oap_ref, xp_ref, modp_ref, ors_ref, oas_ref, xs_ref, mods_ref,
                g2_ref, w_ref, rw_ref, rb_ref, tri_ref,
                x1_ref, h2_ref, route_ref, cnt_ref, cnt_scr, *, n_tiles_p):
    i = pl.program_id(0)

    @pl.when(i == 0)
    def _():
        cnt_scr[...] = jnp.zeros_like(cnt_scr)

    args = (g2_ref, w_ref, rw_ref, rb_ref, tri_ref, x1_ref, h2_ref, route_ref, cnt_scr)

    @pl.when(i < n_tiles_p)
    def _():
        _out_tile(orp_ref, oap_ref, xp_ref, modp_ref, *args)

    @pl.when(i >= n_tiles_p)
    def _():
        _out_tile(ors_ref, oas_ref, xs_ref, mods_ref, *args)

    cnt_ref[...] = jnp.broadcast_to(cnt_scr[...], cnt_ref.shape)


def _out_tile(or_ref, oa_ref, x_ref, mod_ref, g2_ref, w_ref, rw_ref, rb_ref, tri_ref,
              x1_ref, h2_ref, route_ref, cnt_scr):
    bb, t, d = x_ref.shape
    tm = bb * t
    half = or_ref.shape[1]
    mix = _dot(or_ref[...], w_ref[0:half, :]) + _dot(oa_ref[...], w_ref[half:, :])
    m = mod_ref[...]
    x1 = x_ref[...] + m[:, 2:3, :] * mix.reshape(bb, t, d)
    ms = jnp.mean(x1 * x1, axis=-1, keepdims=True)
    h2 = x1 * lax.rsqrt(ms + EPS) * g2_ref[...] * (1.0 + m[:, 4:5, :]) + m[:, 3:4, :]
    x1_ref[...] = x1.reshape(tm, d)
    h2 = h2.reshape(tm, d)
    slabs = d // LANES
    for j in range(slabs):
        h2_ref[pl.ds(j, tm, stride=slabs), :] = h2[:, j * LANES:(j + 1) * LANES]

    hh, hl = _split_bf16(h2)
    wh, wl = _split_bf16(rw_ref[...])
    logits = _dot(hh, wh) + _dot(hl, wh) + _dot(hh, wl) + rb_ref[...]
    lane = lax.broadcasted_iota(jnp.int32, logits.shape, 1)
    neg = jnp.float32(-jnp.inf)
    big = jnp.int32(1 << 20)

    is_g = (lane >= N_EXPERTS) & (lane < N_EXPERTS + N_GROUPS)
    gl = jnp.where(is_g, logits, neg)
    gmax = jnp.max(gl, axis=1, keepdims=True)
    gidx = jnp.min(jnp.where(gl == gmax, lane - N_EXPERTS, big), axis=1, keepdims=True)
    g_prob = 1.0 / jnp.sum(jnp.exp(gl - gmax), axis=1, keepdims=True)

    in_grp = (lane < N_EXPERTS) & ((lane // EXPERTS_PER_GROUP) == gidx)
    el = jnp.where(in_grp, logits, neg)
    m1 = jnp.max(el, axis=1, keepdims=True)
    i1 = jnp.min(jnp.where(el == m1, lane, big), axis=1, keepdims=True)
    el2 = jnp.where(lane == i1, neg, el)
    m2 = jnp.max(el2, axis=1, keepdims=True)
    i2 = jnp.min(jnp.where(el2 == m2, lane, big), axis=1, keepdims=True)
    e21 = jnp.exp(m2 - m1)
    w1 = g_prob / (1.0 + e21)
    w2 = g_prob * e21 / (1.0 + e21)
    tri = tri_ref[...]
    oh1 = jnp.where(lane == i1, 1.0, 0.0)
    oh2 = jnp.where(lane == i2, 1.0, 0.0)
    cnt = cnt_scr[...]
    c1 = jnp.sum(oh1, axis=0, keepdims=True)
    c2 = jnp.sum(oh2, axis=0, keepdims=True)
    r1 = jnp.sum(oh1 * (_dot(tri, oh1.astype(BF16)) + cnt), axis=1, keepdims=True)
    r2 = jnp.sum(oh2 * (_dot(tri, oh2.astype(BF16)) + cnt + c1), axis=1, keepdims=True)
    cnt_scr[...] = cnt + c1 + c2

    vals = (i1.astype(F32), i2.astype(F32), w1, w2, r1, r2)
    route = jnp.zeros(logits.shape, F32)
    for k, v in enumerate(vals):
        route = jnp.where(lane == k, v, route)
    route_ref[...] = route


def _out_proj(o_rp, o_ap, xp, mod_p, o_rs, o_as, xs, mod_s, g2, w_out_b, rw, rb):
    d = xp.shape[2]
    half = o_rp.shape[1]
    n_p = xp.shape[0] * xp.shape[1]
    n_s = xs.shape[0] * xs.shape[1]
    n = n_p + n_s
    bbp, tp = _row_tiling(xp.shape[0], xp.shape[1], ROW_TILE)
    bbs, ts = _row_tiling(xs.shape[0], xs.shape[1], ROW_TILE)
    tm = bbp * tp
    assert bbs * ts == tm and n_p % tm == 0 and n_s % tm == 0
    ntp = n_p // tm
    nts = n_s // tm
    modp_tiles = (xp.shape[1] // tp) if bbp == 1 else 1
    mods_tiles = (xs.shape[1] // ts) if bbs == 1 else 1
    pi = lambda i: jnp.minimum(i, ntp - 1)
    si = lambda i: jnp.maximum(i - ntp, 0)
    tri = jnp.asarray(np.tril(np.ones((tm, tm), np.float32), -1), BF16)
    row = lambda i: (i, 0)
    const = lambda i: (0, 0)
    return pl.pallas_call(
        functools.partial(_out_kernel, n_tiles_p=ntp),
        out_shape=[jax.ShapeDtypeStruct((n, d), F32), jax.ShapeDtypeStruct((n * (d // LANES), LANES), F32),
                   jax.ShapeDtypeStruct((n, LANES), F32), jax.ShapeDtypeStruct((8, LANES), F32)],
        grid=(ntp + nts,),
        in_specs=[pl.BlockSpec((tm, half), lambda i: (pi(i), 0)),
                  pl.BlockSpec((tm, half), lambda i: (pi(i), 0)),
                  pl.BlockSpec((bbp, tp, d), lambda i: (pi(i), 0, 0)),
                  pl.BlockSpec((bbp, 6, d), lambda i: (pi(i) // modp_tiles, 0, 0)),
                  pl.BlockSpec((tm, half), lambda i: (si(i), 0)),
                  pl.BlockSpec((tm, half), lambda i: (si(i), 0)),
                  pl.BlockSpec((bbs, ts, d), lambda i: (si(i), 0, 0)),
                  pl.BlockSpec((bbs, 6, d), lambda i: (si(i) // mods_tiles, 0, 0)),
                  pl.BlockSpec((1, d), const),
                  pl.BlockSpec(w_out_b.shape, const),
                  pl.BlockSpec(rw.shape, const),
                  pl.BlockSpec((1, LANES), const),
                  pl.BlockSpec((tm, tm), const)],
        out_specs=[pl.BlockSpec((tm, d), row), pl.BlockSpec((tm * (d // LANES), LANES), row),
                   pl.BlockSpec((tm, LANES), row), pl.BlockSpec((8, LANES), const)],
        scratch_shapes=[pltpu.VMEM((1, LANES), F32)],
        compiler_params=_cparams(("arbitrary",)),
        name="out_proj_router",
    )(o_rp, o_ap, xp.reshape(n_p // tp, tp, d), mod_p,
      o_rs, o_as, xs.reshape(n_s // ts, ts, d), mod_s, g2, w_out_b, rw, rb, tri)


ROW_PACK_SHIFT = 16
ROW_PITCH = 24
MOE_GATHER_BUFS = 3


def _moe_kernel(pack_ref, be_ref, nu_ref,
                h2_hbm, wg_ref, wu_ref, wd_ref, out_hbm,
                xbuf, obuf, xb, hb, wgb, wub, wdb, gsem, ssem, *, n_blk):
    i = pl.program_id(0)
    n_used = nu_ref[0]
    bm, d = xb.shape
    slabs = d // LANES
    de = wgb.shape[1]
    n_up = de // MXU_COLS
    n_down = d // MXU_COLS
    n_gbuf = xbuf.shape[0]
    gslot = lax.rem(i, n_gbuf)
    gnext = lax.rem(i + n_gbuf - 1, n_gbuf)
    slot = lax.rem(i, 2)
    other = 1 - slot
    dst_mask = (1 << ROW_PACK_SHIFT) - 1

    def gather_copy(tok, r, s):
        src = pl.multiple_of(tok * slabs, slabs)
        return pltpu.make_async_copy(h2_hbm.at[pl.ds(src, slabs)],
                                     xbuf.at[s, pl.ds(r * ROW_PITCH, slabs)], gsem.at[s])

    def scatter_copy(dst, r, s):
        dst = pl.multiple_of(dst * slabs, slabs)
        return pltpu.make_async_copy(obuf.at[s, pl.ds(r * ROW_PITCH, slabs)],
                                     out_hbm.at[pl.ds(dst, slabs)], ssem.at[s])

    def start_gather(blk, s, rows):
        for r in rows:
            tok = lax.shift_right_logical(pack_ref[blk * bm + r], ROW_PACK_SHIFT)
            gather_copy(tok, r, s).start(priority=r % 2)

    def start_scatter(blk, s, rows):
        for r in rows:
            scatter_copy(pack_ref[blk * bm + r] & dst_mask, r, s).start(priority=r % 2)

    def wait_gather(s):
        for r in range(bm):
            gather_copy(0, r, s).wait()

    def wait_scatter(s):
        for r in range(bm):
            scatter_copy(0, r, s).wait()

    @pl.when(i < n_used)
    def _():
        @pl.when(i == 0)
        def _():
            for b in range(n_gbuf - 1):
                start_gather(min(b, n_blk - 1), b, range(bm))
            obuf[...] = jnp.zeros_like(obuf)
            spare0 = out_hbm.shape[0] - 2 * bm * slabs
            fills = [pltpu.make_async_copy(obuf.at[s, pl.ds(0, bm * slabs)],
                                           out_hbm.at[pl.ds(spare0 + s * bm * slabs, bm * slabs)],
                                           ssem.at[s]) for s in range(2)]
            for c in fills:
                c.start()
            for c in fills:
                c.wait()

        new_expert = jnp.logical_or(i == 0, be_ref[i] != be_ref[jnp.maximum(i - 1, 0)])

        @pl.when(new_expert)
        def _():
            wgb[...] = wg_ref[0].astype(BF16)
            wub[...] = wu_ref[0].astype(BF16)
            wdb[...] = wd_ref[0].astype(BF16)

        wait_gather(gslot)
        for j in range(slabs):
            xb[:, j * LANES:(j + 1) * LANES] = xbuf[gslot, pl.ds(j, bm, stride=ROW_PITCH), :].astype(BF16)
        nxt = jnp.minimum(i + n_gbuf - 1, n_blk - 1)
        prv = jnp.where(i == 0, n_blk, i - 1)

        def copy_batch(k, n_batches, base):
            per = (bm // 2) // n_batches
            rows = range(base + k * per, base + (k + 1) * per)
            start_gather(nxt, gnext, rows)
            start_scatter(prv, other, rows)

        for c in range(n_up):
            cols = slice(c * MXU_COLS, (c + 1) * MXU_COLS)
            hb[:, cols] = (_silu(_dot(xb[...], wgb[:, cols])) * _dot(xb[...], wub[:, cols])).astype(BF16)
            copy_batch(c, n_up, 0)

        @pl.when(i >= 1)
        def _():
            wait_scatter(slot)

        for c in range(n_down):
            y = _dot(hb[...], wdb[:, c * MXU_COLS:(c + 1) * MXU_COLS])
            for jj in range(MXU_COLS // LANES):
                j = c * (MXU_COLS // LANES) + jj
                obuf[slot, pl.ds(j, bm, stride=ROW_PITCH), :] = y[:, jj * LANES:(jj + 1) * LANES]
            copy_batch(c, n_down, bm // 2)

        @pl.when(i == n_used - 1)
        def _():
            for b in range(1, n_gbuf):
                wait_gather(lax.rem(i + b, n_gbuf))
            start_scatter(i, slot, range(bm))
            wait_scatter(other)
            wait_scatter(slot)


def _moe(h2, route, counts, wg, wu, wd):
    n = route.shape[0]
    d = wg.shape[1]
    slabs = d // LANES
    assert h2.shape == (n * slabs, LANES)
    assert TOP_K == 2
    s_rows = n * TOP_K
    bm = MOE_ROWS
    n_blk = -(-s_rows // bm) + N_EXPERTS
    n_pad = n_blk * bm
    n_out = s_rows + 2 * bm
    assert n < (1 << (31 - ROW_PACK_SHIFT)) and n_out <= (1 << ROW_PACK_SHIFT)

    counts = counts[0, :N_EXPERTS].astype(jnp.int32)
    padded = (counts + bm - 1) // bm * bm
    pad_end = jnp.cumsum(padded)
    pad_start = pad_end - padded
    expert = route[:, 0:TOP_K].astype(jnp.int32)
    rank = route[:, 2 * TOP_K:3 * TOP_K].astype(jnp.int32)
    onehot = expert[:, :, None] == jnp.arange(N_EXPERTS, dtype=jnp.int32)[None, None, :]
    dest = jnp.sum(jnp.where(onehot, pad_start[None, None, :], 0), axis=2) + rank
    tok = jnp.arange(n, dtype=jnp.int32)
    valid = (tok[:, None] << ROW_PACK_SHIFT) | (jnp.arange(TOP_K, dtype=jnp.int32)[None, :] * n
                                                + tok[:, None])
    pos = jnp.arange(n_pad + bm, dtype=jnp.int32)
    blk_par = jnp.where(pos < n_pad, (pos // bm) % 2, 1)
    spare = s_rows + blk_par * bm + pos % bm
    row_pack = spare.at[dest.reshape(-1)].set(valid.reshape(-1))
    blk_start = jnp.arange(n_blk, dtype=jnp.int32) * bm
    blk_e = jnp.minimum(jnp.sum((pad_end[None, :] <= blk_start[:, None]).astype(jnp.int32), axis=1),
                        N_EXPERTS - 1)
    n_used = (pad_end[-1] // bm).astype(jnp.int32).reshape(1)

    de = wg.shape[2]
    assert de % MXU_COLS == 0 and d % MXU_COLS == 0 and slabs <= ROW_PITCH
    assert (bm // 2) % (de // MXU_COLS) == 0 and (bm // 2) % (d // MXU_COLS) == 0
    wmap = lambda i, pack, be, nu: (be[i], 0, 0)
    grid_spec = pltpu.PrefetchScalarGridSpec(
        num_scalar_prefetch=3,
        grid=(n_blk,),
        in_specs=[pl.BlockSpec(memory_space=pl.ANY),
                  pl.BlockSpec((1, d, de), wmap),
                  pl.BlockSpec((1, d, de), wmap),
                  pl.BlockSpec((1, de, d), wmap)],
        out_specs=pl.BlockSpec(memory_space=pl.ANY),
        scratch_shapes=[pltpu.VMEM((MOE_GATHER_BUFS, bm * ROW_PITCH, LANES), F32),
                        pltpu.VMEM((2, bm * ROW_PITCH, LANES), F32),
                        pltpu.VMEM((bm, d), BF16), pltpu.VMEM((bm, de), BF16),
                        pltpu.VMEM((d, de), BF16), pltpu.VMEM((d, de), BF16),
                        pltpu.VMEM((de, d), BF16),
                        pltpu.SemaphoreType.DMA((MOE_GATHER_BUFS,)), pltpu.SemaphoreType.DMA((2,))])
    return pl.pallas_call(
        functools.partial(_moe_kernel, n_blk=n_blk),
        out_shape=jax.ShapeDtypeStruct((n_out * slabs, LANES), F32),
        grid_spec=grid_spec,
        compiler_params=_cparams(("arbitrary",)),
        name="moe_experts",
    )(row_pack, blk_e, n_used, h2, wg, wu, wd)


def _final_kernel(x1_ref, r0_ref, r1_ref, route_ref, mod_ref, y_ref):
    bb, t, d = x1_ref.shape
    tm = bb * t
    slabs = d // LANES
    w = route_ref[...]
    w0 = w[:, :, TOP_K:TOP_K + 1]
    w1 = w[:, :, TOP_K + 1:TOP_K + 2]
    g_f = mod_ref[...][:, 5:6, :]
    for j in range(slabs):
        cols = slice(j * LANES, (j + 1) * LANES)
        r0 = r0_ref[pl.ds(j, tm, stride=slabs), :].reshape(bb, t, LANES)
        r1 = r1_ref[pl.ds(j, tm, stride=slabs), :].reshape(bb, t, LANES)
        y_ref[:, :, cols] = x1_ref[:, :, cols] + g_f[:, :, cols] * (w0 * r0 + w1 * r1)


def _final(x1, rows, route, mod, batch, seq, row_off):
    n_all, d = x1.shape
    slabs = d // LANES
    n = batch * seq
    bb, t = _row_tiling(batch, seq, ROW_TILE)
    tm = bb * t
    tiles_per_mod = (seq // t) if bb == 1 else 1
    assert row_off % tm == 0 and n_all % tm == 0 and rows.shape[0] % (tm * slabs) == 0
    off = row_off // tm
    slot1 = n_all // tm
    return pl.pallas_call(
        _final_kernel,
        out_shape=jax.ShapeDtypeStruct((n // t, t, d), F32),
        grid=(n // tm,),
        in_specs=[pl.BlockSpec((bb, t, d), lambda i: (i + off, 0, 0)),
                  pl.BlockSpec((tm * slabs, LANES), lambda i: (i + off, 0)),
                  pl.BlockSpec((tm * slabs, LANES), lambda i: (i + off + slot1, 0)),
                  pl.BlockSpec((bb, t, LANES), lambda i: (i + off, 0, 0)),
                  pl.BlockSpec((bb, 6, d), lambda i: (i // tiles_per_mod, 0, 0))],
        out_specs=pl.BlockSpec((bb, t, d), lambda i: (i, 0, 0)),
        compiler_params=_cparams(("arbitrary",)),
        name="final_residual",
    )(x1.reshape(-1, t, d), rows, rows, route.reshape(-1, t, LANES), mod).reshape(batch, seq, d)


def _mixer_group(x, mod, past_k, past_v, s0, lyr):
    batch, seq, d = x.shape
    q, f, i, g, qa, kf, kb, vf, vb = _in_proj(x, mod, lyr["g1"], lyr["w_in"], lyr["qg"], lyr["kg"],
                                               lyr["bd"])
    o_r, s_new = _hgrn(q, f, i, g, s0, lyr["lb"], lyr["og"], batch, seq)
    if past_k is None:
        o_a = _attention_prompt(qa, kb, vb, batch, seq, lyr["lam_params"], lyr["sg"], lyr["lam_init"])
    else:
        o_a = _attention_cached(qa, kb, vb, batch, seq, past_k, past_v, lyr["lam_params"], lyr["sg"],
                                lyr["lam_init"])
    k_new = kf.reshape(batch, seq, N_HEADS, HEAD_DIM)
    v_new = vf.reshape(batch, seq, N_HEADS, HEAD_DIM)
    return o_r, o_a, k_new, v_new, s_new


def kernel(x_prompt, x_sample, cache_k_attn, cache_v_attn, state_hgrn, c_prompt, c_sample, w_ada, b_ada, norm1_g, norm2_g, w_in, w_out, hgrn_lb_param, hgrn_onorm_g, attn_qnorm_g, attn_knorm_g, lambda_q1, lambda_k1, lambda_q2, lambda_k2, attn_subln_g, router_group_w, router_group_b, router_expert_w, router_expert_b, expert_w_gate, expert_w_up, expert_w_down):
    depth = w_ada.shape[0]
    bp, tp, d = x_prompt.shape
    bs, ts, _ = x_sample.shape
    d_h = N_HEADS * HEAD_DIM
    lb_all = jnp.cumsum(jax.nn.softmax(hgrn_lb_param.astype(F32), axis=0), axis=0)
    bd = jnp.asarray(np.kron(np.eye(LANES // D_QK), np.ones((D_QK, D_QK))), BF16)

    xp, xs = x_prompt, x_sample
    kp, vp, sp, ksm, vsm, ssm = [], [], [], [], [], []
    for l in range(depth):
        wl = w_in[l]
        qk = wl[:, 4 * d_h:4 * d_h + 4 * N_HEADS * D_QK].reshape(d, 2, 2, N_HEADS, D_QK)
        qk = qk.transpose(0, 1, 3, 2, 4).reshape(d, 4 * N_HEADS * D_QK)
        w_in_b = jnp.concatenate([wl[:, :4 * d_h], qk, wl[:, 4 * d_h + 4 * N_HEADS * D_QK:]],
                                 axis=1).astype(BF16)
        rw = jnp.zeros((d, LANES), F32)
        rw = rw.at[:, :N_EXPERTS].set(router_expert_w[l])
        rw = rw.at[:, N_EXPERTS:N_EXPERTS + N_GROUPS].set(router_group_w[l])
        rb = jnp.zeros((1, LANES), F32)
        rb = rb.at[0, :N_EXPERTS].set(router_expert_b[l])
        rb = rb.at[0, N_EXPERTS:N_EXPERTS + N_GROUPS].set(router_group_b[l])
        lyr = dict(
            g1=norm1_g[l].reshape(1, d), g2=norm2_g[l].reshape(1, d),
            w_in=w_in_b, w_out=w_out[l].astype(BF16),
            qg=jnp.tile(attn_qnorm_g[l], 2).reshape(1, LANES),
            kg=jnp.tile(attn_knorm_g[l], 2).reshape(1, LANES),
            bd=bd, lb=lb_all[l].reshape(1, d_h), og=hgrn_onorm_g[l].reshape(1, HEAD_DIM),
            sg=attn_subln_g[l].reshape(1, HEAD_DIM),
            lam_params=[p[l].reshape(1, D_QK) for p in (lambda_q1, lambda_k1, lambda_q2, lambda_k2)],
            lam_init=0.8 - 0.6 * math.exp(-0.3 * l),
            rw=rw, rb=rb)

        c_all = jnp.concatenate([c_prompt, c_sample], axis=0)
        rows = c_all.shape[0]
        rows_pad = -(-rows // 8) * 8
        mod = _ada(jnp.pad(c_all, ((0, rows_pad - rows), (0, 0))), w_ada[l], b_ada[l])
        mod_p = mod[:bp].reshape(bp, 6, d)
        mod_s = mod[bp:bp + bs].reshape(bs, 6, d)

        s0_p = jnp.zeros((bp, N_HEADS, HEAD_DIM, HEAD_DIM), F32)
        o_rp, o_ap, k_n, v_n, s_n = _mixer_group(xp, mod_p, None, None, s0_p, lyr)
        kp.append(k_n)
        vp.append(v_n)
        sp.append(s_n)
        o_rs, o_as, k_n, v_n, s_n = _mixer_group(xs, mod_s, cache_k_attn[l], cache_v_attn[l],
                                                 state_hgrn[l], lyr)
        ksm.append(k_n)
        vsm.append(v_n)
        ssm.append(s_n)

        x1, h2, route, counts = _out_proj(o_rp, o_ap, xp, mod_p, o_rs, o_as, xs, mod_s,
                                          lyr["g2"], lyr["w_out"], lyr["rw"], lyr["rb"])
        rows = _moe(h2, route, counts, expert_w_gate[l], expert_w_up[l], expert_w_down[l])
        xp = _final(x1, rows, route, mod_p, bp, tp, 0)
        xs = _final(x1, rows, route, mod_s, bs, ts, bp * tp)
    return (xp, xs, jnp.stack(kp), jnp.stack(vp), jnp.stack(sp),
            jnp.stack(ksm), jnp.stack(vsm), jnp.stack(ssm))
```

```python
import functools
import math

import numpy as np
import jax
import jax.numpy as jnp
from jax import lax
from jax.experimental import pallas as pl
from jax.experimental.pallas import tpu as pltpu

F32 = jnp.float32
BF16 = jnp.bfloat16

N_HEADS = 8
HEAD_DIM = 128
D_QK = 64
CHUNK = 64
N_GROUPS = 4
EXPERTS_PER_GROUP = 8
N_EXPERTS = N_GROUPS * EXPERTS_PER_GROUP
TOP_K = 2
EPS = 1e-6
NEG_BIG = -1e30
LOG2E = math.log2(math.e)

LANES = 128
MXU_COLS = 256
ROW_TILE = 512
ATTN_TILE = 512
ATTN_ROWS = 64
SAMPLE_KEY_TILE = 512
MOE_ROWS = 256
FINAL_TILE = 256
VMEM_LIMIT = 56 * 1024 * 1024

HGRN_LEVELS = (32, 16, 8, 4, 2, 1)
HGRN_CHUNKS_PER_STEP = 4


def _cparams(sem, vmem=VMEM_LIMIT):
    return pltpu.CompilerParams(dimension_semantics=sem, vmem_limit_bytes=vmem)


def _dot(a, b):
    return jnp.dot(a, b, preferred_element_type=F32)


def _dot_nt(a, b):
    return lax.dot_general(a, b, (((1,), (1,)), ((), ())), preferred_element_type=F32)


def _dot_tn(a, b):
    return lax.dot_general(a, b, (((0,), (0,)), ((), ())), preferred_element_type=F32)


def _silu(x):
    return x / (1.0 + jnp.exp(-x))


def _row_tiling(batch, seq, target):
    if seq >= target:
        assert seq % target == 0
        return 1, target
    bb = max(1, min(batch, target // seq))
    while batch % bb:
        bb -= 1
    return bb, seq


def _ada_kernel(c_ref, w_ref, b_ref, o_ref):
    s = _silu(c_ref[...]).astype(BF16)
    o_ref[...] = _dot(s, w_ref[...].astype(BF16)) + b_ref[...]


def _ada(c_all, w_ada, b_ada):
    rows, d = c_all.shape
    n_out = w_ada.shape[1]
    tn = 1024
    return pl.pallas_call(
        _ada_kernel,
        out_shape=jax.ShapeDtypeStruct((rows, n_out), F32),
        grid=(n_out // tn,),
        in_specs=[pl.BlockSpec((rows, d), lambda j: (0, 0)),
                  pl.BlockSpec((d, tn), lambda j: (0, j)),
                  pl.BlockSpec((1, tn), lambda j: (0, j))],
        out_specs=pl.BlockSpec((rows, tn), lambda j: (0, j)),
        compiler_params=_cparams(("arbitrary",)),
        name="ada_mod",
    )(c_all, w_ada, b_ada.reshape(1, n_out))


def _group_norm64(acc, bd, gain):
    sq = (acc * acc).astype(BF16)
    parts = []
    for t in range(acc.shape[1] // LANES):
        ss = _dot(sq[:, t * LANES:(t + 1) * LANES], bd)
        a = acc[:, t * LANES:(t + 1) * LANES]
        parts.append(a * lax.rsqrt(ss * (1.0 / D_QK) + EPS) * gain)
    return jnp.concatenate(parts, axis=1)


def _in_kernel(x_ref, mod_ref, g1_ref, w_ref, qg_ref, kg_ref, bd_ref,
               q_ref, f_ref, i_ref, g_ref, qa_ref, kf_ref, kb_ref, vf_ref, vb_ref,
               h_scr):
    j = pl.program_id(1)
    tm, d = h_scr.shape

    @pl.when(j == 0)
    def _():
        x = x_ref[...]
        ms = jnp.mean(x * x, axis=-1, keepdims=True)
        xn = x * lax.rsqrt(ms + EPS) * g1_ref[...]
        m = mod_ref[...]
        h = xn * (1.0 + m[:, 1:2, :]) + m[:, 0:1, :]
        h_scr[...] = h.reshape(tm, d).astype(BF16)

    def section(k, epilogue, chunked=True):
        @pl.when(j == k)
        def _():
            width = w_ref.shape[1]
            if not chunked:
                epilogue(_dot(h_scr[...], w_ref[...]), slice(0, width))
                return
            for c in range(width // MXU_COLS):
                cols = slice(c * MXU_COLS, (c + 1) * MXU_COLS)
                epilogue(_dot(h_scr[...], w_ref[:, cols]), cols)

    def put(ref, fn):
        def epilogue(acc, cols):
            ref[:, cols] = fn(acc)
        return epilogue

    def put_k(acc, cols):
        kn = _group_norm64(acc, bd_ref[...], kg_ref[...])
        kf_ref[:, cols] = kn
        kb_ref[:, cols] = kn.astype(BF16)

    def put_v(acc, cols):
        vf_ref[:, cols] = acc
        vb_ref[:, cols] = acc.astype(BF16)

    section(0, put(q_ref, lambda a: _silu(a).astype(BF16)))
    section(1, put(f_ref, lambda a: a))
    section(2, put(i_ref, lambda a: a.astype(BF16)))
    section(3, put(g_ref, lambda a: _silu(a).astype(BF16)))
    section(4, put(qa_ref, lambda a: (_group_norm64(a, bd_ref[...], qg_ref[...])
                                      * (LOG2E * D_QK ** -0.5)).astype(BF16)), chunked=False)
    section(5, put_k, chunked=False)
    section(6, put_v)


def _in_proj(x, mod, g1, w_in_b, qg, kg, bd):
    batch, seq, d = x.shape
    n = batch * seq
    bb, t = _row_tiling(batch, seq, ROW_TILE)
    tm = bb * t
    tiles_per_mod = (seq // t) if bb == 1 else 1
    sec = 1024
    n_sec = w_in_b.shape[1] // sec
    xv = x.reshape(n // t, t, d)
    row = lambda i, j: (i, 0)
    out_dt = [BF16, F32, BF16, BF16, BF16, F32, BF16, F32, BF16]
    return pl.pallas_call(
        _in_kernel,
        out_shape=[jax.ShapeDtypeStruct((n, sec), dt) for dt in out_dt],
        grid=(n // tm, n_sec),
        in_specs=[pl.BlockSpec((bb, t, d), lambda i, j: (i, 0, 0)),
                  pl.BlockSpec((bb, 6, d), lambda i, j: (i // tiles_per_mod, 0, 0)),
                  pl.BlockSpec((1, d), lambda i, j: (0, 0)),
                  pl.BlockSpec((d, sec), lambda i, j: (0, j)),
                  pl.BlockSpec((1, LANES), lambda i, j: (0, 0)),
                  pl.BlockSpec((1, LANES), lambda i, j: (0, 0)),
                  pl.BlockSpec((LANES, LANES), lambda i, j: (0, 0))],
        out_specs=[pl.BlockSpec((tm, sec), row) for _ in out_dt],
        scratch_shapes=[pltpu.VMEM((tm, d), BF16)],
        compiler_params=_cparams(("arbitrary", "arbitrary")),
        name="in_proj",
    )(xv, mod, g1, w_in_b, qg, kg, bd)


def _hgrn_consts():
    c = CHUNK
    t = np.arange(c)[:, None]
    s = np.arange(c)[None, :]
    sums = [(s <= t), (s > t)]
    masks = []
    for m in HGRN_LEVELS:
        start = (t // (2 * m)) * (2 * m)
        ref = start + m - 1
        lower = (t % (2 * m)) >= m
        sums.append(np.where(lower, (s > ref) & (s <= t), (s > t) & (s <= ref)))
        s_start = (s // (2 * m)) * (2 * m)
        masks.append((s_start == start) & lower & ((s % (2 * m)) < m))
    masks.append(s == t)
    sum_mat = np.concatenate(sums, axis=0).astype(np.float32)
    sum_cat = np.concatenate([sum_mat, sum_mat], axis=1)
    mask_mat = np.concatenate(masks, axis=0).astype(np.float32)
    return jnp.asarray(sum_cat, BF16), jnp.asarray(mask_mat, F32)


def _hgrn_kernel(q_ref, f_ref, i_ref, g_ref, s0_ref, lb_ref, og_ref, sum_ref, mask_ref,
                 o_ref, sout_ref, st_scr, *, n_steps):
    ci = pl.program_id(1)

    @pl.when(ci == 0)
    def _():
        for h in range(N_HEADS):
            st_scr[h] = s0_ref[0, h].T

    for cc in range(f_ref.shape[0] // CHUNK):
        _hgrn_chunk(slice(cc * CHUNK, (cc + 1) * CHUNK), q_ref, f_ref, i_ref, g_ref, lb_ref, og_ref,
                    sum_ref, mask_ref, o_ref, st_scr)

    @pl.when(ci == n_steps - 1)
    def _():
        for h in range(N_HEADS):
            sout_ref[0, h] = st_scr[h].T


def _hgrn_chunk(rows, q_ref, f_ref, i_ref, g_ref, lb_ref, og_ref, sum_ref, mask_ref, o_ref, st_scr):
    c = CHUNK
    hd = HEAD_DIM
    f = f_ref[rows, :]
    lb = lb_ref[...]
    e = jnp.exp(-jnp.abs(f))
    r = 1.0 / (1.0 + e)
    pos = f >= 0.0
    sig = jnp.where(pos, r, e * r)
    nsig = jnp.where(pos, e * r, r)
    logf = jnp.log(lb + (1.0 - lb) * sig)
    kin = (1.0 - lb) * nsig

    hi = logf.astype(BF16)
    lo = (logf - hi.astype(F32)).astype(BF16)
    expo = _dot(sum_ref[...], jnp.concatenate([hi, lo], axis=0))
    dec = jnp.exp(expo)

    masks = mask_ref[...]
    og = og_ref[...]
    n_lv = len(HGRN_LEVELS)
    for h in range(N_HEADS):
        cs = slice(h * hd, (h + 1) * hd)
        q = q_ref[rows, cs].astype(F32)
        k = kin[:, cs]
        v = i_ref[rows, cs]
        d0 = dec[0:c, cs]
        d1 = dec[c:2 * c, cs]
        st = st_scr[h]
        o = _dot_nt((q * d0).astype(BF16), st.astype(BF16))
        a = _dot_nt(q.astype(BF16), k.astype(BF16)) * masks[n_lv * c:(n_lv + 1) * c]
        for lv in range(n_lv):
            dl = dec[(2 + lv) * c:(3 + lv) * c, cs]
            a = a + _dot_nt((q * dl).astype(BF16), (k * dl).astype(BF16)) * masks[lv * c:(lv + 1) * c]
        o = o + _dot(a.astype(BF16), v)
        st_scr[h] = st * d0[c - 1:c, :] + _dot_tn(v, (k * d1).astype(BF16))
        ms = jnp.mean(o * o, axis=-1, keepdims=True)
        o = o * lax.rsqrt(ms + EPS) * og * g_ref[rows, cs].astype(F32)
        o_ref[rows, cs] = o.astype(BF16)


def _hgrn(q, f, i, g, s0, lb, og, batch, seq):
    n, w = q.shape
    cps = HGRN_CHUNKS_PER_STEP if (seq // CHUNK) % HGRN_CHUNKS_PER_STEP == 0 else 1
    rows = cps * CHUNK
    nc = seq // rows
    sum_cat, mask_mat = _hgrn_consts()
    row = lambda b, c: (b * nc + c, 0)
    const = lambda b, c: (0, 0)
    return pl.pallas_call(
        functools.partial(_hgrn_kernel, n_steps=nc),
        out_shape=[jax.ShapeDtypeStruct((n, w), BF16),
                   jax.ShapeDtypeStruct(s0.shape, F32)],
        grid=(batch, nc),
        in_specs=[pl.BlockSpec((rows, w), row),
                  pl.BlockSpec((rows, w), row),
                  pl.BlockSpec((rows, w), row),
                  pl.BlockSpec((rows, w), row),
                  pl.BlockSpec((1,) + s0.shape[1:], lambda b, c: (b, 0, 0, 0)),
                  pl.BlockSpec((1, w), const),
                  pl.BlockSpec((1, HEAD_DIM), const),
                  pl.BlockSpec(sum_cat.shape, const),
                  pl.BlockSpec(mask_mat.shape, const)],
        out_specs=[pl.BlockSpec((rows, w), row),
                   pl.BlockSpec((1,) + s0.shape[1:], lambda b, c: (b, 0, 0, 0))],
        scratch_shapes=[pltpu.VMEM((N_HEADS, HEAD_DIM, HEAD_DIM), F32)],
        compiler_params=_cparams(("arbitrary", "arbitrary")),
        name="hgrn2",
    )(q, f, i, g, s0, lb, og, sum_cat, mask_mat)


def _stack_maps(qt):
    lane = lax.broadcasted_iota(jnp.int32, qt.shape, 1)
    zero = jnp.zeros_like(qt)
    return jnp.concatenate([jnp.where(lane < D_QK, qt, zero), jnp.where(lane >= D_QK, qt, zero)], axis=0)


def _lane_tile(x, width):
    if width < LANES:
        return x[:, :width]
    return jnp.concatenate([x] * (width // LANES), axis=1)


def _softmax_update(s, v, m_ref, l_ref, acc_ref, rows):
    m_prev = m_ref[rows, :]
    m_next = jnp.maximum(m_prev, jnp.max(s, axis=1, keepdims=True))
    alpha = jnp.exp2(m_prev - m_next)
    p = jnp.exp2(s - _lane_tile(m_next, s.shape[1]))
    l_ref[rows, :] = alpha * l_ref[rows, :] + jnp.sum(p, axis=1, keepdims=True)
    acc_ref[rows, :] = alpha * acc_ref[rows, :] + _dot(p.astype(BF16), v)
    m_ref[rows, :] = m_next


def _diag_bias(slope, r0, rows, tq):
    r = r0 + lax.broadcasted_iota(jnp.int32, (rows, tq), 0)
    c = lax.broadcasted_iota(jnp.int32, (rows, tq), 1)
    allowed = (r // CHUNK) >= (c // CHUNK)
    return allowed, slope * (r - jnp.abs(r - c)).astype(F32)


def _lambda(lq1_ref, lk1_ref, lq2_ref, lk2_ref, lam_init):
    return (jnp.exp(jnp.sum(lq1_ref[...] * lk1_ref[...], axis=1, keepdims=True))
            - jnp.exp(jnp.sum(lq2_ref[...] * lk2_ref[...], axis=1, keepdims=True)) + lam_init)


def _attn_finish(acc_ref, l_ref, row0, tq, lam, sg, lam_init):
    a1 = acc_ref[row0:row0 + tq, :] / l_ref[row0:row0 + tq, :]
    a2 = acc_ref[row0 + tq:row0 + 2 * tq, :] / l_ref[row0 + tq:row0 + 2 * tq, :]
    o = a1 - lam * a2
    ms = jnp.mean(o * o, axis=-1, keepdims=True)
    return (o * lax.rsqrt(ms + EPS) * sg * (1.0 - lam_init)).astype(BF16)


def _attn_prompt_kernel(q_ref, k_ref, v_ref, slope_ref, lq1_ref, lk1_ref, lq2_ref, lk2_ref, sg_ref,
                        o_ref, qs, s_a, s_b, p_a, p_b, vx_a, vx_b, al_a, al_b, m_scr, l_scr, acc_scr,
                        *, tq, lam_init):
    qi = pl.program_id(2)
    slope = slope_ref[0, :, 0:1]
    rb = ATTN_ROWS
    qs[...] = _stack_maps(q_ref[...])
    bufs = ((s_a, p_a, vx_a, al_a), (s_b, p_b, vx_b, al_b))
    for _, _, vx, _ in bufs:
        vx[:, HEAD_DIM:] = jnp.ones((tq, LANES), BF16)
    m_scr[...] = jnp.full(m_scr.shape, NEG_BIG, F32)
    l_scr[...] = jnp.zeros(l_scr.shape, F32)
    acc_scr[...] = jnp.zeros(acc_scr.shape, F32)

    def scores(kj):
        return _dot_nt(qs[...], k_ref[pl.ds(kj * tq, tq), :])

    def update(buf, kj, diag):
        s_ref, p_scr, vx, al_scr = bufs[buf]
        if not diag:
            col = lax.broadcasted_iota(jnp.int32, (1, tq), 1)
            bias = slope * (col + (kj - qi) * tq).astype(F32)
        def biased(r0):
            rows = slice(r0, r0 + rb)
            if diag:
                allowed, bias_d = _diag_bias(slope, r0 % tq, rb, tq)
                return jnp.where(allowed, s_ref[rows, :] + bias_d, NEG_BIG)
            return s_ref[rows, :] + bias

        for r0 in range(0, 2 * tq, rb):
            rows = slice(r0, r0 + rb)
            m_prev = m_scr[rows, :]
            m_next = jnp.maximum(m_prev, jnp.max(biased(r0), axis=1, keepdims=True))
            al_scr[rows, :] = jnp.exp2(m_prev - m_next)
            m_scr[rows, :] = m_next
        for r0 in range(0, 2 * tq, rb):
            rows = slice(r0, r0 + rb)
            p_scr[rows, :] = jnp.exp2(biased(r0) - _lane_tile(m_scr[rows, :], tq)).astype(BF16)
        vx[:, 0:HEAD_DIM] = v_ref[pl.ds(kj * tq, tq), :]
        pv = _dot(p_scr[...], vx[...])
        alpha = al_scr[...]
        acc_scr[...] = alpha * acc_scr[...] + pv[:, 0:HEAD_DIM]
        l_scr[...] = alpha * l_scr[...] + pv[:, HEAD_DIM:]

    s_a[...] = scores(0)

    def pair_step(pj, carry):
        kj = 2 * pj
        s_b[...] = scores(kj + 1)
        update(0, kj, False)
        s_a[...] = scores(kj + 2)
        update(1, kj + 1, False)
        return carry

    lax.fori_loop(0, lax.shift_right_logical(qi, 1), pair_step, 0)
    odd = (qi & 1) == 1

    @pl.when(jnp.logical_not(odd))
    def _():
        update(0, qi, True)

    @pl.when(odd)
    def _():
        s_b[...] = scores(qi)
        update(0, qi - 1, False)
        update(1, qi, True)

    lam = _lambda(lq1_ref, lk1_ref, lq2_ref, lk2_ref, lam_init)
    o_ref[...] = _attn_finish(acc_scr, l_scr, 0, tq, lam, sg_ref[...], lam_init)


def _attn_cached_kernel(q_ref, kn_ref, vn_ref, pk_ref, pv_ref, slope_ref,
                        lq1_ref, lk1_ref, lq2_ref, lk2_ref, sg_ref,
                        o_ref, qs, m_scr, l_scr, acc_scr, *, tq, tk, p_len, lam_init):
    hd = HEAD_DIM
    for h in range(N_HEADS):
        qs[2 * tq * h:2 * tq * (h + 1), :] = _stack_maps(q_ref[:, h * hd:(h + 1) * hd])
    m_scr[...] = jnp.full(m_scr.shape, NEG_BIG, F32)
    l_scr[...] = jnp.zeros(l_scr.shape, F32)
    acc_scr[...] = jnp.zeros(acc_scr.shape, F32)
    col = lax.broadcasted_iota(jnp.int32, (1, tk), 1)

    def past_step(kj, carry):
        pos = (col + (kj * tk - p_len)).astype(F32)
        for h in range(N_HEADS):
            rows = slice(2 * tq * h, 2 * tq * (h + 1))
            k = pk_ref[0, pl.ds(kj * (tk * N_HEADS) + h, tk, stride=N_HEADS), :].astype(BF16)
            v = pv_ref[0, pl.ds(kj * (tk * N_HEADS) + h, tk, stride=N_HEADS), :].astype(BF16)
            s = _dot_nt(qs[rows, :], k) + slope_ref[h, :, 0:1] * pos
            _softmax_update(s, v, m_scr, l_scr, acc_scr, rows)
        return carry

    lax.fori_loop(0, p_len // tk, past_step, 0)

    lam = _lambda(lq1_ref, lk1_ref, lq2_ref, lk2_ref, lam_init)
    for h in range(N_HEADS):
        kd = kn_ref[:, h * hd:(h + 1) * hd]
        vd = vn_ref[:, h * hd:(h + 1) * hd]
        allowed, bias = _diag_bias(slope_ref[h, :, 0:1], 0, tq, tq)
        for half in range(2):
            rows = slice(2 * tq * h + half * tq, 2 * tq * h + (half + 1) * tq)
            s = jnp.where(allowed, _dot_nt(qs[rows, :], kd) + bias, NEG_BIG)
            _softmax_update(s, vd, m_scr, l_scr, acc_scr, rows)
        o_ref[:, h * hd:(h + 1) * hd] = _attn_finish(acc_scr, l_scr, 2 * tq * h, tq, lam, sg_ref[...],
                                                     lam_init)


def _alibi_slopes():
    s = (2.0 ** (-8.0 * np.arange(1, N_HEADS + 1) / N_HEADS)) * LOG2E
    return jnp.asarray(np.broadcast_to(s[:, None, None], (N_HEADS, 1, LANES)), F32)


def _attention_prompt(qa, kb, vb, batch, seq, lam_params, subln_g, lam_init):
    n, w = qa.shape
    tq = min(ATTN_TILE, seq)
    nq = seq // tq
    assert seq % tq == 0 and tq % CHUNK == 0
    small = lambda b, h, qi: (0, 0)
    q_spec = pl.BlockSpec((tq, HEAD_DIM), lambda b, h, qi: (b * nq + qi, h))
    kv_spec = pl.BlockSpec((seq, HEAD_DIM), lambda b, h, qi: (b, h))
    return pl.pallas_call(
        functools.partial(_attn_prompt_kernel, tq=tq, lam_init=lam_init),
        out_shape=jax.ShapeDtypeStruct((n, w), BF16),
        grid=(batch, N_HEADS, nq),
        in_specs=[q_spec, kv_spec, kv_spec,
                  pl.BlockSpec((1, 1, LANES), lambda b, h, qi: (h, 0, 0)),
                  pl.BlockSpec((1, D_QK), small), pl.BlockSpec((1, D_QK), small),
                  pl.BlockSpec((1, D_QK), small), pl.BlockSpec((1, D_QK), small),
                  pl.BlockSpec((1, HEAD_DIM), small)],
        out_specs=q_spec,
        scratch_shapes=[pltpu.VMEM((2 * tq, HEAD_DIM), BF16),
                        pltpu.VMEM((2 * tq, tq), F32), pltpu.VMEM((2 * tq, tq), F32),
                        pltpu.VMEM((2 * tq, tq), BF16), pltpu.VMEM((2 * tq, tq), BF16),
                        pltpu.VMEM((tq, HEAD_DIM + LANES), BF16), pltpu.VMEM((tq, HEAD_DIM + LANES), BF16),
                        pltpu.VMEM((2 * tq, LANES), F32), pltpu.VMEM((2 * tq, LANES), F32),
                        pltpu.VMEM((2 * tq, LANES), F32), pltpu.VMEM((2 * tq, LANES), F32),
                        pltpu.VMEM((2 * tq, HEAD_DIM), F32)],
        compiler_params=_cparams(("arbitrary", "arbitrary", "arbitrary")),
        name="diff_attn",
    )(qa, kb, vb, _alibi_slopes(), *lam_params, subln_g)


def _attention_cached(qa, kb, vb, batch, seq, past_k, past_v, lam_params, subln_g, lam_init):
    n, w = qa.shape
    p_len = past_k.shape[1]
    tk = min(SAMPLE_KEY_TILE, p_len)
    assert p_len % tk == 0 and p_len % CHUNK == 0 and seq == CHUNK
    rows = 2 * seq * N_HEADS
    small = lambda b: (0, 0)
    tok_spec = pl.BlockSpec((seq, w), lambda b: (b, 0))
    past_spec = pl.BlockSpec((1, p_len * N_HEADS, HEAD_DIM), lambda b: (b, 0, 0))
    return pl.pallas_call(
        functools.partial(_attn_cached_kernel, tq=seq, tk=tk, p_len=p_len, lam_init=lam_init),
        out_shape=jax.ShapeDtypeStruct((n, w), BF16),
        grid=(batch,),
        in_specs=[tok_spec, tok_spec, tok_spec, past_spec, past_spec,
                  pl.BlockSpec((N_HEADS, 1, LANES), lambda b: (0, 0, 0)),
                  pl.BlockSpec((1, D_QK), small), pl.BlockSpec((1, D_QK), small),
                  pl.BlockSpec((1, D_QK), small), pl.BlockSpec((1, D_QK), small),
                  pl.BlockSpec((1, HEAD_DIM), small)],
        out_specs=tok_spec,
        scratch_shapes=[pltpu.VMEM((rows, HEAD_DIM), BF16), pltpu.VMEM((rows, LANES), F32),
                        pltpu.VMEM((rows, LANES), F32), pltpu.VMEM((rows, HEAD_DIM), F32)],
        compiler_params=_cparams(("arbitrary",)),
        name="diff_attn_cached",
    )(qa, kb, vb, past_k.reshape(batch, p_len * N_HEADS, HEAD_DIM),
      past_v.reshape(batch, p_len * N_HEADS, HEAD_DIM), _alibi_slopes(), *lam_params, subln_g)


def _split_bf16(x):
    hi = x.astype(BF16)
    lo = (x - hi.astype(F32)).astype(BF16)
    return hi, lo


def _out_kernel(orp_ref, oap_ref, xp_ref, modp_ref, ors_ref, oas_ref, xs_ref, mods_ref,
                g2_ref, w_ref, rw_ref, rb_ref, tri_ref,
                x1_ref, h2_ref, route_ref, cnt_ref, cnt_scr, *, n_tiles_p):
    i = pl.program_id(0)

    @pl.when(i == 0)
    def _():
        cnt_scr[...] = jnp.zeros_like(cnt_scr)

    args = (g2_ref, w_ref, rw_ref, rb_ref, tri_ref, x1_ref, h2_ref, route_ref, cnt_scr)

    @pl.when(i < n_tiles_p)
    def _():
        _out_tile(orp_ref, oap_ref, xp_ref, modp_ref, *args)

    @pl.when(i >= n_tiles_p)
    def _():
        _out_tile(ors_ref, oas_ref, xs_ref, mods_ref, *args)

    cnt_ref[...] = jnp.broadcast_to(cnt_scr[...], cnt_ref.shape)


def _out_tile(or_ref, oa_ref, x_ref, mod_ref, g2_ref, w_ref, rw_ref, rb_ref, tri_ref,
              x1_ref, h2_ref, route_ref, cnt_scr):
    bb, t, d = x_ref.shape
    tm = bb * t
    half = or_ref.shape[1]
    mix = _dot(or_ref[...], w_ref[0:half, :]) + _dot(oa_ref[...], w_ref[half:, :])
    m = mod_ref[...]
    x1 = x_ref[...] + m[:, 2:3, :] * mix.reshape(bb, t, d)
    ms = jnp.mean(x1 * x1, axis=-1, keepdims=True)
    h2 = x1 * lax.rsqrt(ms + EPS) * g2_ref[...] * (1.0 + m[:, 4:5, :]) + m[:, 3:4, :]
    x1_ref[...] = x1.reshape(tm, d)
    h2 = h2.reshape(tm, d)
    slabs = d // LANES
    for j in range(slabs):
        h2_ref[pl.ds(j, tm, stride=slabs), :] = h2[:, j * LANES:(j + 1) * LANES]

    hh, hl = _split_bf16(h2)
    wh, wl = _split_bf16(rw_ref[...])
    logits = _dot(hh, wh) + _dot(hl, wh) + _dot(hh, wl) + rb_ref[...]
    lane = lax.broadcasted_iota(jnp.int32, logits.shape, 1)
    neg = jnp.float32(-jnp.inf)
    big = jnp.int32(1 << 20)

    is_g = (lane >= N_EXPERTS) & (lane < N_EXPERTS + N_GROUPS)
    gl = jnp.where(is_g, logits, neg)
    gmax = jnp.max(gl, axis=1, keepdims=True)
    gidx = jnp.min(jnp.where(gl == gmax, lane - N_EXPERTS, big), axis=1, keepdims=True)
    g_prob = 1.0 / jnp.sum(jnp.exp(gl - gmax), axis=1, keepdims=True)

    in_grp = (lane < N_EXPERTS) & ((lane // EXPERTS_PER_GROUP) == gidx)
    el = jnp.where(in_grp, logits, neg)
    m1 = jnp.max(el, axis=1, keepdims=True)
    i1 = jnp.min(jnp.where(el == m1, lane, big), axis=1, keepdims=True)
    el2 = jnp.where(lane == i1, neg, el)
    m2 = jnp.max(el2, axis=1, keepdims=True)
    i2 = jnp.min(jnp.where(el2 == m2, lane, big), axis=1, keepdims=True)
    e21 = jnp.exp(m2 - m1)
    w1 = g_prob / (1.0 + e21)
    w2 = g_prob * e21 / (1.0 + e21)
    tri = tri_ref[...]
    oh1 = jnp.where(lane == i1, 1.0, 0.0)
    oh2 = jnp.where(lane == i2, 1.0, 0.0)
    cnt = cnt_scr[...]
    c1 = jnp.sum(oh1, axis=0, keepdims=True)
    c2 = jnp.sum(oh2, axis=0, keepdims=True)
    r1 = jnp.sum(oh1 * (_dot(tri, oh1.astype(BF16)) + cnt), axis=1, keepdims=True)
    r2 = jnp.sum(oh2 * (_dot(tri, oh2.astype(BF16)) + cnt + c1), axis=1, keepdims=True)
    cnt_scr[...] = cnt + c1 + c2

    vals = (i1.astype(F32), i2.astype(F32), w1, w2, r1, r2)
    route = jnp.zeros(logits.shape, F32)
    for k, v in enumerate(vals):
        route = jnp.where(lane == k, v, route)
    route_ref[...] = route


def _out_proj(o_rp, o_ap, xp, mod_p, o_rs, o_as, xs, mod_s, g2, w_out_b, rw, rb):
    d = xp.shape[2]
    half = o_rp.shape[1]
    n_p = xp.shape[0] * xp.shape[1]
    n_s = xs.shape[0] * xs.shape[1]
    n = n_p + n_s
    bbp, tp = _row_tiling(xp.shape[0], xp.shape[1], ROW_TILE)
    bbs, ts = _row_tiling(xs.shape[0], xs.shape[1], ROW_TILE)
    tm = bbp * tp
    assert bbs * ts == tm and n_p % tm == 0 and n_s % tm == 0
    ntp = n_p // tm
    nts = n_s // tm
    modp_tiles = (xp.shape[1] // tp) if bbp == 1 else 1
    mods_tiles = (xs.shape[1] // ts) if bbs == 1 else 1
    pi = lambda i: jnp.minimum(i, ntp - 1)
    si = lambda i: jnp.maximum(i - ntp, 0)
    tri = jnp.asarray(np.tril(np.ones((tm, tm), np.float32), -1), BF16)
    row = lambda i: (i, 0)
    const = lambda i: (0, 0)
    return pl.pallas_call(
        functools.partial(_out_kernel, n_tiles_p=ntp),
        out_shape=[jax.ShapeDtypeStruct((n, d), F32), jax.ShapeDtypeStruct((n * (d // LANES), LANES), F32),
                   jax.ShapeDtypeStruct((n, LANES), F32), jax.ShapeDtypeStruct((8, LANES), F32)],
        grid=(ntp + nts,),
        in_specs=[pl.BlockSpec((tm, half), lambda i: (pi(i), 0)),
                  pl.BlockSpec((tm, half), lambda i: (pi(i), 0)),
                  pl.BlockSpec((bbp, tp, d), lambda i: (pi(i), 0, 0)),
                  pl.BlockSpec((bbp, 6, d), lambda i: (pi(i) // modp_tiles, 0, 0)),
                  pl.BlockSpec((tm, half), lambda i: (si(i), 0)),
                  pl.BlockSpec((tm, half), lambda i: (si(i), 0)),
                  pl.BlockSpec((bbs, ts, d), lambda i: (si(i), 0, 0)),
                  pl.BlockSpec((bbs, 6, d), lambda i: (si(i) // mods_tiles, 0, 0)),
                  pl.BlockSpec((1, d), const),
                  pl.BlockSpec(w_out_b.shape, const),
                  pl.BlockSpec(rw.shape, const),
                  pl.BlockSpec((1, LANES), const),
                  pl.BlockSpec((tm, tm), const)],
        out_specs=[pl.BlockSpec((tm, d), row), pl.BlockSpec((tm * (d // LANES), LANES), row),
                   pl.BlockSpec((tm, LANES), row), pl.BlockSpec((8, LANES), const)],
        scratch_shapes=[pltpu.VMEM((1, LANES), F32)],
        compiler_params=_cparams(("arbitrary",)),
        name="out_proj_router",
    )(o_rp, o_ap, xp.reshape(n_p // tp, tp, d), mod_p,
      o_rs, o_as, xs.reshape(n_s // ts, ts, d), mod_s, g2, w_out_b, rw, rb, tri)


ROW_PITCH = 24
MOE_GATHER_BUFS = 3


def _moe_kernel(tok_ref, be_ref, nu_ref,
                h2_hbm, wg_ref, wu_ref, wd_ref, out_ref,
                xbuf, xb, hb, wgb, wub, wdb, gsem, *, n_blk):
    i = pl.program_id(0)
    n_used = nu_ref[0]
    bm, d = xb.shape
    slabs = d // LANES
    de = wgb.shape[1]
    n_up = de // MXU_COLS
    n_down = d // MXU_COLS
    n_gbuf = xbuf.shape[0]
    gslot = lax.rem(i, n_gbuf)
    gnext = lax.rem(i + n_gbuf - 1, n_gbuf)

    def gather_copy(tok, r, s):
        src = pl.multiple_of(tok * slabs, slabs)
        return pltpu.make_async_copy(h2_hbm.at[pl.ds(src, slabs)],
                                     xbuf.at[s, pl.ds(r * ROW_PITCH, slabs)], gsem.at[s])

    def start_gather(blk, s, rows):
        for r in rows:
            gather_copy(tok_ref[blk * bm + r], r, s).start(priority=r % 2)

    def wait_gather(s):
        for r in range(bm):
            gather_copy(0, r, s).wait()

    @pl.when(i >= n_used)
    def _():
        out_ref[...] = jnp.zeros_like(out_ref)

    @pl.when(i < n_used)
    def _():
        @pl.when(i == 0)
        def _():
            for b in range(n_gbuf - 1):
                start_gather(min(b, n_blk - 1), b, range(bm))

        new_expert = jnp.logical_or(i == 0, be_ref[i] != be_ref[jnp.maximum(i - 1, 0)])

        @pl.when(new_expert)
        def _():
            wgb[...] = wg_ref[0].astype(BF16)
            wub[...] = wu_ref[0].astype(BF16)
            wdb[...] = wd_ref[0].astype(BF16)

        wait_gather(gslot)
        for j in range(slabs):
            xb[:, j * LANES:(j + 1) * LANES] = xbuf[gslot, pl.ds(j, bm, stride=ROW_PITCH), :].astype(BF16)
        nxt = jnp.minimum(i + n_gbuf - 1, n_blk - 1)

        def copy_batch(k, n_batches, base):
            per = (bm // 2) // n_batches
            start_gather(nxt, gnext, range(base + k * per, base + (k + 1) * per))

        for c in range(n_up):
            cols = slice(c * MXU_COLS, (c + 1) * MXU_COLS)
            hb[:, cols] = (_silu(_dot(xb[...], wgb[:, cols])) * _dot(xb[...], wub[:, cols])).astype(BF16)
            copy_batch(c, n_up, 0)

        for c in range(n_down):
            y = _dot(hb[...], wdb[:, c * MXU_COLS:(c + 1) * MXU_COLS])
            for jj in range(MXU_COLS // LANES):
                j = c * (MXU_COLS // LANES) + jj
                out_ref[pl.ds(j, bm, stride=slabs), :] = y[:, jj * LANES:(jj + 1) * LANES]
            copy_batch(c, n_down, bm // 2)

        @pl.when(i == n_used - 1)
        def _():
            for b in range(1, n_gbuf):
                wait_gather(lax.rem(i + b, n_gbuf))


def _moe(h2, route, counts, wg, wu, wd):
    n = route.shape[0]
    d = wg.shape[1]
    slabs = d // LANES
    assert h2.shape == (n * slabs, LANES)
    s_rows = n * TOP_K
    bm = MOE_ROWS
    n_blk = -(-s_rows // bm) + N_EXPERTS
    n_pad = n_blk * bm

    counts = counts[0, :N_EXPERTS].astype(jnp.int32)
    padded = (counts + bm - 1) // bm * bm
    pad_end = jnp.cumsum(padded)
    pad_start = pad_end - padded
    expert = route[:, 0:TOP_K].astype(jnp.int32)
    rank = route[:, 2 * TOP_K:3 * TOP_K].astype(jnp.int32)
    onehot = expert[:, :, None] == jnp.arange(N_EXPERTS, dtype=jnp.int32)[None, None, :]
    dest = jnp.sum(jnp.where(onehot, pad_start[None, None, :], 0), axis=2) + rank
    tok = jnp.broadcast_to(jnp.arange(n, dtype=jnp.int32)[:, None], (n, TOP_K))
    row_tok = jnp.zeros((n_pad,), jnp.int32).at[dest.reshape(-1)].set(tok.reshape(-1))
    blk_start = jnp.arange(n_blk, dtype=jnp.int32) * bm
    blk_e = jnp.minimum(jnp.sum((pad_end[None, :] <= blk_start[:, None]).astype(jnp.int32), axis=1),
                        N_EXPERTS - 1)
    n_used = (pad_end[-1] // bm).astype(jnp.int32).reshape(1)

    de = wg.shape[2]
    assert de % MXU_COLS == 0 and d % MXU_COLS == 0 and slabs <= ROW_PITCH
    assert (bm // 2) % (de // MXU_COLS) == 0 and (bm // 2) % (d // MXU_COLS) == 0
    wmap = lambda i, tok, be, nu: (be[i], 0, 0)
    grid_spec = pltpu.PrefetchScalarGridSpec(
        num_scalar_prefetch=3,
        grid=(n_blk,),
        in_specs=[pl.BlockSpec(memory_space=pl.ANY),
                  pl.BlockSpec((1, d, de), wmap),
                  pl.BlockSpec((1, d, de), wmap),
                  pl.BlockSpec((1, de, d), wmap)],
        out_specs=pl.BlockSpec((bm * slabs, LANES), lambda i, tok, be, nu: (i, 0)),
        scratch_shapes=[pltpu.VMEM((MOE_GATHER_BUFS, bm * ROW_PITCH, LANES), F32),
                        pltpu.VMEM((bm, d), BF16), pltpu.VMEM((bm, de), BF16),
                        pltpu.VMEM((d, de), BF16), pltpu.VMEM((d, de), BF16),
                        pltpu.VMEM((de, d), BF16),
                        pltpu.SemaphoreType.DMA((MOE_GATHER_BUFS,))])
    rows = pl.pallas_call(
        functools.partial(_moe_kernel, n_blk=n_blk),
        out_shape=jax.ShapeDtypeStruct((n_pad * slabs, LANES), F32),
        grid_spec=grid_spec,
        compiler_params=_cparams(("arbitrary",)),
        name="moe_experts",
    )(row_tok, blk_e, n_used, h2, wg, wu, wd)
    return rows, dest


def _final_kernel(dest_ref, x1_ref, rows_hbm, route_ref, mod_ref, y_ref, rbuf, sem, *, tok_off, n_steps):
    i = pl.program_id(0)
    bb, t, d = x1_ref.shape
    tm = bb * t
    slabs = d // LANES
    slot = lax.rem(i, 2)

    def row_copy(pos, r, k, s):
        src = pl.multiple_of(pos * slabs, slabs)
        return pltpu.make_async_copy(rows_hbm.at[pl.ds(src, slabs)],
                                     rbuf.at[s, k, pl.ds(r * ROW_PITCH, slabs)], sem.at[s])

    def start_rows(step, s):
        for r in range(tm):
            for k in range(TOP_K):
                pos = dest_ref[(tok_off + step * tm + r) * TOP_K + k]
                row_copy(pos, r, k, s).start(priority=(r + k) % 2)

    def wait_rows(s):
        for r in range(tm):
            for k in range(TOP_K):
                row_copy(0, r, k, s).wait()

    @pl.when(i == 0)
    def _():
        start_rows(0, 0)

    @pl.when(i + 1 < n_steps)
    def _():
        start_rows(i + 1, 1 - slot)

    wait_rows(slot)
    w = route_ref[...]
    g_f = mod_ref[...][:, 5:6, :]
    for j in range(slabs):
        cols = slice(j * LANES, (j + 1) * LANES)
        moe = None
        for k in range(TOP_K):
            rk = rbuf[slot, k, pl.ds(j, tm, stride=ROW_PITCH), :].reshape(bb, t, LANES)
            term = w[:, :, TOP_K + k:TOP_K + k + 1] * rk
            moe = term if moe is None else moe + term
        y_ref[:, :, cols] = x1_ref[:, :, cols] + g_f[:, :, cols] * moe


def _final(x1, rows, dest, route, mod, batch, seq, row_off):
    n_all, d = x1.shape
    n = batch * seq
    bb, t = _row_tiling(batch, seq, FINAL_TILE)
    tm = bb * t
    tiles_per_mod = (seq // t) if bb == 1 else 1
    assert row_off % tm == 0 and n % tm == 0
    off = row_off // tm
    n_steps = n // tm
    grid_spec = pltpu.PrefetchScalarGridSpec(
        num_scalar_prefetch=1,
        grid=(n_steps,),
        in_specs=[pl.BlockSpec((bb, t, d), lambda i, dst: (i + off, 0, 0)),
                  pl.BlockSpec(memory_space=pl.ANY),
                  pl.BlockSpec((bb, t, LANES), lambda i, dst: (i + off, 0, 0)),
                  pl.BlockSpec((bb, 6, d), lambda i, dst: (i // tiles_per_mod, 0, 0))],
        out_specs=pl.BlockSpec((bb, t, d), lambda i, dst: (i, 0, 0)),
        scratch_shapes=[pltpu.VMEM((2, TOP_K, tm * ROW_PITCH, LANES), F32),
                        pltpu.SemaphoreType.DMA((2,))])
    return pl.pallas_call(
        functools.partial(_final_kernel, tok_off=row_off, n_steps=n_steps),
        out_shape=jax.ShapeDtypeStruct((n // t, t, d), F32),
        grid_spec=grid_spec,
        compiler_params=_cparams(("arbitrary",)),
        name="final_residual",
    )(dest.reshape(-1), x1.reshape(-1, t, d), rows, route.reshape(-1, t, LANES), mod).reshape(batch, seq, d)


def _mixer_group(x, mod, past_k, past_v, s0, lyr):
    batch, seq, d = x.shape
    q, f, i, g, qa, kf, kb, vf, vb = _in_proj(x, mod, lyr["g1"], lyr["w_in"], lyr["qg"], lyr["kg"],
                                               lyr["bd"])
    o_r, s_new = _hgrn(q, f, i, g, s0, lyr["lb"], lyr["og"], batch, seq)
    if past_k is None:
        o_a = _attention_prompt(qa, kb, vb, batch, seq, lyr["lam_params"], lyr["sg"], lyr["lam_init"])
    else:
        o_a = _attention_cached(qa, kb, vb, batch, seq, past_k, past_v, lyr["lam_params"], lyr["sg"],
                                lyr["lam_init"])
    k_new = kf.reshape(batch, seq, N_HEADS, HEAD_DIM)
    v_new = vf.reshape(batch, seq, N_HEADS, HEAD_DIM)
    return o_r, o_a, k_new, v_new, s_new


def kernel(x_prompt, x_sample, cache_k_attn, cache_v_attn, state_hgrn, c_prompt, c_sample, w_ada, b_ada, norm1_g, norm2_g, w_in, w_out, hgrn_lb_param, hgrn_onorm_g, attn_qnorm_g, attn_knorm_g, lambda_q1, lambda_k1, lambda_q2, lambda_k2, attn_subln_g, router_group_w, router_group_b, router_expert_w, router_expert_b, expert_w_gate, expert_w_up, expert_w_down):
    depth = w_ada.shape[0]
    bp, tp, d = x_prompt.shape
    bs, ts, _ = x_sample.shape
    d_h = N_HEADS * HEAD_DIM
    lb_all = jnp.cumsum(jax.nn.softmax(hgrn_lb_param.astype(F32), axis=0), axis=0)
    bd = jnp.asarray(np.kron(np.eye(LANES // D_QK), np.ones((D_QK, D_QK))), BF16)

    xp, xs = x_prompt, x_sample
    kp, vp, sp, ksm, vsm, ssm = [], [], [], [], [], []
    for l in range(depth):
        wl = w_in[l]
        qk = wl[:, 4 * d_h:4 * d_h + 4 * N_HEADS * D_QK].reshape(d, 2, 2, N_HEADS, D_QK)
        qk = qk.transpose(0, 1, 3, 2, 4).reshape(d, 4 * N_HEADS * D_QK)
        w_in_b = jnp.concatenate([wl[:, :4 * d_h], qk, wl[:, 4 * d_h + 4 * N_HEADS * D_QK:]],
                                 axis=1).astype(BF16)
        rw = jnp.zeros((d, LANES), F32)
        rw = rw.at[:, :N_EXPERTS].set(router_expert_w[l])
        rw = rw.at[:, N_EXPERTS:N_EXPERTS + N_GROUPS].set(router_group_w[l])
        rb = jnp.zeros((1, LANES), F32)
        rb = rb.at[0, :N_EXPERTS].set(router_expert_b[l])
        rb = rb.at[0, N_EXPERTS:N_EXPERTS + N_GROUPS].set(router_group_b[l])
        lyr = dict(
            g1=norm1_g[l].reshape(1, d), g2=norm2_g[l].reshape(1, d),
            w_in=w_in_b, w_out=w_out[l].astype(BF16),
            qg=jnp.tile(attn_qnorm_g[l], 2).reshape(1, LANES),
            kg=jnp.tile(attn_knorm_g[l], 2).reshape(1, LANES),
            bd=bd, lb=lb_all[l].reshape(1, d_h), og=hgrn_onorm_g[l].reshape(1, HEAD_DIM),
            sg=attn_subln_g[l].reshape(1, HEAD_DIM),
            lam_params=[p[l].reshape(1, D_QK) for p in (lambda_q1, lambda_k1, lambda_q2, lambda_k2)],
            lam_init=0.8 - 0.6 * math.exp(-0.3 * l),
            rw=rw, rb=rb)

        c_all = jnp.concatenate([c_prompt, c_sample], axis=0)
        rows = c_all.shape[0]
        rows_pad = -(-rows // 8) * 8
        mod = _ada(jnp.pad(c_all, ((0, rows_pad - rows), (0, 0))), w_ada[l], b_ada[l])
        mod_p = mod[:bp].reshape(bp, 6, d)
        mod_s = mod[bp:bp + bs].reshape(bs, 6, d)

        s0_p = jnp.zeros((bp, N_HEADS, HEAD_DIM, HEAD_DIM), F32)
        o_rp, o_ap, k_n, v_n, s_n = _mixer_group(xp, mod_p, None, None, s0_p, lyr)
        kp.append(k_n)
        vp.append(v_n)
        sp.append(s_n)
        o_rs, o_as, k_n, v_n, s_n = _mixer_group(xs, mod_s, cache_k_attn[l], cache_v_attn[l],
                                                 state_hgrn[l], lyr)
        ksm.append(k_n)
        vsm.append(v_n)
        ssm.append(s_n)

        x1, h2, route, counts = _out_proj(o_rp, o_ap, xp, mod_p, o_rs, o_as, xs, mod_s,
                                          lyr["g2"], lyr["w_out"], lyr["rw"], lyr["rb"])
        rows, dest = _moe(h2, route, counts, expert_w_gate[l], expert_w_up[l], expert_w_down[l])
        xp = _final(x1, rows, dest, route, mod_p, bp, tp, 0)
        xs = _final(x1, rows, dest, route, mod_s, bs, ts, bp * tp)
    return (xp, xs, jnp.stack(kp), jnp.stack(vp), jnp.stack(sp),
            jnp.stack(ksm), jnp.stack(vsm), jnp.stack(ssm))
```

```python
import functools
import math

import numpy as np
import jax
import jax.numpy as jnp
from jax import lax
from jax.experimental import pallas as pl
from jax.experimental.pallas import tpu as pltpu

F32 = jnp.float32
BF16 = jnp.bfloat16

N_HEADS = 8
HEAD_DIM = 128
D_QK = 64
CHUNK = 64
N_GROUPS = 4
EXPERTS_PER_GROUP = 8
N_EXPERTS = N_GROUPS * EXPERTS_PER_GROUP
TOP_K = 2
EPS = 1e-6
NEG_BIG = -1e30
LOG2E = math.log2(math.e)

LANES = 128
MXU_COLS = 256
ROW_TILE = 512
ATTN_TILE = 512
ATTN_ROWS = 64
SAMPLE_KEY_TILE = 2048
MOE_ROWS = 256
FINAL_TILE = 256
VMEM_LIMIT = 56 * 1024 * 1024

HGRN_LEVELS = (32, 16, 8, 4, 2, 1)
HGRN_CHUNKS_PER_STEP = 8


def _cparams(sem, vmem=VMEM_LIMIT):
    return pltpu.CompilerParams(dimension_semantics=sem, vmem_limit_bytes=vmem)


def _dot(a, b):
    return jnp.dot(a, b, preferred_element_type=F32)


def _dot_nt(a, b):
    return lax.dot_general(a, b, (((1,), (1,)), ((), ())), preferred_element_type=F32)


def _dot_tn(a, b):
    return lax.dot_general(a, b, (((0,), (0,)), ((), ())), preferred_element_type=F32)


def _silu(x):
    return x / (1.0 + jnp.exp(-x))


def _row_tiling(batch, seq, target):
    if seq >= target:
        assert seq % target == 0
        return 1, target
    bb = max(1, min(batch, target // seq))
    while batch % bb:
        bb -= 1
    return bb, seq


def _ada_kernel(c_ref, w_ref, b_ref, o_ref):
    s = _silu(c_ref[...]).astype(BF16)
    o_ref[...] = _dot(s, w_ref[...].astype(BF16)) + b_ref[...]


def _ada(c_all, w_ada, b_ada):
    rows, d = c_all.shape
    n_out = w_ada.shape[1]
    tn = 1024
    return pl.pallas_call(
        _ada_kernel,
        out_shape=jax.ShapeDtypeStruct((rows, n_out), F32),
        grid=(n_out // tn,),
        in_specs=[pl.BlockSpec((rows, d), lambda j: (0, 0)),
                  pl.BlockSpec((d, tn), lambda j: (0, j)),
                  pl.BlockSpec((1, tn), lambda j: (0, j))],
        out_specs=pl.BlockSpec((rows, tn), lambda j: (0, j)),
        compiler_params=_cparams(("arbitrary",)),
        name="ada_mod",
    )(c_all, w_ada, b_ada.reshape(1, n_out))


def _group_norm64(acc, bd, gain):
    sq = (acc * acc).astype(BF16)
    parts = []
    for t in range(acc.shape[1] // LANES):
        ss = _dot(sq[:, t * LANES:(t + 1) * LANES], bd)
        a = acc[:, t * LANES:(t + 1) * LANES]
        parts.append(a * lax.rsqrt(ss * (1.0 / D_QK) + EPS) * gain)
    return jnp.concatenate(parts, axis=1)


def _in_kernel(x_ref, mod_ref, g1_ref, w_ref, qg_ref, kg_ref, bd_ref,
               q_ref, f_ref, i_ref, g_ref, qa_ref, kf_ref, kb_ref, vf_ref, vb_ref,
               h_scr):
    j = pl.program_id(1)
    tm, d = h_scr.shape

    @pl.when(j == 0)
    def _():
        x = x_ref[...]
        ms = jnp.mean(x * x, axis=-1, keepdims=True)
        xn = x * lax.rsqrt(ms + EPS) * g1_ref[...]
        m = mod_ref[...]
        h = xn * (1.0 + m[:, 1:2, :]) + m[:, 0:1, :]
        h_scr[...] = h.reshape(tm, d).astype(BF16)

    def section(k, epilogue, chunked=True):
        @pl.when(j == k)
        def _():
            width = w_ref.shape[1]
            if not chunked:
                epilogue(_dot(h_scr[...], w_ref[...]), slice(0, width))
                return
            for c in range(width // MXU_COLS):
                cols = slice(c * MXU_COLS, (c + 1) * MXU_COLS)
                epilogue(_dot(h_scr[...], w_ref[:, cols]), cols)

    def put(ref, fn):
        def epilogue(acc, cols):
            ref[:, cols] = fn(acc)
        return epilogue

    def put_k(acc, cols):
        kn = _group_norm64(acc, bd_ref[...], kg_ref[...])
        kf_ref[:, cols] = kn
        kb_ref[:, cols] = kn.astype(BF16)

    def put_v(acc, cols):
        vf_ref[:, cols] = acc
        vb_ref[:, cols] = acc.astype(BF16)

    section(0, put(q_ref, lambda a: _silu(a).astype(BF16)))
    section(1, put(f_ref, lambda a: a))
    section(2, put(i_ref, lambda a: a.astype(BF16)))
    section(3, put(g_ref, lambda a: _silu(a).astype(BF16)))
    section(4, put(qa_ref, lambda a: (_group_norm64(a, bd_ref[...], qg_ref[...])
                                      * (LOG2E * D_QK ** -0.5)).astype(BF16)), chunked=False)
    section(5, put_k, chunked=False)
    section(6, put_v)


def _in_proj(x, mod, g1, w_in_b, qg, kg, bd):
    batch, seq, d = x.shape
    n = batch * seq
    bb, t = _row_tiling(batch, seq, ROW_TILE)
    tm = bb * t
    tiles_per_mod = (seq // t) if bb == 1 else 1
    sec = 1024
    n_sec = w_in_b.shape[1] // sec
    xv = x.reshape(n // t, t, d)
    row = lambda i, j: (i, 0)
    out_dt = [BF16, F32, BF16, BF16, BF16, F32, BF16, F32, BF16]
    return pl.pallas_call(
        _in_kernel,
        out_shape=[jax.ShapeDtypeStruct((n, sec), dt) for dt in out_dt],
        grid=(n // tm, n_sec),
        in_specs=[pl.BlockSpec((bb, t, d), lambda i, j: (i, 0, 0)),
                  pl.BlockSpec((bb, 6, d), lambda i, j: (i // tiles_per_mod, 0, 0)),
                  pl.BlockSpec((1, d), lambda i, j: (0, 0)),
                  pl.BlockSpec((d, sec), lambda i, j: (0, j)),
                  pl.BlockSpec((1, LANES), lambda i, j: (0, 0)),
                  pl.BlockSpec((1, LANES), lambda i, j: (0, 0)),
                  pl.BlockSpec((LANES, LANES), lambda i, j: (0, 0))],
        out_specs=[pl.BlockSpec((tm, sec), row) for _ in out_dt],
        scratch_shapes=[pltpu.VMEM((tm, d), BF16)],
        compiler_params=_cparams(("arbitrary", "arbitrary")),
        name="in_proj",
    )(xv, mod, g1, w_in_b, qg, kg, bd)


def _hgrn_consts():
    c = CHUNK
    t = np.arange(c)[:, None]
    s = np.arange(c)[None, :]
    sums = [(s <= t), (s > t)]
    masks = []
    for m in HGRN_LEVELS:
        start = (t // (2 * m)) * (2 * m)
        ref = start + m - 1
        lower = (t % (2 * m)) >= m
        sums.append(np.where(lower, (s > ref) & (s <= t), (s > t) & (s <= ref)))
        s_start = (s // (2 * m)) * (2 * m)
        masks.append((s_start == start) & lower & ((s % (2 * m)) < m))
    masks.append(s == t)
    sum_mat = np.concatenate(sums, axis=0).astype(np.float32)
    sum_cat = np.concatenate([sum_mat, sum_mat], axis=1)
    mask_mat = np.concatenate(masks, axis=0).astype(np.float32)
    return jnp.asarray(sum_cat, BF16), jnp.asarray(mask_mat, F32)


def _hgrn_kernel(q_ref, f_ref, i_ref, g_ref, s0_ref, lb_ref, og_ref, sum_ref, mask_ref,
                 o_ref, sout_ref, st_scr, *, n_steps):
    ci = pl.program_id(1)

    @pl.when(ci == 0)
    def _():
        for h in range(N_HEADS):
            st_scr[h] = s0_ref[0, h].T

    for cc in range(f_ref.shape[0] // CHUNK):
        _hgrn_chunk(slice(cc * CHUNK, (cc + 1) * CHUNK), q_ref, f_ref, i_ref, g_ref, lb_ref, og_ref,
                    sum_ref, mask_ref, o_ref, st_scr)

    @pl.when(ci == n_steps - 1)
    def _():
        for h in range(N_HEADS):
            sout_ref[0, h] = st_scr[h].T


def _hgrn_chunk(rows, q_ref, f_ref, i_ref, g_ref, lb_ref, og_ref, sum_ref, mask_ref, o_ref, st_scr):
    c = CHUNK
    hd = HEAD_DIM
    f = f_ref[rows, :]
    lb = lb_ref[...]
    e = jnp.exp(-jnp.abs(f))
    r = 1.0 / (1.0 + e)
    pos = f >= 0.0
    sig = jnp.where(pos, r, e * r)
    nsig = jnp.where(pos, e * r, r)
    logf = jnp.log(lb + (1.0 - lb) * sig)
    kin = (1.0 - lb) * nsig

    hi = logf.astype(BF16)
    lo = (logf - hi.astype(F32)).astype(BF16)
    expo = _dot(sum_ref[...], jnp.concatenate([hi, lo], axis=0))
    dec = jnp.exp(expo)

    masks = mask_ref[...]
    og = og_ref[...]
    n_lv = len(HGRN_LEVELS)
    for h in range(N_HEADS):
        cs = slice(h * hd, (h + 1) * hd)
        q = q_ref[rows, cs].astype(F32)
        k = kin[:, cs]
        v = i_ref[rows, cs]
        d0 = dec[0:c, cs]
        d1 = dec[c:2 * c, cs]
        st = st_scr[h]
        o = _dot_nt((q * d0).astype(BF16), st.astype(BF16))
        a = _dot_nt(q.astype(BF16), k.astype(BF16)) * masks[n_lv * c:(n_lv + 1) * c]
        for lv in range(n_lv):
            dl = dec[(2 + lv) * c:(3 + lv) * c, cs]
            a = a + _dot_nt((q * dl).astype(BF16), (k * dl).astype(BF16)) * masks[lv * c:(lv + 1) * c]
        o = o + _dot(a.astype(BF16), v)
        st_scr[h] = st * d0[c - 1:c, :] + _dot_tn(v, (k * d1).astype(BF16))
        ms = jnp.mean(o * o, axis=-1, keepdims=True)
        o = o * lax.rsqrt(ms + EPS) * og * g_ref[rows, cs].astype(F32)
        o_ref[rows, cs] = o.astype(BF16)


def _hgrn(q, f, i, g, s0, lb, og, batch, seq):
    n, w = q.shape
    cps = HGRN_CHUNKS_PER_STEP if (seq // CHUNK) % HGRN_CHUNKS_PER_STEP == 0 else 1
    rows = cps * CHUNK
    nc = seq // rows
    sum_cat, mask_mat = _hgrn_consts()
    row = lambda b, c: (b * nc + c, 0)
    const = lambda b, c: (0, 0)
    return pl.pallas_call(
        functools.partial(_hgrn_kernel, n_steps=nc),
        out_shape=[jax.ShapeDtypeStruct((n, w), BF16),
                   jax.ShapeDtypeStruct(s0.shape, F32)],
        grid=(batch, nc),
        in_specs=[pl.BlockSpec((rows, w), row),
                  pl.BlockSpec((rows, w), row),
                  pl.BlockSpec((rows, w), row),
                  pl.BlockSpec((rows, w), row),
                  pl.BlockSpec((1,) + s0.shape[1:], lambda b, c: (b, 0, 0, 0)),
                  pl.BlockSpec((1, w), const),
                  pl.BlockSpec((1, HEAD_DIM), const),
                  pl.BlockSpec(sum_cat.shape, const),
                  pl.BlockSpec(mask_mat.shape, const)],
        out_specs=[pl.BlockSpec((rows, w), row),
                   pl.BlockSpec((1,) + s0.shape[1:], lambda b, c: (b, 0, 0, 0))],
        scratch_shapes=[pltpu.VMEM((N_HEADS, HEAD_DIM, HEAD_DIM), F32)],
        compiler_params=_cparams(("arbitrary", "arbitrary")),
        name="hgrn2",
    )(q, f, i, g, s0, lb, og, sum_cat, mask_mat)


def _stack_maps(qt):
    lane = lax.broadcasted_iota(jnp.int32, qt.shape, 1)
    zero = jnp.zeros_like(qt)
    return jnp.concatenate([jnp.where(lane < D_QK, qt, zero), jnp.where(lane >= D_QK, qt, zero)], axis=0)


def _lane_tile(x, width):
    if width < LANES:
        return x[:, :width]
    return jnp.concatenate([x] * (width // LANES), axis=1)


def _softmax_update(s, v, m_ref, l_ref, acc_ref, rows):
    m_prev = m_ref[rows, :]
    m_next = jnp.maximum(m_prev, jnp.max(s, axis=1, keepdims=True))
    alpha = jnp.exp2(m_prev - m_next)
    p = jnp.exp2(s - _lane_tile(m_next, s.shape[1]))
    l_ref[rows, :] = alpha * l_ref[rows, :] + jnp.sum(p, axis=1, keepdims=True)
    acc_ref[rows, :] = alpha * acc_ref[rows, :] + _dot(p.astype(BF16), v)
    m_ref[rows, :] = m_next


def _diag_bias(slope, r0, rows, tq):
    r = r0 + lax.broadcasted_iota(jnp.int32, (rows, tq), 0)
    c = lax.broadcasted_iota(jnp.int32, (rows, tq), 1)
    allowed = (r // CHUNK) >= (c // CHUNK)
    return allowed, slope * (r - jnp.abs(r - c)).astype(F32)


def _lambda(lq1_ref, lk1_ref, lq2_ref, lk2_ref, lam_init):
    return (jnp.exp(jnp.sum(lq1_ref[...] * lk1_ref[...], axis=1, keepdims=True))
            - jnp.exp(jnp.sum(lq2_ref[...] * lk2_ref[...], axis=1, keepdims=True)) + lam_init)


def _attn_finish(acc_ref, l_ref, row0, tq, lam, sg, lam_init):
    a1 = acc_ref[row0:row0 + tq, :] / l_ref[row0:row0 + tq, :]
    a2 = acc_ref[row0 + tq:row0 + 2 * tq, :] / l_ref[row0 + tq:row0 + 2 * tq, :]
    o = a1 - lam * a2
    ms = jnp.mean(o * o, axis=-1, keepdims=True)
    return (o * lax.rsqrt(ms + EPS) * sg * (1.0 - lam_init)).astype(BF16)


def _attn_prompt_kernel(q_ref, k_ref, v_ref, slope_ref, lq1_ref, lk1_ref, lq2_ref, lk2_ref, sg_ref,
                        o_ref, qs, s_a, s_b, p_a, p_b, vx_a, vx_b, al_a, al_b, m_scr, l_scr, acc_scr,
                        *, tq, lam_init):
    qi = pl.program_id(2)
    slope = slope_ref[0, :, 0:1]
    rb = ATTN_ROWS
    qs[...] = _stack_maps(q_ref[...])
    bufs = ((s_a, p_a, vx_a, al_a), (s_b, p_b, vx_b, al_b))
    for _, _, vx, _ in bufs:
        vx[:, HEAD_DIM:] = jnp.ones((tq, LANES), BF16)
    m_scr[...] = jnp.full(m_scr.shape, NEG_BIG, F32)
    l_scr[...] = jnp.zeros(l_scr.shape, F32)
    acc_scr[...] = jnp.zeros(acc_scr.shape, F32)

    def scores(kj):
        return _dot_nt(qs[...], k_ref[pl.ds(kj * tq, tq), :])

    def update(buf, kj, diag):
        s_ref, p_scr, vx, al_scr = bufs[buf]
        if not diag:
            col = lax.broadcasted_iota(jnp.int32, (1, tq), 1)
            bias = slope * (col + (kj - qi) * tq).astype(F32)
        def biased(r0):
            rows = slice(r0, r0 + rb)
            if diag:
                allowed, bias_d = _diag_bias(slope, r0 % tq, rb, tq)
                return jnp.where(allowed, s_ref[rows, :] + bias_d, NEG_BIG)
            return s_ref[rows, :] + bias

        for r0 in range(0, 2 * tq, rb):
            rows = slice(r0, r0 + rb)
            m_prev = m_scr[rows, :]
            m_next = jnp.maximum(m_prev, jnp.max(biased(r0), axis=1, keepdims=True))
            al_scr[rows, :] = jnp.exp2(m_prev - m_next)
            m_scr[rows, :] = m_next
        for r0 in range(0, 2 * tq, rb):
            rows = slice(r0, r0 + rb)
            p_scr[rows, :] = jnp.exp2(biased(r0) - _lane_tile(m_scr[rows, :], tq)).astype(BF16)
        vx[:, 0:HEAD_DIM] = v_ref[pl.ds(kj * tq, tq), :]
        pv = _dot(p_scr[...], vx[...])
        alpha = al_scr[...]
        acc_scr[...] = alpha * acc_scr[...] + pv[:, 0:HEAD_DIM]
        l_scr[...] = alpha * l_scr[...] + pv[:, HEAD_DIM:]

    s_a[...] = scores(0)

    def pair_step(pj, carry):
        kj = 2 * pj
        s_b[...] = scores(kj + 1)
        update(0, kj, False)
        s_a[...] = scores(kj + 2)
        update(1, kj + 1, False)
        return carry

    lax.fori_loop(0, lax.shift_right_logical(qi, 1), pair_step, 0)
    odd = (qi & 1) == 1

    @pl.when(jnp.logical_not(odd))
    def _():
        update(0, qi, True)

    @pl.when(odd)
    def _():
        s_b[...] = scores(qi)
        update(0, qi - 1, False)
        update(1, qi, True)

    lam = _lambda(lq1_ref, lk1_ref, lq2_ref, lk2_ref, lam_init)
    o_ref[...] = _attn_finish(acc_scr, l_scr, 0, tq, lam, sg_ref[...], lam_init)


def _attn_cached_kernel(q_ref, kn_ref, vn_ref, pk_ref, pv_ref, slope_ref,
                        lq1_ref, lk1_ref, lq2_ref, lk2_ref, sg_ref,
                        o_ref, qs, m_scr, l_scr, acc_scr, *, tq, tk, p_len, lam_init):
    hd = HEAD_DIM
    for h in range(N_HEADS):
        qs[2 * tq * h:2 * tq * (h + 1), :] = _stack_maps(q_ref[:, h * hd:(h + 1) * hd])
    m_scr[...] = jnp.full(m_scr.shape, NEG_BIG, F32)
    l_scr[...] = jnp.zeros(l_scr.shape, F32)
    acc_scr[...] = jnp.zeros(acc_scr.shape, F32)
    col = lax.broadcasted_iota(jnp.int32, (1, tk), 1)

    def past_step(kj, carry):
        pos = (col + (kj * tk - p_len)).astype(F32)
        for h in range(N_HEADS):
            rows = slice(2 * tq * h, 2 * tq * (h + 1))
            k = pk_ref[0, pl.ds(kj * (tk * N_HEADS) + h, tk, stride=N_HEADS), :].astype(BF16)
            v = pv_ref[0, pl.ds(kj * (tk * N_HEADS) + h, tk, stride=N_HEADS), :].astype(BF16)
            s = _dot_nt(qs[rows, :], k) + slope_ref[h, :, 0:1] * pos
            _softmax_update(s, v, m_scr, l_scr, acc_scr, rows)
        return carry

    lax.fori_loop(0, p_len // tk, past_step, 0)

    lam = _lambda(lq1_ref, lk1_ref, lq2_ref, lk2_ref, lam_init)
    for h in range(N_HEADS):
        kd = kn_ref[:, h * hd:(h + 1) * hd]
        vd = vn_ref[:, h * hd:(h + 1) * hd]
        allowed, bias = _diag_bias(slope_ref[h, :, 0:1], 0, tq, tq)
        for half in range(2):
            rows = slice(2 * tq * h + half * tq, 2 * tq * h + (half + 1) * tq)
            s = jnp.where(allowed, _dot_nt(qs[rows, :], kd) + bias, NEG_BIG)
            _softmax_update(s, vd, m_scr, l_scr, acc_scr, rows)
        o_ref[:, h * hd:(h + 1) * hd] = _attn_finish(acc_scr, l_scr, 2 * tq * h, tq, lam, sg_ref[...],
                                                     lam_init)


def _alibi_slopes():
    s = (2.0 ** (-8.0 * np.arange(1, N_HEADS + 1) / N_HEADS)) * LOG2E
    return jnp.asarray(np.broadcast_to(s[:, None, None], (N_HEADS, 1, LANES)), F32)


def _attention_prompt(qa, kb, vb, batch, seq, lam_params, subln_g, lam_init):
    n, w = qa.shape
    tq = min(ATTN_TILE, seq)
    nq = seq // tq
    assert seq % tq == 0 and tq % CHUNK == 0
    small = lambda b, h, qi: (0, 0)
    q_spec = pl.BlockSpec((tq, HEAD_DIM), lambda b, h, qi: (b * nq + qi, h))
    kv_spec = pl.BlockSpec((seq, HEAD_DIM), lambda b, h, qi: (b, h))
    return pl.pallas_call(
        functools.partial(_attn_prompt_kernel, tq=tq, lam_init=lam_init),
        out_shape=jax.ShapeDtypeStruct((n, w), BF16),
        grid=(batch, N_HEADS, nq),
        in_specs=[q_spec, kv_spec, kv_spec,
                  pl.BlockSpec((1, 1, LANES), lambda b, h, qi: (h, 0, 0)),
                  pl.BlockSpec((1, D_QK), small), pl.BlockSpec((1, D_QK), small),
                  pl.BlockSpec((1, D_QK), small), pl.BlockSpec((1, D_QK), small),
                  pl.BlockSpec((1, HEAD_DIM), small)],
        out_specs=q_spec,
        scratch_shapes=[pltpu.VMEM((2 * tq, HEAD_DIM), BF16),
                        pltpu.VMEM((2 * tq, tq), F32), pltpu.VMEM((2 * tq, tq), F32),
                        pltpu.VMEM((2 * tq, tq), BF16), pltpu.VMEM((2 * tq, tq), BF16),
                        pltpu.VMEM((tq, HEAD_DIM + LANES), BF16), pltpu.VMEM((tq, HEAD_DIM + LANES), BF16),
                        pltpu.VMEM((2 * tq, LANES), F32), pltpu.VMEM((2 * tq, LANES), F32),
                        pltpu.VMEM((2 * tq, LANES), F32), pltpu.VMEM((2 * tq, LANES), F32),
                        pltpu.VMEM((2 * tq, HEAD_DIM), F32)],
        compiler_params=_cparams(("arbitrary", "arbitrary", "arbitrary")),
        name="diff_attn",
    )(qa, kb, vb, _alibi_slopes(), *lam_params, subln_g)


def _attention_cached(qa, kb, vb, batch, seq, past_k, past_v, lam_params, subln_g, lam_init):
    n, w = qa.shape
    p_len = past_k.shape[1]
    tk = min(SAMPLE_KEY_TILE, p_len)
    assert p_len % tk == 0 and p_len % CHUNK == 0 and seq == CHUNK
    rows = 2 * seq * N_HEADS
    small = lambda b: (0, 0)
    tok_spec = pl.BlockSpec((seq, w), lambda b: (b, 0))
    past_spec = pl.BlockSpec((1, p_len * N_HEADS, HEAD_DIM), lambda b: (b, 0, 0))
    return pl.pallas_call(
        functools.partial(_attn_cached_kernel, tq=seq, tk=tk, p_len=p_len, lam_init=lam_init),
        out_shape=jax.ShapeDtypeStruct((n, w), BF16),
        grid=(batch,),
        in_specs=[tok_spec, tok_spec, tok_spec, past_spec, past_spec,
                  pl.BlockSpec((N_HEADS, 1, LANES), lambda b: (0, 0, 0)),
                  pl.BlockSpec((1, D_QK), small), pl.BlockSpec((1, D_QK), small),
                  pl.BlockSpec((1, D_QK), small), pl.BlockSpec((1, D_QK), small),
                  pl.BlockSpec((1, HEAD_DIM), small)],
        out_specs=tok_spec,
        scratch_shapes=[pltpu.VMEM((rows, HEAD_DIM), BF16), pltpu.VMEM((rows, LANES), F32),
                        pltpu.VMEM((rows, LANES), F32), pltpu.VMEM((rows, HEAD_DIM), F32)],
        compiler_params=_cparams(("arbitrary",)),
        name="diff_attn_cached",
    )(qa, kb, vb, past_k.reshape(batch, p_len * N_HEADS, HEAD_DIM),
      past_v.reshape(batch, p_len * N_HEADS, HEAD_DIM), _alibi_slopes(), *lam_params, subln_g)


def _split_bf16(x):
    hi = x.astype(BF16)
    lo = (x - hi.astype(F32)).astype(BF16)
    return hi, lo


def _out_kernel(orp_ref, oap_ref, xp_ref, modp_ref, ors_ref, oas_ref, xs_ref, mods_ref,
                g2_ref, w_ref, rw_ref, rb_ref, tri_ref,
                x1_ref, h2_ref, route_ref, cnt_ref, cnt_scr, *, n_tiles_p):
    i = pl.program_id(0)

    @pl.when(i == 0)
    def _():
        cnt_scr[...] = jnp.zeros_like(cnt_scr)

    args = (g2_ref, w_ref, rw_ref, rb_ref, tri_ref, x1_ref, h2_ref, route_ref, cnt_scr)

    @pl.when(i < n_tiles_p)
    def _():
        _out_tile(orp_ref, oap_ref, xp_ref, modp_ref, *args)

    @pl.when(i >= n_tiles_p)
    def _():
        _out_tile(ors_ref, oas_ref, xs_ref, mods_ref, *args)

    cnt_ref[...] = jnp.broadcast_to(cnt_scr[...], cnt_ref.shape)


def _out_tile(or_ref, oa_ref, x_ref, mod_ref, g2_ref, w_ref, rw_ref, rb_ref, tri_ref,
              x1_ref, h2_ref, route_ref, cnt_scr):
    bb, t, d = x_ref.shape
    tm = bb * t
    half = or_ref.shape[1]
    mix = _dot(or_ref[...], w_ref[0:half, :]) + _dot(oa_ref[...], w_ref[half:, :])
    m = mod_ref[...]
    x1 = x_ref[...] + m[:, 2:3, :] * mix.reshape(bb, t, d)
    ms = jnp.mean(x1 * x1, axis=-1, keepdims=True)
    h2 = x1 * lax.rsqrt(ms + EPS) * g2_ref[...] * (1.0 + m[:, 4:5, :]) + m[:, 3:4, :]
    x1_ref[...] = x1.reshape(tm, d)
    h2 = h2.reshape(tm, d)
    slabs = d // LANES
    for j in range(slabs):
        h2_ref[pl.ds(j, tm, stride=slabs), :] = h2[:, j * LANES:(j + 1) * LANES]

    hh, hl = _split_bf16(h2)
    wh, wl = _split_bf16(rw_ref[...])
    logits = _dot(hh, wh) + _dot(hl, wh) + _dot(hh, wl) + rb_ref[...]
    lane = lax.broadcasted_iota(jnp.int32, logits.shape, 1)
    neg = jnp.float32(-jnp.inf)
    big = jnp.int32(1 << 20)

    is_g = (lane >= N_EXPERTS) & (lane < N_EXPERTS + N_GROUPS)
    gl = jnp.where(is_g, logits, neg)
    gmax = jnp.max(gl, axis=1, keepdims=True)
    gidx = jnp.min(jnp.where(gl == gmax, lane - N_EXPERTS, big), axis=1, keepdims=True)
    g_prob = 1.0 / jnp.sum(jnp.exp(gl - gmax), axis=1, keepdims=True)

    in_grp = (lane < N_EXPERTS) & ((lane // EXPERTS_PER_GROUP) == gidx)
    el = jnp.where(in_grp, logits, neg)
    m1 = jnp.max(el, axis=1, keepdims=True)
    i1 = jnp.min(jnp.where(el == m1, lane, big), axis=1, keepdims=True)
    el2 = jnp.where(lane == i1, neg, el)
    m2 = jnp.max(el2, axis=1, keepdims=True)
    i2 = jnp.min(jnp.where(el2 == m2, lane, big), axis=1, keepdims=True)
    e21 = jnp.exp(m2 - m1)
    w1 = g_prob / (1.0 + e21)
    w2 = g_prob * e21 / (1.0 + e21)
    tri = tri_ref[...]
    oh1 = jnp.where(lane == i1, 1.0, 0.0)
    oh2 = jnp.where(lane == i2, 1.0, 0.0)
    cnt = cnt_scr[...]
    c1 = jnp.sum(oh1, axis=0, keepdims=True)
    c2 = jnp.sum(oh2, axis=0, keepdims=True)
    r1 = jnp.sum(oh1 * (_dot(tri, oh1.astype(BF16)) + cnt), axis=1, keepdims=True)
    r2 = jnp.sum(oh2 * (_dot(tri, oh2.astype(BF16)) + cnt + c1), axis=1, keepdims=True)
    cnt_scr[...] = cnt + c1 + c2

    vals = (i1.astype(F32), i2.astype(F32), w1, w2, r1, r2)
    route = jnp.zeros(logits.shape, F32)
    for k, v in enumerate(vals):
        route = jnp.where(lane == k, v, route)
    route_ref[...] = route


def _out_proj(o_rp, o_ap, xp, mod_p, o_rs, o_as, xs, mod_s, g2, w_out_b, rw, rb):
    d = xp.shape[2]
    half = o_rp.shape[1]
    n_p = xp.shape[0] * xp.shape[1]
    n_s = xs.shape[0] * xs.shape[1]
    n = n_p + n_s
    bbp, tp = _row_tiling(xp.shape[0], xp.shape[1], ROW_TILE)
    bbs, ts = _row_tiling(xs.shape[0], xs.shape[1], ROW_TILE)
    tm = bbp * tp
    assert bbs * ts == tm and n_p % tm == 0 and n_s % tm == 0
    ntp = n_p // tm
    nts = n_s // tm
    modp_tiles = (xp.shape[1] // tp) if bbp == 1 else 1
    mods_tiles = (xs.shape[1] // ts) if bbs == 1 else 1
    pi = lambda i: jnp.minimum(i, ntp - 1)
    si = lambda i: jnp.maximum(i - ntp, 0)
    tri = jnp.asarray(np.tril(np.ones((tm, tm), np.float32), -1), BF16)
    row = lambda i: (i, 0)
    const = lambda i: (0, 0)
    return pl.pallas_call(
        functools.partial(_out_kernel, n_tiles_p=ntp),
        out_shape=[jax.ShapeDtypeStruct((n, d), F32), jax.ShapeDtypeStruct((n * (d // LANES), LANES), F32),
                   jax.ShapeDtypeStruct((n, LANES), F32), jax.ShapeDtypeStruct((8, LANES), F32)],
        grid=(ntp + nts,),
        in_specs=[pl.BlockSpec((tm, half), lambda i: (pi(i), 0)),
                  pl.BlockSpec((tm, half), lambda i: (pi(i), 0)),
                  pl.BlockSpec((bbp, tp, d), lambda i: (pi(i), 0, 0)),
                  pl.BlockSpec((bbp, 6, d), lambda i: (pi(i) // modp_tiles, 0, 0)),
                  pl.BlockSpec((tm, half), lambda i: (si(i), 0)),
                  pl.BlockSpec((tm, half), lambda i: (si(i), 0)),
                  pl.BlockSpec((bbs, ts, d), lambda i: (si(i), 0, 0)),
                  pl.BlockSpec((bbs, 6, d), lambda i: (si(i) // mods_tiles, 0, 0)),
                  pl.BlockSpec((1, d), const),
                  pl.BlockSpec(w_out_b.shape, const),
                  pl.BlockSpec(rw.shape, const),
                  pl.BlockSpec((1, LANES), const),
                  pl.BlockSpec((tm, tm), const)],
        out_specs=[pl.BlockSpec((tm, d), row), pl.BlockSpec((tm * (d // LANES), LANES), row),
                   pl.BlockSpec((tm, LANES), row), pl.BlockSpec((8, LANES), const)],
        scratch_shapes=[pltpu.VMEM((1, LANES), F32)],
        compiler_params=_cparams(("arbitrary",)),
        name="out_proj_router",
    )(o_rp, o_ap, xp.reshape(n_p // tp, tp, d), mod_p,
      o_rs, o_as, xs.reshape(n_s // ts, ts, d), mod_s, g2, w_out_b, rw, rb, tri)


ROW_PITCH = 24
MOE_GATHER_BUFS = 4


def _moe_kernel(tok_ref, be_ref, nu_ref,
                h2_hbm, wg_ref, wu_ref, wd_ref, out_ref,
                xbuf, xb, hb, wgb, wub, wdb, gsem, *, n_blk):
    i = pl.program_id(0)
    n_used = nu_ref[0]
    bm, d = xb.shape
    slabs = d // LANES
    de = wgb.shape[1]
    n_up = de // MXU_COLS
    n_down = d // MXU_COLS
    n_gbuf = xbuf.shape[0]
    gslot = lax.rem(i, n_gbuf)
    gnext = lax.rem(i + n_gbuf - 1, n_gbuf)

    def gather_copy(tok, r, s):
        src = pl.multiple_of(tok * slabs, slabs)
        return pltpu.make_async_copy(h2_hbm.at[pl.ds(src, slabs)],
                                     xbuf.at[s, pl.ds(r * ROW_PITCH, slabs)], gsem.at[s])

    def start_gather(blk, s, rows):
        for r in rows:
            gather_copy(tok_ref[blk * bm + r], r, s).start(priority=r % 2)

    def wait_gather(s):
        for r in range(bm):
            gather_copy(0, r, s).wait()

    @pl.when(i >= n_used)
    def _():
        out_ref[...] = jnp.zeros_like(out_ref)

    @pl.when(i < n_used)
    def _():
        @pl.when(i == 0)
        def _():
            for b in range(n_gbuf - 1):
                start_gather(min(b, n_blk - 1), b, range(bm))

        new_expert = jnp.logical_or(i == 0, be_ref[i] != be_ref[jnp.maximum(i - 1, 0)])

        @pl.when(new_expert)
        def _():
            wgb[...] = wg_ref[0].astype(BF16)
            wub[...] = wu_ref[0].astype(BF16)
            wdb[...] = wd_ref[0].astype(BF16)

        wait_gather(gslot)
        for j in range(slabs):
            xb[:, j * LANES:(j + 1) * LANES] = xbuf[gslot, pl.ds(j, bm, stride=ROW_PITCH), :].astype(BF16)
        nxt = jnp.minimum(i + n_gbuf - 1, n_blk - 1)

        def copy_batch(k, n_batches, base):
            per = (bm // 2) // n_batches
            start_gather(nxt, gnext, range(base + k * per, base + (k + 1) * per))

        for c in range(n_up):
            cols = slice(c * MXU_COLS, (c + 1) * MXU_COLS)
            hb[:, cols] = (_silu(_dot(xb[...], wgb[:, cols])) * _dot(xb[...], wub[:, cols])).astype(BF16)
            copy_batch(c, n_up, 0)

        for c in range(n_down):
            y = _dot(hb[...], wdb[:, c * MXU_COLS:(c + 1) * MXU_COLS])
            for jj in range(MXU_COLS // LANES):
                j = c * (MXU_COLS // LANES) + jj
                out_ref[pl.ds(j, bm, stride=slabs), :] = y[:, jj * LANES:(jj + 1) * LANES]
            copy_batch(c, n_down, bm // 2)

        @pl.when(i == n_used - 1)
        def _():
            for b in range(1, n_gbuf):
                wait_gather(lax.rem(i + b, n_gbuf))


def _moe(h2, route, counts, wg, wu, wd):
    n = route.shape[0]
    d = wg.shape[1]
    slabs = d // LANES
    assert h2.shape == (n * slabs, LANES)
    s_rows = n * TOP_K
    bm = MOE_ROWS
    n_blk = -(-s_rows // bm) + N_EXPERTS
    n_pad = n_blk * bm

    counts = counts[0, :N_EXPERTS].astype(jnp.int32)
    padded = (counts + bm - 1) // bm * bm
    pad_end = jnp.cumsum(padded)
    pad_start = pad_end - padded
    expert = route[:, 0:TOP_K].astype(jnp.int32)
    rank = route[:, 2 * TOP_K:3 * TOP_K].astype(jnp.int32)
    onehot = expert[:, :, None] == jnp.arange(N_EXPERTS, dtype=jnp.int32)[None, None, :]
    dest = jnp.sum(jnp.where(onehot, pad_start[None, None, :], 0), axis=2) + rank
    tok = jnp.broadcast_to(jnp.arange(n, dtype=jnp.int32)[:, None], (n, TOP_K))
    row_tok = jnp.zeros((n_pad,), jnp.int32).at[dest.reshape(-1)].set(tok.reshape(-1))
    blk_start = jnp.arange(n_blk, dtype=jnp.int32) * bm
    blk_e = jnp.minimum(jnp.sum((pad_end[None, :] <= blk_start[:, None]).astype(jnp.int32), axis=1),
                        N_EXPERTS - 1)
    n_used = (pad_end[-1] // bm).astype(jnp.int32).reshape(1)

    de = wg.shape[2]
    assert de % MXU_COLS == 0 and d % MXU_COLS == 0 and slabs <= ROW_PITCH
    assert (bm // 2) % (de // MXU_COLS) == 0 and (bm // 2) % (d // MXU_COLS) == 0
    wmap = lambda i, tok, be, nu: (be[i], 0, 0)
    grid_spec = pltpu.PrefetchScalarGridSpec(
        num_scalar_prefetch=3,
        grid=(n_blk,),
        in_specs=[pl.BlockSpec(memory_space=pl.ANY),
                  pl.BlockSpec((1, d, de), wmap),
                  pl.BlockSpec((1, d, de), wmap),
                  pl.BlockSpec((1, de, d), wmap)],
        out_specs=pl.BlockSpec((bm * slabs, LANES), lambda i, tok, be, nu: (i, 0)),
        scratch_shapes=[pltpu.VMEM((MOE_GATHER_BUFS, bm * ROW_PITCH, LANES), F32),
                        pltpu.VMEM((bm, d), BF16), pltpu.VMEM((bm, de), BF16),
                        pltpu.VMEM((d, de), BF16), pltpu.VMEM((d, de), BF16),
                        pltpu.VMEM((de, d), BF16),
                        pltpu.SemaphoreType.DMA((MOE_GATHER_BUFS,))])
    rows = pl.pallas_call(
        functools.partial(_moe_kernel, n_blk=n_blk),
        out_shape=jax.ShapeDtypeStruct((n_pad * slabs, LANES), F32),
        grid_spec=grid_spec,
        compiler_params=_cparams(("arbitrary",)),
        name="moe_experts",
    )(row_tok, blk_e, n_used, h2, wg, wu, wd)
    return rows, dest


def _final_kernel(dest_ref, x1_ref, rows_hbm, route_ref, mod_ref, y_ref, rbuf, sem, *, tok_off, n_steps):
    i = pl.program_id(0)
    bb, t, d = x1_ref.shape
    tm = bb * t
    slabs = d // LANES
    slot = lax.rem(i, 2)

    def row_copy(pos, r, k, s):
        src = pl.multiple_of(pos * slabs, slabs)
        return pltpu.make_async_copy(rows_hbm.at[pl.ds(src, slabs)],
                                     rbuf.at[s, k, pl.ds(r * ROW_PITCH, slabs)], sem.at[s])

    def start_rows(step, s):
        for r in range(tm):
            for k in range(TOP_K):
                pos = dest_ref[(tok_off + step * tm + r) * TOP_K + k]
                row_copy(pos, r, k, s).start(priority=(r + k) % 2)

    def wait_rows(s):
        for r in range(tm):
            for k in range(TOP_K):
                row_copy(0, r, k, s).wait()

    @pl.when(i == 0)
    def _():
        start_rows(0, 0)

    @pl.when(i + 1 < n_steps)
    def _():
        start_rows(i + 1, 1 - slot)

    wait_rows(slot)
    w = route_ref[...]
    g_f = mod_ref[...][:, 5:6, :]
    for j in range(slabs):
        cols = slice(j * LANES, (j + 1) * LANES)
        moe = None
        for k in range(TOP_K):
            rk = rbuf[slot, k, pl.ds(j, tm, stride=ROW_PITCH), :].reshape(bb, t, LANES)
            term = w[:, :, TOP_K + k:TOP_K + k + 1] * rk
            moe = term if moe is None else moe + term
        y_ref[:, :, cols] = x1_ref[:, :, cols] + g_f[:, :, cols] * moe


def _final(x1, rows, dest, route, mod, batch, seq, row_off):
    n_all, d = x1.shape
    n = batch * seq
    bb, t = _row_tiling(batch, seq, FINAL_TILE)
    tm = bb * t
    tiles_per_mod = (seq // t) if bb == 1 else 1
    assert row_off % tm == 0 and n % tm == 0
    off = row_off // tm
    n_steps = n // tm
    grid_spec = pltpu.PrefetchScalarGridSpec(
        num_scalar_prefetch=1,
        grid=(n_steps,),
        in_specs=[pl.BlockSpec((bb, t, d), lambda i, dst: (i + off, 0, 0)),
                  pl.BlockSpec(memory_space=pl.ANY),
                  pl.BlockSpec((bb, t, LANES), lambda i, dst: (i + off, 0, 0)),
                  pl.BlockSpec((bb, 6, d), lambda i, dst: (i // tiles_per_mod, 0, 0))],
        out_specs=pl.BlockSpec((bb, t, d), lambda i, dst: (i, 0, 0)),
        scratch_shapes=[pltpu.VMEM((2, TOP_K, tm * ROW_PITCH, LANES), F32),
                        pltpu.SemaphoreType.DMA((2,))])
    return pl.pallas_call(
        functools.partial(_final_kernel, tok_off=row_off, n_steps=n_steps),
        out_shape=jax.ShapeDtypeStruct((n // t, t, d), F32),
        grid_spec=grid_spec,
        compiler_params=_cparams(("arbitrary",)),
        name="final_residual",
    )(dest.reshape(-1), x1.reshape(-1, t, d), rows, route.reshape(-1, t, LANES), mod).reshape(batch, seq, d)


def _mixer_group(x, mod, past_k, past_v, s0, lyr):
    batch, seq, d = x.shape
    q, f, i, g, qa, kf, kb, vf, vb = _in_proj(x, mod, lyr["g1"], lyr["w_in"], lyr["qg"], lyr["kg"],
                                               lyr["bd"])
    o_r, s_new = _hgrn(q, f, i, g, s0, lyr["lb"], lyr["og"], batch, seq)
    if past_k is None:
        o_a = _attention_prompt(qa, kb, vb, batch, seq, lyr["lam_params"], lyr["sg"], lyr["lam_init"])
    else:
        o_a = _attention_cached(qa, kb, vb, batch, seq, past_k, past_v, lyr["lam_params"], lyr["sg"],
                                lyr["lam_init"])
    k_new = kf.reshape(batch, seq, N_HEADS, HEAD_DIM)
    v_new = vf.reshape(batch, seq, N_HEADS, HEAD_DIM)
    return o_r, o_a, k_new, v_new, s_new


def kernel(x_prompt, x_sample, cache_k_attn, cache_v_attn, state_hgrn, c_prompt, c_sample, w_ada, b_ada, norm1_g, norm2_g, w_in, w_out, hgrn_lb_param, hgrn_onorm_g, attn_qnorm_g, attn_knorm_g, lambda_q1, lambda_k1, lambda_q2, lambda_k2, attn_subln_g, router_group_w, router_group_b, router_expert_w, router_expert_b, expert_w_gate, expert_w_up, expert_w_down):
    depth = w_ada.shape[0]
    bp, tp, d = x_prompt.shape
    bs, ts, _ = x_sample.shape
    d_h = N_HEADS * HEAD_DIM
    lb_all = jnp.cumsum(jax.nn.softmax(hgrn_lb_param.astype(F32), axis=0), axis=0)
    bd = jnp.asarray(np.kron(np.eye(LANES // D_QK), np.ones((D_QK, D_QK))), BF16)

    xp, xs = x_prompt, x_sample
    kp, vp, sp, ksm, vsm, ssm = [], [], [], [], [], []
    for l in range(depth):
        wl = w_in[l]
        qk = wl[:, 4 * d_h:4 * d_h + 4 * N_HEADS * D_QK].reshape(d, 2, 2, N_HEADS, D_QK)
        qk = qk.transpose(0, 1, 3, 2, 4).reshape(d, 4 * N_HEADS * D_QK)
        w_in_b = jnp.concatenate([wl[:, :4 * d_h], qk, wl[:, 4 * d_h + 4 * N_HEADS * D_QK:]],
                                 axis=1).astype(BF16)
        rw = jnp.zeros((d, LANES), F32)
        rw = rw.at[:, :N_EXPERTS].set(router_expert_w[l])
        rw = rw.at[:, N_EXPERTS:N_EXPERTS + N_GROUPS].set(router_group_w[l])
        rb = jnp.zeros((1, LANES), F32)
        rb = rb.at[0, :N_EXPERTS].set(router_expert_b[l])
        rb = rb.at[0, N_EXPERTS:N_EXPERTS + N_GROUPS].set(router_group_b[l])
        lyr = dict(
            g1=norm1_g[l].reshape(1, d), g2=norm2_g[l].reshape(1, d),
            w_in=w_in_b, w_out=w_out[l].astype(BF16),
            qg=jnp.tile(attn_qnorm_g[l], 2).reshape(1, LANES),
            kg=jnp.tile(attn_knorm_g[l], 2).reshape(1, LANES),
            bd=bd, lb=lb_all[l].reshape(1, d_h), og=hgrn_onorm_g[l].reshape(1, HEAD_DIM),
            sg=attn_subln_g[l].reshape(1, HEAD_DIM),
            lam_params=[p[l].reshape(1, D_QK) for p in (lambda_q1, lambda_k1, lambda_q2, lambda_k2)],
            lam_init=0.8 - 0.6 * math.exp(-0.3 * l),
            rw=rw, rb=rb)

        c_all = jnp.concatenate([c_prompt, c_sample], axis=0)
        rows = c_all.shape[0]
        rows_pad = -(-rows // 8) * 8
        mod = _ada(jnp.pad(c_all, ((0, rows_pad - rows), (0, 0))), w_ada[l], b_ada[l])
        mod_p = mod[:bp].reshape(bp, 6, d)
        mod_s = mod[bp:bp + bs].reshape(bs, 6, d)

        s0_p = jnp.zeros((bp, N_HEADS, HEAD_DIM, HEAD_DIM), F32)
        o_rp, o_ap, k_n, v_n, s_n = _mixer_group(xp, mod_p, None, None, s0_p, lyr)
        kp.append(k_n)
        vp.append(v_n)
        sp.append(s_n)
        o_rs, o_as, k_n, v_n, s_n = _mixer_group(xs, mod_s, cache_k_attn[l], cache_v_attn[l],
                                                 state_hgrn[l], lyr)
        ksm.append(k_n)
        vsm.append(v_n)
        ssm.append(s_n)

        x1, h2, route, counts = _out_proj(o_rp, o_ap, xp, mod_p, o_rs, o_as, xs, mod_s,
                                          lyr["g2"], lyr["w_out"], lyr["rw"], lyr["rb"])
        rows, dest = _moe(h2, route, counts, expert_w_gate[l], expert_w_up[l], expert_w_down[l])
        xp = _final(x1, rows, dest, route, mod_p, bp, tp, 0)
        xs = _final(x1, rows, dest, route, mod_s, bs, ts, bp * tp)
    return (xp, xs, jnp.stack(kp), jnp.stack(vp), jnp.stack(sp),
            jnp.stack(ksm), jnp.stack(vsm), jnp.stack(ssm))
```

```python
import functools
import math

import numpy as np
import jax
import jax.numpy as jnp
from jax import lax
from jax.experimental import pallas as pl
from jax.experimental.pallas import tpu as pltpu

F32 = jnp.float32
BF16 = jnp.bfloat16

N_HEADS = 8
HEAD_DIM = 128
D_QK = 64
CHUNK = 64
N_GROUPS = 4
EXPERTS_PER_GROUP = 8
N_EXPERTS = N_GROUPS * EXPERTS_PER_GROUP
TOP_K = 2
EPS = 1e-6
NEG_BIG = -1e30
LOG2E = math.log2(math.e)

LANES = 128
MXU_COLS = 256
ROW_TILE = 512
ATTN_TILE = 512
ATTN_ROWS = 64
SAMPLE_KEY_TILE = 2048
MOE_ROWS = 256
FINAL_TILE = 256
VMEM_LIMIT = 56 * 1024 * 1024

HGRN_LEVELS = (32, 16, 8, 4, 2, 1)
HGRN_CHUNKS_PER_STEP = 8


def _cparams(sem, vmem=VMEM_LIMIT):
    return pltpu.CompilerParams(dimension_semantics=sem, vmem_limit_bytes=vmem)


def _dot(a, b):
    return jnp.dot(a, b, preferred_element_type=F32)


def _dot_nt(a, b):
    return lax.dot_general(a, b, (((1,), (1,)), ((), ())), preferred_element_type=F32)


def _dot_tn(a, b):
    return lax.dot_general(a, b, (((0,), (0,)), ((), ())), preferred_element_type=F32)


def _silu(x):
    return x / (1.0 + jnp.exp(-x))


def _row_tiling(batch, seq, target):
    if seq >= target:
        assert seq % target == 0
        return 1, target
    bb = max(1, min(batch, target // seq))
    while batch % bb:
        bb -= 1
    return bb, seq


def _ada_kernel(c_ref, w_ref, b_ref, o_ref):
    s = _silu(c_ref[...]).astype(BF16)
    o_ref[...] = _dot(s, w_ref[...].astype(BF16)) + b_ref[...]


def _ada(c_all, w_ada, b_ada):
    rows, d = c_all.shape
    n_out = w_ada.shape[1]
    tn = 1024
    return pl.pallas_call(
        _ada_kernel,
        out_shape=jax.ShapeDtypeStruct((rows, n_out), F32),
        grid=(n_out // tn,),
        in_specs=[pl.BlockSpec((rows, d), lambda j: (0, 0)),
                  pl.BlockSpec((d, tn), lambda j: (0, j)),
                  pl.BlockSpec((1, tn), lambda j: (0, j))],
        out_specs=pl.BlockSpec((rows, tn), lambda j: (0, j)),
        compiler_params=_cparams(("arbitrary",)),
        name="ada_mod",
    )(c_all, w_ada, b_ada.reshape(1, n_out))


def _w_prep_kernel(w_ref, o_ref, *, qk_start):
    half = N_HEADS * D_QK
    o_ref[:, 0:qk_start] = w_ref[:, 0:qk_start].astype(BF16)
    o_ref[:, qk_start + 4 * half:] = w_ref[:, qk_start + 4 * half:].astype(BF16)
    for sec in range(2):
        base = qk_start + sec * 2 * half
        for h in range(N_HEADS):
            m1 = w_ref[:, base + h * D_QK:base + (h + 1) * D_QK]
            m2 = w_ref[:, base + half + h * D_QK:base + half + (h + 1) * D_QK]
            o_ref[:, base + h * LANES:base + (h + 1) * LANES] = jnp.concatenate([m1, m2], axis=1).astype(BF16)


def _prep_w_in(w, qk_start):
    d, n_out = w.shape
    tr = 256
    return pl.pallas_call(
        functools.partial(_w_prep_kernel, qk_start=qk_start),
        out_shape=jax.ShapeDtypeStruct((d, n_out), BF16),
        grid=(d // tr,),
        in_specs=[pl.BlockSpec((tr, n_out), lambda i: (i, 0))],
        out_specs=pl.BlockSpec((tr, n_out), lambda i: (i, 0)),
        compiler_params=_cparams(("arbitrary",)),
        name="w_in_prep",
    )(w)


def _group_norm64(acc, bd, gain):
    sq = (acc * acc).astype(BF16)
    parts = []
    for t in range(acc.shape[1] // LANES):
        ss = _dot(sq[:, t * LANES:(t + 1) * LANES], bd)
        a = acc[:, t * LANES:(t + 1) * LANES]
        parts.append(a * lax.rsqrt(ss * (1.0 / D_QK) + EPS) * gain)
    return jnp.concatenate(parts, axis=1)


def _in_kernel(x_ref, mod_ref, g1_ref, w_ref, qg_ref, kg_ref, bd_ref,
               q_ref, f_ref, i_ref, g_ref, qa_ref, kf_ref, kb_ref, vf_ref, vb_ref,
               h_scr):
    j = pl.program_id(1)
    tm, d = h_scr.shape

    @pl.when(j == 0)
    def _():
        x = x_ref[...]
        ms = jnp.mean(x * x, axis=-1, keepdims=True)
        xn = x * lax.rsqrt(ms + EPS) * g1_ref[...]
        m = mod_ref[...]
        h = xn * (1.0 + m[:, 1:2, :]) + m[:, 0:1, :]
        h_scr[...] = h.reshape(tm, d).astype(BF16)

    def section(k, epilogue, chunked=True):
        @pl.when(j == k)
        def _():
            width = w_ref.shape[1]
            if not chunked:
                epilogue(_dot(h_scr[...], w_ref[...]), slice(0, width))
                return
            for c in range(width // MXU_COLS):
                cols = slice(c * MXU_COLS, (c + 1) * MXU_COLS)
                epilogue(_dot(h_scr[...], w_ref[:, cols]), cols)

    def put(ref, fn):
        def epilogue(acc, cols):
            ref[:, cols] = fn(acc)
        return epilogue

    def put_k(acc, cols):
        kn = _group_norm64(acc, bd_ref[...], kg_ref[...])
        kf_ref[:, cols] = kn
        kb_ref[:, cols] = kn.astype(BF16)

    def put_v(acc, cols):
        vf_ref[:, cols] = acc
        vb_ref[:, cols] = acc.astype(BF16)

    section(0, put(q_ref, lambda a: _silu(a).astype(BF16)))
    section(1, put(f_ref, lambda a: a))
    section(2, put(i_ref, lambda a: a.astype(BF16)))
    section(3, put(g_ref, lambda a: _silu(a).astype(BF16)))
    section(4, put(qa_ref, lambda a: (_group_norm64(a, bd_ref[...], qg_ref[...])
                                      * (LOG2E * D_QK ** -0.5)).astype(BF16)), chunked=False)
    section(5, put_k, chunked=False)
    section(6, put_v)


def _in_proj(x, mod, g1, w_in_b, qg, kg, bd):
    batch, seq, d = x.shape
    n = batch * seq
    bb, t = _row_tiling(batch, seq, ROW_TILE)
    tm = bb * t
    tiles_per_mod = (seq // t) if bb == 1 else 1
    sec = 1024
    n_sec = w_in_b.shape[1] // sec
    xv = x.reshape(n // t, t, d)
    row = lambda i, j: (i, 0)
    out_dt = [BF16, F32, BF16, BF16, BF16, F32, BF16, F32, BF16]
    return pl.pallas_call(
        _in_kernel,
        out_shape=[jax.ShapeDtypeStruct((n, sec), dt) for dt in out_dt],
        grid=(n // tm, n_sec),
        in_specs=[pl.BlockSpec((bb, t, d), lambda i, j: (i, 0, 0)),
                  pl.BlockSpec((bb, 6, d), lambda i, j: (i // tiles_per_mod, 0, 0)),
                  pl.BlockSpec((1, d), lambda i, j: (0, 0)),
                  pl.BlockSpec((d, sec), lambda i, j: (0, j)),
                  pl.BlockSpec((1, LANES), lambda i, j: (0, 0)),
                  pl.BlockSpec((1, LANES), lambda i, j: (0, 0)),
                  pl.BlockSpec((LANES, LANES), lambda i, j: (0, 0))],
        out_specs=[pl.BlockSpec((tm, sec), row) for _ in out_dt],
        scratch_shapes=[pltpu.VMEM((tm, d), BF16)],
        compiler_params=_cparams(("arbitrary", "arbitrary")),
        name="in_proj",
    )(xv, mod, g1, w_in_b, qg, kg, bd)


def _hgrn_consts():
    c = CHUNK
    t = np.arange(c)[:, None]
    s = np.arange(c)[None, :]
    sums = [(s <= t), (s > t)]
    masks = []
    for m in HGRN_LEVELS:
        start = (t // (2 * m)) * (2 * m)
        ref = start + m - 1
        lower = (t % (2 * m)) >= m
        sums.append(np.where(lower, (s > ref) & (s <= t), (s > t) & (s <= ref)))
        s_start = (s // (2 * m)) * (2 * m)
        masks.append((s_start == start) & lower & ((s % (2 * m)) < m))
    masks.append(s == t)
    sum_mat = np.concatenate(sums, axis=0).astype(np.float32)
    sum_cat = np.concatenate([sum_mat, sum_mat], axis=1)
    mask_mat = np.concatenate(masks, axis=0).astype(np.float32)
    return jnp.asarray(sum_cat, BF16), jnp.asarray(mask_mat, F32)


def _hgrn_kernel(q_ref, f_ref, i_ref, g_ref, s0_ref, lb_ref, og_ref, sum_ref, mask_ref,
                 o_ref, sout_ref, st_scr, *, n_steps):
    ci = pl.program_id(1)

    @pl.when(ci == 0)
    def _():
        for h in range(N_HEADS):
            st_scr[h] = s0_ref[0, h].T

    for cc in range(f_ref.shape[0] // CHUNK):
        _hgrn_chunk(slice(cc * CHUNK, (cc + 1) * CHUNK), q_ref, f_ref, i_ref, g_ref, lb_ref, og_ref,
                    sum_ref, mask_ref, o_ref, st_scr)

    @pl.when(ci == n_steps - 1)
    def _():
        for h in range(N_HEADS):
            sout_ref[0, h] = st_scr[h].T


def _hgrn_chunk(rows, q_ref, f_ref, i_ref, g_ref, lb_ref, og_ref, sum_ref, mask_ref, o_ref, st_scr):
    c = CHUNK
    hd = HEAD_DIM
    f = f_ref[rows, :]
    lb = lb_ref[...]
    e = jnp.exp(-jnp.abs(f))
    r = 1.0 / (1.0 + e)
    pos = f >= 0.0
    sig = jnp.where(pos, r, e * r)
    nsig = jnp.where(pos, e * r, r)
    logf = jnp.log(lb + (1.0 - lb) * sig)
    kin = (1.0 - lb) * nsig

    hi = logf.astype(BF16)
    lo = (logf - hi.astype(F32)).astype(BF16)
    expo = _dot(sum_ref[...], jnp.concatenate([hi, lo], axis=0))
    dec = jnp.exp(expo)

    masks = mask_ref[...]
    og = og_ref[...]
    n_lv = len(HGRN_LEVELS)
    for h in range(N_HEADS):
        cs = slice(h * hd, (h + 1) * hd)
        q = q_ref[rows, cs].astype(F32)
        k = kin[:, cs]
        v = i_ref[rows, cs]
        d0 = dec[0:c, cs]
        d1 = dec[c:2 * c, cs]
        st = st_scr[h]
        o = _dot_nt((q * d0).astype(BF16), st.astype(BF16))
        a = _dot_nt(q.astype(BF16), k.astype(BF16)) * masks[n_lv * c:(n_lv + 1) * c]
        for lv in range(n_lv):
            dl = dec[(2 + lv) * c:(3 + lv) * c, cs]
            a = a + _dot_nt((q * dl).astype(BF16), (k * dl).astype(BF16)) * masks[lv * c:(lv + 1) * c]
        o = o + _dot(a.astype(BF16), v)
        st_scr[h] = st * d0[c - 1:c, :] + _dot_tn(v, (k * d1).astype(BF16))
        ms = jnp.mean(o * o, axis=-1, keepdims=True)
        o = o * lax.rsqrt(ms + EPS) * og * g_ref[rows, cs].astype(F32)
        o_ref[rows, cs] = o.astype(BF16)


def _hgrn(q, f, i, g, s0, lb, og, batch, seq):
    n, w = q.shape
    cps = HGRN_CHUNKS_PER_STEP if (seq // CHUNK) % HGRN_CHUNKS_PER_STEP == 0 else 1
    rows = cps * CHUNK
    nc = seq // rows
    sum_cat, mask_mat = _hgrn_consts()
    row = lambda b, c: (b * nc + c, 0)
    const = lambda b, c: (0, 0)
    return pl.pallas_call(
        functools.partial(_hgrn_kernel, n_steps=nc),
        out_shape=[jax.ShapeDtypeStruct((n, w), BF16),
                   jax.ShapeDtypeStruct(s0.shape, F32)],
        grid=(batch, nc),
        in_specs=[pl.BlockSpec((rows, w), row),
                  pl.BlockSpec((rows, w), row),
                  pl.BlockSpec((rows, w), row),
                  pl.BlockSpec((rows, w), row),
                  pl.BlockSpec((1,) + s0.shape[1:], lambda b, c: (b, 0, 0, 0)),
                  pl.BlockSpec((1, w), const),
                  pl.BlockSpec((1, HEAD_DIM), const),
                  pl.BlockSpec(sum_cat.shape, const),
                  pl.BlockSpec(mask_mat.shape, const)],
        out_specs=[pl.BlockSpec((rows, w), row),
                   pl.BlockSpec((1,) + s0.shape[1:], lambda b, c: (b, 0, 0, 0))],
        scratch_shapes=[pltpu.VMEM((N_HEADS, HEAD_DIM, HEAD_DIM), F32)],
        compiler_params=_cparams(("arbitrary", "arbitrary")),
        name="hgrn2",
    )(q, f, i, g, s0, lb, og, sum_cat, mask_mat)


def _stack_maps(qt):
    lane = lax.broadcasted_iota(jnp.int32, qt.shape, 1)
    zero = jnp.zeros_like(qt)
    return jnp.concatenate([jnp.where(lane < D_QK, qt, zero), jnp.where(lane >= D_QK, qt, zero)], axis=0)


def _lane_tile(x, width):
    if width < LANES:
        return x[:, :width]
    return jnp.concatenate([x] * (width // LANES), axis=1)


def _softmax_update(s, v, m_ref, l_ref, acc_ref, rows):
    m_prev = m_ref[rows, :]
    m_next = jnp.maximum(m_prev, jnp.max(s, axis=1, keepdims=True))
    alpha = jnp.exp2(m_prev - m_next)
    p = jnp.exp2(s - _lane_tile(m_next, s.shape[1]))
    l_ref[rows, :] = alpha * l_ref[rows, :] + jnp.sum(p, axis=1, keepdims=True)
    acc_ref[rows, :] = alpha * acc_ref[rows, :] + _dot(p.astype(BF16), v)
    m_ref[rows, :] = m_next


def _diag_bias(slope, r0, rows, tq):
    r = r0 + lax.broadcasted_iota(jnp.int32, (rows, tq), 0)
    c = lax.broadcasted_iota(jnp.int32, (rows, tq), 1)
    allowed = (r // CHUNK) >= (c // CHUNK)
    return allowed, slope * (r - jnp.abs(r - c)).astype(F32)


def _lambda(lq1_ref, lk1_ref, lq2_ref, lk2_ref, lam_init):
    return (jnp.exp(jnp.sum(lq1_ref[...] * lk1_ref[...], axis=1, keepdims=True))
            - jnp.exp(jnp.sum(lq2_ref[...] * lk2_ref[...], axis=1, keepdims=True)) + lam_init)


def _attn_finish(acc_ref, l_ref, row0, tq, lam, sg, lam_init):
    a1 = acc_ref[row0:row0 + tq, :] / l_ref[row0:row0 + tq, :]
    a2 = acc_ref[row0 + tq:row0 + 2 * tq, :] / l_ref[row0 + tq:row0 + 2 * tq, :]
    o = a1 - lam * a2
    ms = jnp.mean(o * o, axis=-1, keepdims=True)
    return (o * lax.rsqrt(ms + EPS) * sg * (1.0 - lam_init)).astype(BF16)


def _attn_prompt_kernel(q_ref, k_ref, v_ref, slope_ref, lq1_ref, lk1_ref, lq2_ref, lk2_ref, sg_ref,
                        o_ref, qs, s_a, s_b, p_a, p_b, vx_a, vx_b, al_a, al_b, m_scr, l_scr, acc_scr,
                        *, tq, lam_init):
    qi = pl.program_id(2)
    slope = slope_ref[0, :, 0:1]
    rb = ATTN_ROWS
    qs[...] = _stack_maps(q_ref[...])
    bufs = ((s_a, p_a, vx_a, al_a), (s_b, p_b, vx_b, al_b))
    for _, _, vx, _ in bufs:
        vx[:, HEAD_DIM:] = jnp.ones((tq, LANES), BF16)
    m_scr[...] = jnp.full(m_scr.shape, NEG_BIG, F32)
    l_scr[...] = jnp.zeros(l_scr.shape, F32)
    acc_scr[...] = jnp.zeros(acc_scr.shape, F32)

    def scores(kj):
        return _dot_nt(qs[...], k_ref[pl.ds(kj * tq, tq), :])

    def update(buf, kj, diag):
        s_ref, p_scr, vx, al_scr = bufs[buf]
        if not diag:
            col = lax.broadcasted_iota(jnp.int32, (1, tq), 1)
            bias = slope * (col + (kj - qi) * tq).astype(F32)
        def biased(r0):
            rows = slice(r0, r0 + rb)
            if diag:
                allowed, bias_d = _diag_bias(slope, r0 % tq, rb, tq)
                return jnp.where(allowed, s_ref[rows, :] + bias_d, NEG_BIG)
            return s_ref[rows, :] + bias

        for r0 in range(0, 2 * tq, rb):
            rows = slice(r0, r0 + rb)
            m_prev = m_scr[rows, :]
            m_next = jnp.maximum(m_prev, jnp.max(biased(r0), axis=1, keepdims=True))
            al_scr[rows, :] = jnp.exp2(m_prev - m_next)
            m_scr[rows, :] = m_next
        for r0 in range(0, 2 * tq, rb):
            rows = slice(r0, r0 + rb)
            p_scr[rows, :] = jnp.exp2(biased(r0) - _lane_tile(m_scr[rows, :], tq)).astype(BF16)
        vx[:, 0:HEAD_DIM] = v_ref[pl.ds(kj * tq, tq), :]
        pv = _dot(p_scr[...], vx[...])
        alpha = al_scr[...]
        acc_scr[...] = alpha * acc_scr[...] + pv[:, 0:HEAD_DIM]
        l_scr[...] = alpha * l_scr[...] + pv[:, HEAD_DIM:]

    s_a[...] = scores(0)

    def pair_step(pj, carry):
        kj = 2 * pj
        s_b[...] = scores(kj + 1)
        update(0, kj, False)
        s_a[...] = scores(kj + 2)
        update(1, kj + 1, False)
        return carry

    lax.fori_loop(0, lax.shift_right_logical(qi, 1), pair_step, 0)
    odd = (qi & 1) == 1

    @pl.when(jnp.logical_not(odd))
    def _():
        update(0, qi, True)

    @pl.when(odd)
    def _():
        s_b[...] = scores(qi)
        update(0, qi - 1, False)
        update(1, qi, True)

    lam = _lambda(lq1_ref, lk1_ref, lq2_ref, lk2_ref, lam_init)
    o_ref[...] = _attn_finish(acc_scr, l_scr, 0, tq, lam, sg_ref[...], lam_init)


def _attn_cached_kernel(q_ref, kn_ref, vn_ref, pk_ref, pv_ref, slope_ref,
                        lq1_ref, lk1_ref, lq2_ref, lk2_ref, sg_ref,
                        o_ref, qs, m_scr, l_scr, acc_scr, *, tq, tk, p_len, lam_init):
    hd = HEAD_DIM
    for h in range(N_HEADS):
        qs[2 * tq * h:2 * tq * (h + 1), :] = _stack_maps(q_ref[:, h * hd:(h + 1) * hd])
    m_scr[...] = jnp.full(m_scr.shape, NEG_BIG, F32)
    l_scr[...] = jnp.zeros(l_scr.shape, F32)
    acc_scr[...] = jnp.zeros(acc_scr.shape, F32)
    col = lax.broadcasted_iota(jnp.int32, (1, tk), 1)

    def past_step(kj, carry):
        pos = (col + (kj * tk - p_len)).astype(F32)
        for h in range(N_HEADS):
            rows = slice(2 * tq * h, 2 * tq * (h + 1))
            k = pk_ref[0, pl.ds(kj * (tk * N_HEADS) + h, tk, stride=N_HEADS), :].astype(BF16)
            v = pv_ref[0, pl.ds(kj * (tk * N_HEADS) + h, tk, stride=N_HEADS), :].astype(BF16)
            s = _dot_nt(qs[rows, :], k) + slope_ref[h, :, 0:1] * pos
            _softmax_update(s, v, m_scr, l_scr, acc_scr, rows)
        return carry

    lax.fori_loop(0, p_len // tk, past_step, 0)

    lam = _lambda(lq1_ref, lk1_ref, lq2_ref, lk2_ref, lam_init)
    for h in range(N_HEADS):
        kd = kn_ref[:, h * hd:(h + 1) * hd]
        vd = vn_ref[:, h * hd:(h + 1) * hd]
        allowed, bias = _diag_bias(slope_ref[h, :, 0:1], 0, tq, tq)
        for half in range(2):
            rows = slice(2 * tq * h + half * tq, 2 * tq * h + (half + 1) * tq)
            s = jnp.where(allowed, _dot_nt(qs[rows, :], kd) + bias, NEG_BIG)
            _softmax_update(s, vd, m_scr, l_scr, acc_scr, rows)
        o_ref[:, h * hd:(h + 1) * hd] = _attn_finish(acc_scr, l_scr, 2 * tq * h, tq, lam, sg_ref[...],
                                                     lam_init)


def _alibi_slopes():
    s = (2.0 ** (-8.0 * np.arange(1, N_HEADS + 1) / N_HEADS)) * LOG2E
    return jnp.asarray(np.broadcast_to(s[:, None, None], (N_HEADS, 1, LANES)), F32)


def _attention_prompt(qa, kb, vb, batch, seq, lam_params, subln_g, lam_init):
    n, w = qa.shape
    tq = min(ATTN_TILE, seq)
    nq = seq // tq
    assert seq % tq == 0 and tq % CHUNK == 0
    small = lambda b, h, qi: (0, 0)
    q_spec = pl.BlockSpec((tq, HEAD_DIM), lambda b, h, qi: (b * nq + qi, h))
    kv_spec = pl.BlockSpec((seq, HEAD_DIM), lambda b, h, qi: (b, h))
    return pl.pallas_call(
        functools.partial(_attn_prompt_kernel, tq=tq, lam_init=lam_init),
        out_shape=jax.ShapeDtypeStruct((n, w), BF16),
        grid=(batch, N_HEADS, nq),
        in_specs=[q_spec, kv_spec, kv_spec,
                  pl.BlockSpec((1, 1, LANES), lambda b, h, qi: (h, 0, 0)),
                  pl.BlockSpec((1, D_QK), small), pl.BlockSpec((1, D_QK), small),
                  pl.BlockSpec((1, D_QK), small), pl.BlockSpec((1, D_QK), small),
                  pl.BlockSpec((1, HEAD_DIM), small)],
        out_specs=q_spec,
        scratch_shapes=[pltpu.VMEM((2 * tq, HEAD_DIM), BF16),
                        pltpu.VMEM((2 * tq, tq), F32), pltpu.VMEM((2 * tq, tq), F32),
                        pltpu.VMEM((2 * tq, tq), BF16), pltpu.VMEM((2 * tq, tq), BF16),
                        pltpu.VMEM((tq, HEAD_DIM + LANES), BF16), pltpu.VMEM((tq, HEAD_DIM + LANES), BF16),
                        pltpu.VMEM((2 * tq, LANES), F32), pltpu.VMEM((2 * tq, LANES), F32),
                        pltpu.VMEM((2 * tq, LANES), F32), pltpu.VMEM((2 * tq, LANES), F32),
                        pltpu.VMEM((2 * tq, HEAD_DIM), F32)],
        compiler_params=_cparams(("arbitrary", "arbitrary", "arbitrary")),
        name="diff_attn",
    )(qa, kb, vb, _alibi_slopes(), *lam_params, subln_g)


def _attention_cached(qa, kb, vb, batch, seq, past_k, past_v, lam_params, subln_g, lam_init):
    n, w = qa.shape
    p_len = past_k.shape[1]
    tk = min(SAMPLE_KEY_TILE, p_len)
    assert p_len % tk == 0 and p_len % CHUNK == 0 and seq == CHUNK
    rows = 2 * seq * N_HEADS
    small = lambda b: (0, 0)
    tok_spec = pl.BlockSpec((seq, w), lambda b: (b, 0))
    past_spec = pl.BlockSpec((1, p_len * N_HEADS, HEAD_DIM), lambda b: (b, 0, 0))
    return pl.pallas_call(
        functools.partial(_attn_cached_kernel, tq=seq, tk=tk, p_len=p_len, lam_init=lam_init),
        out_shape=jax.ShapeDtypeStruct((n, w), BF16),
        grid=(batch,),
        in_specs=[tok_spec, tok_spec, tok_spec, past_spec, past_spec,
                  pl.BlockSpec((N_HEADS, 1, LANES), lambda b: (0, 0, 0)),
                  pl.BlockSpec((1, D_QK), small), pl.BlockSpec((1, D_QK), small),
                  pl.BlockSpec((1, D_QK), small), pl.BlockSpec((1, D_QK), small),
                  pl.BlockSpec((1, HEAD_DIM), small)],
        out_specs=tok_spec,
        scratch_shapes=[pltpu.VMEM((rows, HEAD_DIM), BF16), pltpu.VMEM((rows, LANES), F32),
                        pltpu.VMEM((rows, LANES), F32), pltpu.VMEM((rows, HEAD_DIM), F32)],
        compiler_params=_cparams(("arbitrary",)),
        name="diff_attn_cached",
    )(qa, kb, vb, past_k.reshape(batch, p_len * N_HEADS, HEAD_DIM),
      past_v.reshape(batch, p_len * N_HEADS, HEAD_DIM), _alibi_slopes(), *lam_params, subln_g)


def _split_bf16(x):
    hi = x.astype(BF16)
    lo = (x - hi.astype(F32)).astype(BF16)
    return hi, lo


def _out_kernel(orp_ref, oap_ref, xp_ref, modp_ref, ors_ref, oas_ref, xs_ref, mods_ref,
                g2_ref, w_ref, rw_ref, rb_ref, tri_ref,
                x1_ref, h2_ref, route_ref, cnt_ref, cnt_scr, *, n_tiles_p):
    i = pl.program_id(0)

    @pl.when(i == 0)
    def _():
        cnt_scr[...] = jnp.zeros_like(cnt_scr)

    args = (g2_ref, w_ref, rw_ref, rb_ref, tri_ref, x1_ref, h2_ref, route_ref, cnt_scr)

    @pl.when(i < n_tiles_p)
    def _():
        _out_tile(orp_ref, oap_ref, xp_ref, modp_ref, *args)

    @pl.when(i >= n_tiles_p)
    def _():
        _out_tile(ors_ref, oas_ref, xs_ref, mods_ref, *args)

    cnt_ref[...] = jnp.broadcast_to(cnt_scr[...], cnt_ref.shape)


def _out_tile(or_ref, oa_ref, x_ref, mod_ref, g2_ref, w_ref, rw_ref, rb_ref, tri_ref,
              x1_ref, h2_ref, route_ref, cnt_scr):
    bb, t, d = x_ref.shape
    tm = bb * t
    half = or_ref.shape[1]
    mix = _dot(or_ref[...], w_ref[0:half, :]) + _dot(oa_ref[...], w_ref[half:, :])
    m = mod_ref[...]
    x1 = x_ref[...] + m[:, 2:3, :] * mix.reshape(bb, t, d)
    ms = jnp.mean(x1 * x1, axis=-1, keepdims=True)
    h2 = x1 * lax.rsqrt(ms + EPS) * g2_ref[...] * (1.0 + m[:, 4:5, :]) + m[:, 3:4, :]
    x1_ref[...] = x1.reshape(tm, d)
    h2 = h2.reshape(tm, d)
    slabs = d // LANES
    for j in range(slabs):
        h2_ref[pl.ds(j, tm, stride=slabs), :] = h2[:, j * LANES:(j + 1) * LANES]

    hh, hl = _split_bf16(h2)
    wh, wl = _split_bf16(rw_ref[...])
    logits = _dot(hh, wh) + _dot(hl, wh) + _dot(hh, wl) + rb_ref[...]
    lane = lax.broadcasted_iota(jnp.int32, logits.shape, 1)
    neg = jnp.float32(-jnp.inf)
    big = jnp.int32(1 << 20)

    is_g = (lane >= N_EXPERTS) & (lane < N_EXPERTS + N_GROUPS)
    gl = jnp.where(is_g, logits, neg)
    gmax = jnp.max(gl, axis=1, keepdims=True)
    gidx = jnp.min(jnp.where(gl == gmax, lane - N_EXPERTS, big), axis=1, keepdims=True)
    g_prob = 1.0 / jnp.sum(jnp.exp(gl - gmax), axis=1, keepdims=True)

    in_grp = (lane < N_EXPERTS) & ((lane // EXPERTS_PER_GROUP) == gidx)
    el = jnp.where(in_grp, logits, neg)
    m1 = jnp.max(el, axis=1, keepdims=True)
    i1 = jnp.min(jnp.where(el == m1, lane, big), axis=1, keepdims=True)
    el2 = jnp.where(lane == i1, neg, el)
    m2 = jnp.max(el2, axis=1, keepdims=True)
    i2 = jnp.min(jnp.where(el2 == m2, lane, big), axis=1, keepdims=True)
    e21 = jnp.exp(m2 - m1)
    w1 = g_prob / (1.0 + e21)
    w2 = g_prob * e21 / (1.0 + e21)
    tri = tri_ref[...]
    oh1 = jnp.where(lane == i1, 1.0, 0.0)
    oh2 = jnp.where(lane == i2, 1.0, 0.0)
    cnt = cnt_scr[...]
    c1 = jnp.sum(oh1, axis=0, keepdims=True)
    c2 = jnp.sum(oh2, axis=0, keepdims=True)
    r1 = jnp.sum(oh1 * (_dot(tri, oh1.astype(BF16)) + cnt), axis=1, keepdims=True)
    r2 = jnp.sum(oh2 * (_dot(tri, oh2.astype(BF16)) + cnt + c1), axis=1, keepdims=True)
    cnt_scr[...] = cnt + c1 + c2

    vals = (i1.astype(F32), i2.astype(F32), w1, w2, r1, r2)
    route = jnp.zeros(logits.shape, F32)
    for k, v in enumerate(vals):
        route = jnp.where(lane == k, v, route)
    route_ref[...] = route


def _out_proj(o_rp, o_ap, xp, mod_p, o_rs, o_as, xs, mod_s, g2, w_out_b, rw, rb):
    d = xp.shape[2]
    half = o_rp.shape[1]
    n_p = xp.shape[0] * xp.shape[1]
    n_s = xs.shape[0] * xs.shape[1]
    n = n_p + n_s
    bbp, tp = _row_tiling(xp.shape[0], xp.shape[1], ROW_TILE)
    bbs, ts = _row_tiling(xs.shape[0], xs.shape[1], ROW_TILE)
    tm = bbp * tp
    assert bbs * ts == tm and n_p % tm == 0 and n_s % tm == 0
    ntp = n_p // tm
    nts = n_s // tm
    modp_tiles = (xp.shape[1] // tp) if bbp == 1 else 1
    mods_tiles = (xs.shape[1] // ts) if bbs == 1 else 1
    pi = lambda i: jnp.minimum(i, ntp - 1)
    si = lambda i: jnp.maximum(i - ntp, 0)
    tri = jnp.asarray(np.tril(np.ones((tm, tm), np.float32), -1), BF16)
    row = lambda i: (i, 0)
    const = lambda i: (0, 0)
    return pl.pallas_call(
        functools.partial(_out_kernel, n_tiles_p=ntp),
        out_shape=[jax.ShapeDtypeStruct((n, d), F32), jax.ShapeDtypeStruct((n * (d // LANES), LANES), F32),
                   jax.ShapeDtypeStruct((n, LANES), F32), jax.ShapeDtypeStruct((8, LANES), F32)],
        grid=(ntp + nts,),
        in_specs=[pl.BlockSpec((tm, half), lambda i: (pi(i), 0)),
                  pl.BlockSpec((tm, half), lambda i: (pi(i), 0)),
                  pl.BlockSpec((bbp, tp, d), lambda i: (pi(i), 0, 0)),
                  pl.BlockSpec((bbp, 6, d), lambda i: (pi(i) // modp_tiles, 0, 0)),
                  pl.BlockSpec((tm, half), lambda i: (si(i), 0)),
                  pl.BlockSpec((tm, half), lambda i: (si(i), 0)),
                  pl.BlockSpec((bbs, ts, d), lambda i: (si(i), 0, 0)),
                  pl.BlockSpec((bbs, 6, d), lambda i: (si(i) // mods_tiles, 0, 0)),
                  pl.BlockSpec((1, d), const),
                  pl.BlockSpec(w_out_b.shape, const),
                  pl.BlockSpec(rw.shape, const),
                  pl.BlockSpec((1, LANES), const),
                  pl.BlockSpec((tm, tm), const)],
        out_specs=[pl.BlockSpec((tm, d), row), pl.BlockSpec((tm * (d // LANES), LANES), row),
                   pl.BlockSpec((tm, LANES), row), pl.BlockSpec((8, LANES), const)],
        scratch_shapes=[pltpu.VMEM((1, LANES), F32)],
        compiler_params=_cparams(("arbitrary",)),
        name="out_proj_router",
    )(o_rp, o_ap, xp.reshape(n_p // tp, tp, d), mod_p,
      o_rs, o_as, xs.reshape(n_s // ts, ts, d), mod_s, g2, w_out_b, rw, rb, tri)


ROW_PITCH = 24
MOE_GATHER_BUFS = 4


def _moe_kernel(tok_ref, be_ref, nu_ref,
                h2_hbm, wg_ref, wu_ref, wd_ref, out_ref,
                xbuf, xb, hb, wgb, wub, wdb, gsem, *, n_blk):
    i = pl.program_id(0)
    n_used = nu_ref[0]
    bm, d = xb.shape
    slabs = d // LANES
    de = wgb.shape[1]
    n_up = de // MXU_COLS
    n_down = d // MXU_COLS
    n_gbuf = xbuf.shape[0]
    gslot = lax.rem(i, n_gbuf)
    gnext = lax.rem(i + n_gbuf - 1, n_gbuf)

    def gather_copy(tok, r, s):
        src = pl.multiple_of(tok * slabs, slabs)
        return pltpu.make_async_copy(h2_hbm.at[pl.ds(src, slabs)],
                                     xbuf.at[s, pl.ds(r * ROW_PITCH, slabs)], gsem.at[s])

    def start_gather(blk, s, rows):
        for r in rows:
            gather_copy(tok_ref[blk * bm + r], r, s).start(priority=r % 2)

    def wait_gather(s):
        for r in range(bm):
            gather_copy(0, r, s).wait()

    @pl.when(i >= n_used)
    def _():
        out_ref[...] = jnp.zeros_like(out_ref)

    @pl.when(i < n_used)
    def _():
        @pl.when(i == 0)
        def _():
            for b in range(n_gbuf - 1):
                start_gather(min(b, n_blk - 1), b, range(bm))

        new_expert = jnp.logical_or(i == 0, be_ref[i] != be_ref[jnp.maximum(i - 1, 0)])

        @pl.when(new_expert)
        def _():
            wgb[...] = wg_ref[0].astype(BF16)
            wub[...] = wu_ref[0].astype(BF16)
            wdb[...] = wd_ref[0].astype(BF16)

        wait_gather(gslot)
        for j in range(slabs):
            xb[:, j * LANES:(j + 1) * LANES] = xbuf[gslot, pl.ds(j, bm, stride=ROW_PITCH), :].astype(BF16)
        nxt = jnp.minimum(i + n_gbuf - 1, n_blk - 1)

        def copy_batch(k, n_batches, base):
            per = (bm // 2) // n_batches
            start_gather(nxt, gnext, range(base + k * per, base + (k + 1) * per))

        for c in range(n_up):
            cols = slice(c * MXU_COLS, (c + 1) * MXU_COLS)
            hb[:, cols] = (_silu(_dot(xb[...], wgb[:, cols])) * _dot(xb[...], wub[:, cols])).astype(BF16)
            copy_batch(c, n_up, 0)

        for c in range(n_down):
            y = _dot(hb[...], wdb[:, c * MXU_COLS:(c + 1) * MXU_COLS])
            for jj in range(MXU_COLS // LANES):
                j = c * (MXU_COLS // LANES) + jj
                out_ref[pl.ds(j, bm, stride=slabs), :] = y[:, jj * LANES:(jj + 1) * LANES]
            copy_batch(c, n_down, bm // 2)

        @pl.when(i == n_used - 1)
        def _():
            for b in range(1, n_gbuf):
                wait_gather(lax.rem(i + b, n_gbuf))


def _moe(h2, route, counts, wg, wu, wd):
    n = route.shape[0]
    d = wg.shape[1]
    slabs = d // LANES
    assert h2.shape == (n * slabs, LANES)
    s_rows = n * TOP_K
    bm = MOE_ROWS
    n_blk = -(-s_rows // bm) + N_EXPERTS
    n_pad = n_blk * bm

    counts = counts[0, :N_EXPERTS].astype(jnp.int32)
    padded = (counts + bm - 1) // bm * bm
    pad_end = jnp.cumsum(padded)
    pad_start = pad_end - padded
    expert = route[:, 0:TOP_K].astype(jnp.int32)
    rank = route[:, 2 * TOP_K:3 * TOP_K].astype(jnp.int32)
    onehot = expert[:, :, None] == jnp.arange(N_EXPERTS, dtype=jnp.int32)[None, None, :]
    dest = jnp.sum(jnp.where(onehot, pad_start[None, None, :], 0), axis=2) + rank
    tok = jnp.broadcast_to(jnp.arange(n, dtype=jnp.int32)[:, None], (n, TOP_K))
    row_tok = jnp.zeros((n_pad,), jnp.int32).at[dest.reshape(-1)].set(tok.reshape(-1))
    blk_start = jnp.arange(n_blk, dtype=jnp.int32) * bm
    blk_e = jnp.minimum(jnp.sum((pad_end[None, :] <= blk_start[:, None]).astype(jnp.int32), axis=1),
                        N_EXPERTS - 1)
    n_used = (pad_end[-1] // bm).astype(jnp.int32).reshape(1)

    de = wg.shape[2]
    assert de % MXU_COLS == 0 and d % MXU_COLS == 0 and slabs <= ROW_PITCH
    assert (bm // 2) % (de // MXU_COLS) == 0 and (bm // 2) % (d // MXU_COLS) == 0
    wmap = lambda i, tok, be, nu: (be[i], 0, 0)
    grid_spec = pltpu.PrefetchScalarGridSpec(
        num_scalar_prefetch=3,
        grid=(n_blk,),
        in_specs=[pl.BlockSpec(memory_space=pl.ANY),
                  pl.BlockSpec((1, d, de), wmap),
                  pl.BlockSpec((1, d, de), wmap),
                  pl.BlockSpec((1, de, d), wmap)],
        out_specs=pl.BlockSpec((bm * slabs, LANES), lambda i, tok, be, nu: (i, 0)),
        scratch_shapes=[pltpu.VMEM((MOE_GATHER_BUFS, bm * ROW_PITCH, LANES), F32),
                        pltpu.VMEM((bm, d), BF16), pltpu.VMEM((bm, de), BF16),
                        pltpu.VMEM((d, de), BF16), pltpu.VMEM((d, de), BF16),
                        pltpu.VMEM((de, d), BF16),
                        pltpu.SemaphoreType.DMA((MOE_GATHER_BUFS,))])
    rows = pl.pallas_call(
        functools.partial(_moe_kernel, n_blk=n_blk),
        out_shape=jax.ShapeDtypeStruct((n_pad * slabs, LANES), F32),
        grid_spec=grid_spec,
        compiler_params=_cparams(("arbitrary",)),
        name="moe_experts",
    )(row_tok, blk_e, n_used, h2, wg, wu, wd)
    return rows, dest


def _final_kernel(dest_ref, x1_ref, rows_hbm, route_ref, mod_ref, y_ref, rbuf, sem, *, tok_off, n_steps):
    i = pl.program_id(0)
    bb, t, d = x1_ref.shape
    tm = bb * t
    slabs = d // LANES
    slot = lax.rem(i, 2)

    def row_copy(pos, r, k, s):
        src = pl.multiple_of(pos * slabs, slabs)
        return pltpu.make_async_copy(rows_hbm.at[pl.ds(src, slabs)],
                                     rbuf.at[s, k, pl.ds(r * ROW_PITCH, slabs)], sem.at[s])

    def start_rows(step, s):
        for r in range(tm):
            for k in range(TOP_K):
                pos = dest_ref[(tok_off + step * tm + r) * TOP_K + k]
                row_copy(pos, r, k, s).start(priority=(r + k) % 2)

    def wait_rows(s):
        for r in range(tm):
            for k in range(TOP_K):
                row_copy(0, r, k, s).wait()

    @pl.when(i == 0)
    def _():
        start_rows(0, 0)

    @pl.when(i + 1 < n_steps)
    def _():
        start_rows(i + 1, 1 - slot)

    wait_rows(slot)
    w = route_ref[...]
    g_f = mod_ref[...][:, 5:6, :]
    for j in range(slabs):
        cols = slice(j * LANES, (j + 1) * LANES)
        moe = None
        for k in range(TOP_K):
            rk = rbuf[slot, k, pl.ds(j, tm, stride=ROW_PITCH), :].reshape(bb, t, LANES)
            term = w[:, :, TOP_K + k:TOP_K + k + 1] * rk
            moe = term if moe is None else moe + term
        y_ref[:, :, cols] = x1_ref[:, :, cols] + g_f[:, :, cols] * moe


def _final(x1, rows, dest, route, mod, batch, seq, row_off):
    n_all, d = x1.shape
    n = batch * seq
    bb, t = _row_tiling(batch, seq, FINAL_TILE)
    tm = bb * t
    tiles_per_mod = (seq // t) if bb == 1 else 1
    assert row_off % tm == 0 and n % tm == 0
    off = row_off // tm
    n_steps = n // tm
    grid_spec = pltpu.PrefetchScalarGridSpec(
        num_scalar_prefetch=1,
        grid=(n_steps,),
        in_specs=[pl.BlockSpec((bb, t, d), lambda i, dst: (i + off, 0, 0)),
                  pl.BlockSpec(memory_space=pl.ANY),
                  pl.BlockSpec((bb, t, LANES), lambda i, dst: (i + off, 0, 0)),
                  pl.BlockSpec((bb, 6, d), lambda i, dst: (i // tiles_per_mod, 0, 0))],
        out_specs=pl.BlockSpec((bb, t, d), lambda i, dst: (i, 0, 0)),
        scratch_shapes=[pltpu.VMEM((2, TOP_K, tm * ROW_PITCH, LANES), F32),
                        pltpu.SemaphoreType.DMA((2,))])
    return pl.pallas_call(
        functools.partial(_final_kernel, tok_off=row_off, n_steps=n_steps),
        out_shape=jax.ShapeDtypeStruct((n // t, t, d), F32),
        grid_spec=grid_spec,
        compiler_params=_cparams(("arbitrary",)),
        name="final_residual",
    )(dest.reshape(-1), x1.reshape(-1, t, d), rows, route.reshape(-1, t, LANES), mod).reshape(batch, seq, d)


def _mixer_group(x, mod, past_k, past_v, s0, lyr):
    batch, seq, d = x.shape
    q, f, i, g, qa, kf, kb, vf, vb = _in_proj(x, mod, lyr["g1"], lyr["w_in"], lyr["qg"], lyr["kg"],
                                               lyr["bd"])
    o_r, s_new = _hgrn(q, f, i, g, s0, lyr["lb"], lyr["og"], batch, seq)
    if past_k is None:
        o_a = _attention_prompt(qa, kb, vb, batch, seq, lyr["lam_params"], lyr["sg"], lyr["lam_init"])
    else:
        o_a = _attention_cached(qa, kb, vb, batch, seq, past_k, past_v, lyr["lam_params"], lyr["sg"],
                                lyr["lam_init"])
    k_new = kf.reshape(batch, seq, N_HEADS, HEAD_DIM)
    v_new = vf.reshape(batch, seq, N_HEADS, HEAD_DIM)
    return o_r, o_a, k_new, v_new, s_new


def kernel(x_prompt, x_sample, cache_k_attn, cache_v_attn, state_hgrn, c_prompt, c_sample, w_ada, b_ada, norm1_g, norm2_g, w_in, w_out, hgrn_lb_param, hgrn_onorm_g, attn_qnorm_g, attn_knorm_g, lambda_q1, lambda_k1, lambda_q2, lambda_k2, attn_subln_g, router_group_w, router_group_b, router_expert_w, router_expert_b, expert_w_gate, expert_w_up, expert_w_down):
    depth = w_ada.shape[0]
    bp, tp, d = x_prompt.shape
    bs, ts, _ = x_sample.shape
    d_h = N_HEADS * HEAD_DIM
    lb_all = jnp.cumsum(jax.nn.softmax(hgrn_lb_param.astype(F32), axis=0), axis=0)
    bd = jnp.asarray(np.kron(np.eye(LANES // D_QK), np.ones((D_QK, D_QK))), BF16)

    xp, xs = x_prompt, x_sample
    kp, vp, sp, ksm, vsm, ssm = [], [], [], [], [], []
    for l in range(depth):
        w_in_b = _prep_w_in(w_in[l], 4 * d_h)
        rw = jnp.zeros((d, LANES), F32)
        rw = rw.at[:, :N_EXPERTS].set(router_expert_w[l])
        rw = rw.at[:, N_EXPERTS:N_EXPERTS + N_GROUPS].set(router_group_w[l])
        rb = jnp.zeros((1, LANES), F32)
        rb = rb.at[0, :N_EXPERTS].set(router_expert_b[l])
        rb = rb.at[0, N_EXPERTS:N_EXPERTS + N_GROUPS].set(router_group_b[l])
        lyr = dict(
            g1=norm1_g[l].reshape(1, d), g2=norm2_g[l].reshape(1, d),
            w_in=w_in_b, w_out=w_out[l].astype(BF16),
            qg=jnp.tile(attn_qnorm_g[l], 2).reshape(1, LANES),
            kg=jnp.tile(attn_knorm_g[l], 2).reshape(1, LANES),
            bd=bd, lb=lb_all[l].reshape(1, d_h), og=hgrn_onorm_g[l].reshape(1, HEAD_DIM),
            sg=attn_subln_g[l].reshape(1, HEAD_DIM),
            lam_params=[p[l].reshape(1, D_QK) for p in (lambda_q1, lambda_k1, lambda_q2, lambda_k2)],
            lam_init=0.8 - 0.6 * math.exp(-0.3 * l),
            rw=rw, rb=rb)

        c_all = jnp.concatenate([c_prompt, c_sample], axis=0)
        rows = c_all.shape[0]
        rows_pad = -(-rows // 8) * 8
        mod = _ada(jnp.pad(c_all, ((0, rows_pad - rows), (0, 0))), w_ada[l], b_ada[l])
        mod_p = mod[:bp].reshape(bp, 6, d)
        mod_s = mod[bp:bp + bs].reshape(bs, 6, d)

        s0_p = jnp.zeros((bp, N_HEADS, HEAD_DIM, HEAD_DIM), F32)
        o_rp, o_ap, k_n, v_n, s_n = _mixer_group(xp, mod_p, None, None, s0_p, lyr)
        kp.append(k_n)
        vp.append(v_n)
        sp.append(s_n)
        o_rs, o_as, k_n, v_n, s_n = _mixer_group(xs, mod_s, cache_k_attn[l], cache_v_attn[l],
                                                 state_hgrn[l], lyr)
        ksm.append(k_n)
        vsm.append(v_n)
        ssm.append(s_n)

        x1, h2, route, counts = _out_proj(o_rp, o_ap, xp, mod_p, o_rs, o_as, xs, mod_s,
                                          lyr["g2"], lyr["w_out"], lyr["rw"], lyr["rb"])
        rows, dest = _moe(h2, route, counts, expert_w_gate[l], expert_w_up[l], expert_w_down[l])
        xp = _final(x1, rows, dest, route, mod_p, bp, tp, 0)
        xs = _final(x1, rows, dest, route, mod_s, bs, ts, bp * tp)
    return (xp, xs, jnp.stack(kp), jnp.stack(vp), jnp.stack(sp),
            jnp.stack(ksm), jnp.stack(vsm), jnp.stack(ssm))
```
